```python
import math
import jax, jax.numpy as jnp
from jax import lax
import numpy as np

D_MODEL = 1024
BATCH = 4
SEQ = 4096
DEPTH = 2

CHUNK = 64
Q_BLOCK = 128
LRU_WIDTH = 256
LRU_BLOCKS = 4
LRU_BLOCK_W = LRU_WIDTH // LRU_BLOCKS
CONV_WIDTH = 4
RG_C = 8.0
DIFF_HEADS = 6
DIFF_HEAD_DIM = 32
DIFF_V_DIM = 2 * DIFF_HEAD_DIM
DIFF_QK_WIDTH = DIFF_HEADS * 2 * DIFF_HEAD_DIM
DIFF_WIDTH = DIFF_HEADS * DIFF_V_DIM
ALIBI_MAX_BIAS = 8.0
MLA_HEADS = 6
MLA_Q_RANK = 192
MLA_KV_RANK = 128
MLA_NOPE_DIM = 64
MLA_ROPE_DIM = 32
MLA_V_DIM = 64
MLA_QK_DIM = MLA_NOPE_DIM + MLA_ROPE_DIM
MLA_WIDTH = MLA_HEADS * MLA_V_DIM
ROPE_THETA = 10000.0
D_MIX = LRU_WIDTH + DIFF_WIDTH + MLA_WIDTH
IN_SIZES = (LRU_WIDTH, LRU_WIDTH, DIFF_QK_WIDTH, DIFF_QK_WIDTH, DIFF_WIDTH, MLA_Q_RANK, MLA_KV_RANK, MLA_ROPE_DIM)
D_IN = 2 * LRU_WIDTH + 2 * DIFF_QK_WIDTH + DIFF_WIDTH + MLA_Q_RANK + MLA_KV_RANK + MLA_ROPE_DIM
N_GROUPS = 4
EXPERTS_PER_GROUP = 8
N_EXPERTS = N_GROUPS * EXPERTS_PER_GROUP
TOP_K = 2
D_EXPERT = 256
ROW_BLOCK = 128
EPS = 1e-6

kernel_name = "hymba_lru_diffattn_mla_hmoe"


def rms_norm(x, g):
    xf = x.astype(jnp.float32)
    y = xf * lax.rsqrt(jnp.mean(xf * xf, axis=-1, keepdims=True) + EPS)
    return (y * g.astype(jnp.float32)).astype(x.dtype)


def rotary_tables(seq):
    half = MLA_ROPE_DIM // 2
    inv_freq = ROPE_THETA ** (-jnp.arange(half, dtype=jnp.float32) / half)
    ang = jnp.arange(seq, dtype=jnp.float32)[:, None] * inv_freq[None, :]
    return jnp.cos(ang), jnp.sin(ang)


def apply_rotary(x, cos, sin):
    half = x.shape[-1] // 2
    x1 = x[..., :half].astype(jnp.float32)
    x2 = x[..., half:].astype(jnp.float32)
    c = cos[None, :, None, :]
    s = sin[None, :, None, :]
    return jnp.concatenate([x1 * c - x2 * s, x1 * s + x2 * c], axis=-1).astype(x.dtype)


def chunk_allowed(q0, q1, kv_end):
    qpos = jnp.arange(q0, q1)
    kpos = jnp.arange(kv_end)
    return (kpos[None, :] // CHUNK) <= (qpos[:, None] // CHUNK)


def sweep_query_blocks(block_fn, seq):
    outs = []
    for q0 in range(0, seq, Q_BLOCK):
        q1 = q0 + Q_BLOCK
        kv_end = min(seq, -(-q1 // CHUNK) * CHUNK)
        outs.append(block_fn(q0, q1, kv_end))
    return jnp.concatenate(outs, axis=1)


def alibi_slopes(n_heads):
    return jnp.asarray([2.0 ** (-ALIBI_MAX_BIAS * (h + 1) / n_heads) for h in range(n_heads)], dtype=jnp.float32)


def rg_lru_mixer(u, gate, conv_w, conv_b, wa, ba, wx, bx, lam, out_g):
    B, S, W = u.shape
    xc = lax.conv_general_dilated(u, conv_w[:, None, :], window_strides=(1,), padding=[(CONV_WIDTH - 1, 0)],
                                  dimension_numbers=('NWC', 'WIO', 'NWC'), feature_group_count=W) + conv_b
    xb = xc.reshape(B, S, LRU_BLOCKS, LRU_BLOCK_W)
    r = jax.nn.sigmoid((jnp.einsum('bsni,nij->bsnj', xb, wa).reshape(B, S, W) + ba).astype(jnp.float32))
    i = jax.nn.sigmoid((jnp.einsum('bsni,nij->bsnj', xb, wx).reshape(B, S, W) + bx).astype(jnp.float32))
    log_a = -RG_C * r * jax.nn.softplus(-lam.astype(jnp.float32))
    a = jnp.exp(log_a)
    b = jnp.sqrt(-jnp.expm1(2.0 * log_a)) * i * xc.astype(jnp.float32)

    def combine(left, right):
        a1, b1 = left
        a2, b2 = right
        return a1 * a2, a2 * b1 + b2

    _, h = lax.associative_scan(combine, (a, b), axis=1)
    y = h * jax.nn.gelu(gate.astype(jnp.float32))
    return rms_norm(y.astype(u.dtype), out_g)


def diff_attention_mixer(q, k, v, q_g, k_g, lq1, lk1, lq2, lk2, sub_g, lambda_init):
    B, S, _ = q.shape
    q = rms_norm(q.reshape(B, S, DIFF_HEADS, 2, DIFF_HEAD_DIM), q_g)
    k = rms_norm(k.reshape(B, S, DIFF_HEADS, 2, DIFF_HEAD_DIM), k_g)
    v = v.reshape(B, S, DIFF_HEADS, DIFF_V_DIM)
    f32 = jnp.float32
    lam = (jnp.exp(jnp.sum(lq1.astype(f32) * lk1.astype(f32)))
           - jnp.exp(jnp.sum(lq2.astype(f32) * lk2.astype(f32))) + lambda_init)
    slopes = alibi_slopes(DIFF_HEADS)
    scale = DIFF_HEAD_DIM ** -0.5

    def block(q0, q1, kv_end):
        allowed = chunk_allowed(q0, q1, kv_end)
        dist = jnp.abs(jnp.arange(q0, q1)[:, None] - jnp.arange(kv_end)[None, :]).astype(f32)
        bias = jnp.where(allowed[None], -slopes[:, None, None] * dist[None], -jnp.inf)
        s = jnp.einsum('bqhmd,bkhmd->bmhqk', q[:, q0:q1], k[:, :kv_end]).astype(f32) * scale + bias
        p = jax.nn.softmax(s, axis=-1)
        p_diff = p[:, 0] - lam * p[:, 1]
        return jnp.einsum('bhqk,bkhd->bqhd', p_diff.astype(v.dtype), v[:, :kv_end])

    o = sweep_query_blocks(block, S)
    o = rms_norm(o, sub_g) * (1.0 - lambda_init)
    return o.reshape(B, S, DIFF_WIDTH)


def mla_mixer(c_q, c_kv, k_rope, cq_g, ckv_g, w_uq, w_ukv, q_g, k_g, out_g, cos, sin):
    B, S, _ = c_q.shape
    q = (rms_norm(c_q, cq_g) @ w_uq).reshape(B, S, MLA_HEADS, MLA_QK_DIM)
    kv = (rms_norm(c_kv, ckv_g) @ w_ukv).reshape(B, S, MLA_HEADS, MLA_NOPE_DIM + MLA_V_DIM)
    k_nope, v = kv[..., :MLA_NOPE_DIM], kv[..., MLA_NOPE_DIM:]
    k_r = jnp.broadcast_to(k_rope[:, :, None, :], (B, S, MLA_HEADS, MLA_ROPE_DIM))
    k = jnp.concatenate([k_nope, k_r], axis=-1)
    q = rms_norm(q, q_g)
    k = rms_norm(k, k_g)
    q = jnp.concatenate([q[..., :MLA_NOPE_DIM], apply_rotary(q[..., MLA_NOPE_DIM:], cos, sin)], axis=-1)
    k = jnp.concatenate([k[..., :MLA_NOPE_DIM], apply_rotary(k[..., MLA_NOPE_DIM:], cos, sin)], axis=-1)
    scale = MLA_QK_DIM ** -0.5

    def block(q0, q1, kv_end):
        allowed = chunk_allowed(q0, q1, kv_end)
        s = jnp.einsum('bqhd,bkhd->bhqk', q[:, q0:q1], k[:, :kv_end]).astype(jnp.float32) * scale
        p = jax.nn.softmax(jnp.where(allowed, s, -jnp.inf), axis=-1)
        return jnp.einsum('bhqk,bkhd->bqhd', p.astype(v.dtype), v[:, :kv_end])

    o = sweep_query_blocks(block, S)
    return rms_norm(o.reshape(B, S, MLA_WIDTH), out_g)


def hierarchical_moe(h, wg, bg, we, be, w1, w3, w2):
    B, S, D = h.shape
    N = B * S
    f32 = jnp.float32
    hf = h.reshape(N, D)
    g_logits = (hf @ wg).astype(f32) + bg.astype(f32)
    pg = jax.nn.softmax(g_logits, axis=-1)
    g_idx = jnp.argmax(g_logits, axis=-1)
    pg_top = jnp.take_along_axis(pg, g_idx[:, None], axis=-1)
    e_logits = ((hf @ we).astype(f32) + be.astype(f32)).reshape(N, N_GROUPS, EXPERTS_PER_GROUP)
    e_in = jnp.take_along_axis(e_logits, g_idx[:, None, None], axis=1)[:, 0]
    top_p, top_i = lax.top_k(jax.nn.softmax(e_in, axis=-1), TOP_K)
    gate = pg_top * top_p / jnp.sum(top_p, axis=-1, keepdims=True)
    expert_id = g_idx[:, None] * EXPERTS_PER_GROUP + top_i

    NK = N * TOP_K
    flat_e = expert_id.reshape(NK).astype(jnp.int32)
    flat_t = jnp.repeat(jnp.arange(N, dtype=jnp.int32), TOP_K)
    flat_w = gate.reshape(NK)
    order = jnp.argsort(flat_e)
    se, st, sw = flat_e[order], flat_t[order], flat_w[order]
    counts = jnp.bincount(flat_e, length=N_EXPERTS)
    starts = jnp.cumsum(counts) - counts
    padded = (counts + ROW_BLOCK - 1) // ROW_BLOCK * ROW_BLOCK
    pad_ends = jnp.cumsum(padded)
    pad_starts = pad_ends - padded
    dest = pad_starts[se] + (jnp.arange(NK, dtype=jnp.int32) - starts[se])
    n_rows = (-(-NK // ROW_BLOCK) + N_EXPERTS) * ROW_BLOCK
    n_blk = n_rows // ROW_BLOCK
    row_tok = jnp.full((n_rows,), N, jnp.int32).at[dest].set(st)
    row_w = jnp.zeros((n_rows,), hf.dtype).at[dest].set(sw.astype(hf.dtype))
    blk_expert = jnp.minimum(jnp.searchsorted(pad_ends, jnp.arange(n_blk, dtype=jnp.int32) * ROW_BLOCK, side='right'),
                             N_EXPERTS - 1)
    x_pad = jnp.concatenate([hf, jnp.zeros((1, D), hf.dtype)], axis=0)
    x_rows = x_pad[row_tok].reshape(n_blk, ROW_BLOCK, D)

    def expert_block(args):
        xb, e = args
        return (jax.nn.silu(xb @ w1[e]) * (xb @ w3[e])) @ w2[e]

    y_rows = lax.map(expert_block, (x_rows, blk_expert)).reshape(n_rows, D)
    y = jax.ops.segment_sum(y_rows * row_w[:, None], row_tok, num_segments=N + 1)[:N]
    return y.reshape(B, S, D)


def setup_inputs(seed: int = 0) -> dict:
    key = jax.random.key(seed)
    ks = jax.random.split(key, 40)
    counter = [0]
    f32 = jnp.float32
    L = DEPTH

    def nxt():
        k = ks[counter[0]]
        counter[0] += 1
        return k

    def normal(shape, scale):
        return jax.random.normal(nxt(), shape, f32) * scale

    def gain(width):
        return 1.0 + normal((L, width), 0.02)

    x = normal((BATCH, SEQ, D_MODEL), 1.0)
    norm1_g = gain(D_MODEL)
    w_in = normal((L, D_MODEL, D_IN), D_MODEL ** -0.5)
    lru_conv_w = normal((L, CONV_WIDTH, LRU_WIDTH), CONV_WIDTH ** -0.5)
    lru_conv_b = normal((L, LRU_WIDTH), 0.01)
    lru_wa = normal((L, LRU_BLOCKS, LRU_BLOCK_W, LRU_BLOCK_W), LRU_BLOCK_W ** -0.5)
    lru_ba = normal((L, LRU_WIDTH), 0.01)
    lru_wx = normal((L, LRU_BLOCKS, LRU_BLOCK_W, LRU_BLOCK_W), LRU_BLOCK_W ** -0.5)
    lru_bx = normal((L, LRU_WIDTH), 0.01)
    u = jax.random.uniform(nxt(), (L, LRU_WIDTH), f32, minval=0.9, maxval=0.999)
    s = u ** (1.0 / RG_C)
    lru_lambda = jnp.log(s) - jnp.log1p(-s)
    lru_out_g = gain(LRU_WIDTH)
    diff_q_g = gain(DIFF_HEAD_DIM)
    diff_k_g = gain(DIFF_HEAD_DIM)
    diff_lq1 = normal((L, DIFF_HEAD_DIM), 0.1)
    diff_lk1 = normal((L, DIFF_HEAD_DIM), 0.1)
    diff_lq2 = normal((L, DIFF_HEAD_DIM), 0.1)
    diff_lk2 = normal((L, DIFF_HEAD_DIM), 0.1)
    diff_sub_g = gain(DIFF_V_DIM)
    mla_cq_g = gain(MLA_Q_RANK)
    mla_ckv_g = gain(MLA_KV_RANK)
    mla_w_uq = normal((L, MLA_Q_RANK, MLA_HEADS * MLA_QK_DIM), MLA_Q_RANK ** -0.5)
    mla_w_ukv = normal((L, MLA_KV_RANK, MLA_HEADS * (MLA_NOPE_DIM + MLA_V_DIM)), MLA_KV_RANK ** -0.5)
    mla_q_g = gain(MLA_QK_DIM)
    mla_k_g = gain(MLA_QK_DIM)
    mla_out_g = gain(MLA_WIDTH)
    w_out = normal((L, D_MIX, D_MODEL), D_MIX ** -0.5)
    norm2_g = gain(D_MODEL)
    router_g_w = normal((L, D_MODEL, N_GROUPS), D_MODEL ** -0.5)
    router_g_b = normal((L, N_GROUPS), 0.01)
    router_e_w = normal((L, D_MODEL, N_EXPERTS), D_MODEL ** -0.5)
    router_e_b = normal((L, N_EXPERTS), 0.01)
    exp_w1 = normal((L, N_EXPERTS, D_MODEL, D_EXPERT), D_MODEL ** -0.5)
    exp_w3 = normal((L, N_EXPERTS, D_MODEL, D_EXPERT), D_MODEL ** -0.5)
    exp_w2 = normal((L, N_EXPERTS, D_EXPERT, D_MODEL), D_EXPERT ** -0.5)
    return {"x": x, "norm1_g": norm1_g, "w_in": w_in, "lru_conv_w": lru_conv_w, "lru_conv_b": lru_conv_b,
            "lru_wa": lru_wa, "lru_ba": lru_ba, "lru_wx": lru_wx, "lru_bx": lru_bx, "lru_lambda": lru_lambda,
            "lru_out_g": lru_out_g, "diff_q_g": diff_q_g, "diff_k_g": diff_k_g, "diff_lq1": diff_lq1,
            "diff_lk1": diff_lk1, "diff_lq2": diff_lq2, "diff_lk2": diff_lk2, "diff_sub_g": diff_sub_g,
            "mla_cq_g": mla_cq_g, "mla_ckv_g": mla_ckv_g, "mla_w_uq": mla_w_uq, "mla_w_ukv": mla_w_ukv,
            "mla_q_g": mla_q_g, "mla_k_g": mla_k_g, "mla_out_g": mla_out_g, "w_out": w_out, "norm2_g": norm2_g,
            "router_g_w": router_g_w, "router_g_b": router_g_b, "router_e_w": router_e_w, "router_e_b": router_e_b,
            "exp_w1": exp_w1, "exp_w3": exp_w3, "exp_w2": exp_w2}


def reference(x, norm1_g, w_in, lru_conv_w, lru_conv_b, lru_wa, lru_ba, lru_wx, lru_bx, lru_lambda, lru_out_g,
              diff_q_g, diff_k_g, diff_lq1, diff_lk1, diff_lq2, diff_lk2, diff_sub_g,
              mla_cq_g, mla_ckv_g, mla_w_uq, mla_w_ukv, mla_q_g, mla_k_g, mla_out_g, w_out, norm2_g,
              router_g_w, router_g_b, router_e_w, router_e_b, exp_w1, exp_w3, exp_w2):
    B, S, _ = x.shape
    cos, sin = rotary_tables(S)
    split_at = [int(v) for v in np.cumsum(IN_SIZES)[:-1]]
    for l in range(DEPTH):
        hn = rms_norm(x, norm1_g[l])
        proj = hn @ w_in[l]
        lru_x, lru_gate, dq, dk, dv, c_q, c_kv, k_rope = jnp.split(proj, split_at, axis=-1)
        y_lru = rg_lru_mixer(lru_x, lru_gate, lru_conv_w[l], lru_conv_b[l], lru_wa[l], lru_ba[l],
                             lru_wx[l], lru_bx[l], lru_lambda[l], lru_out_g[l])
        lambda_init = 0.8 - 0.6 * math.exp(-0.3 * l)
        y_diff = diff_attention_mixer(dq, dk, dv, diff_q_g[l], diff_k_g[l], diff_lq1[l], diff_lk1[l],
                                      diff_lq2[l], diff_lk2[l], diff_sub_g[l], lambda_init)
        y_mla = mla_mixer(c_q, c_kv, k_rope, mla_cq_g[l], mla_ckv_g[l], mla_w_uq[l], mla_w_ukv[l],
                          mla_q_g[l], mla_k_g[l], mla_out_g[l], cos, sin)
        x = x + jnp.concatenate([y_lru, y_diff, y_mla], axis=-1) @ w_out[l]
        x = x + hierarchical_moe(rms_norm(x, norm2_g[l]), router_g_w[l], router_g_b[l], router_e_w[l],
                                 router_e_b[l], exp_w1[l], exp_w3[l], exp_w2[l])
    return x
```

```python
import functools
import math

import jax
import jax.numpy as jnp
import numpy as np
from jax import lax
from jax.experimental import pallas as pl
from jax.experimental.pallas import tpu as pltpu

F32 = jnp.float32
BF16 = jnp.bfloat16

D_MODEL = 1024
CHUNK = 64
LRU_WIDTH = 256
CONV_WIDTH = 4
RG_C = 8.0
DIFF_HEADS = 6
DIFF_HEAD_DIM = 32
DIFF_V_DIM = 64
DIFF_QK_WIDTH = DIFF_HEADS * 2 * DIFF_HEAD_DIM
DIFF_WIDTH = DIFF_HEADS * DIFF_V_DIM
ALIBI_MAX_BIAS = 8.0
MLA_HEADS = 6
MLA_Q_RANK = 192
MLA_KV_RANK = 128
MLA_NOPE_DIM = 64
MLA_ROPE_DIM = 32
MLA_V_DIM = 64
MLA_QK_DIM = MLA_NOPE_DIM + MLA_ROPE_DIM
MLA_WIDTH = MLA_HEADS * MLA_V_DIM
ROPE_THETA = 10000.0
N_GROUPS = 4
EXPERTS_PER_GROUP = 8
N_EXPERTS = N_GROUPS * EXPERTS_PER_GROUP
D_EXPERT = 256
ROW_BLOCK = 128
EPS = 1e-6

LANES = 128
MLA_HEAD_PAD = LANES
MLA_QK_PAD = MLA_HEADS * MLA_HEAD_PAD
VMEM_LIMIT = 56 * 1024 * 1024
NEG_BIG = -1e30

C_LRU = 0
C_DQ = 512
C_DK = 896
C_DV = 1280
C_CQ = 1664
C_CKV = 1920
C_KR = 2048
D_IN_PAD = 2176


def _cparams(sem):
    return pltpu.CompilerParams(dimension_semantics=sem, vmem_limit_bytes=VMEM_LIMIT)


def _rms(v, width):
    return v * lax.rsqrt(jnp.sum(v * v, axis=-1, keepdims=True) * (1.0 / width) + EPS)


def _dot(a, b):
    return jnp.dot(a, b, preferred_element_type=F32)


def _split_bf16(v):
    hi = v.astype(BF16)
    lo = (v - hi.astype(F32)).astype(BF16)
    return hi, lo


def _in_proj_kernel(x_ref, g1_ref, w_ref, gsum_ref, dqg_ref, dkg_ref, cqg_ref, wuq_ref, mqg_ref, ckvg_ref,
                    wk_ref, mkg_ref, wv_ref, cos_ref, sin_ref,
                    lru_ref, dq_ref, dk_ref, dv_ref, mq_ref, mk_ref, mv_ref):
    x = x_ref[0]
    hn = (_rms(x, D_MODEL) * g1_ref[...]).astype(BF16)
    p = _dot(hn, w_ref[...])
    lru_ref[0] = p[:, C_LRU:C_LRU + 2 * LRU_WIDTH]

    gsum = gsum_ref[...]

    def group_norm32(v, gain):
        hi, lo = _split_bf16(v * v)
        ss = _dot(hi, gsum) + _dot(lo, gsum)
        return v * lax.rsqrt(ss * (1.0 / DIFF_HEAD_DIM) + EPS) * gain

    dq_ref[0] = group_norm32(p[:, C_DQ:C_DQ + DIFF_QK_WIDTH], dqg_ref[...]).astype(BF16)
    dk_ref[0] = group_norm32(p[:, C_DK:C_DK + DIFF_QK_WIDTH], dkg_ref[...]).astype(BF16)
    dv_ref[0] = p[:, C_DV:C_DV + DIFF_WIDTH].astype(BF16)

    cos = cos_ref[...]
    sin = sin_ref[...]
    lane = lax.broadcasted_iota(jnp.int32, cos.shape, 1)
    first_half = lane < MLA_NOPE_DIM + MLA_ROPE_DIM // 2

    def head_norm_rotary(v, gain):
        outs = []
        for h in range(MLA_HEADS):
            c = v[:, h * MLA_HEAD_PAD:(h + 1) * MLA_HEAD_PAD]
            c = _rms(c, MLA_QK_DIM) * gain[:, h * MLA_HEAD_PAD:(h + 1) * MLA_HEAD_PAD]
            swapped = jnp.where(first_half, pltpu.roll(c, MLA_HEAD_PAD - MLA_ROPE_DIM // 2, 1),
                                pltpu.roll(c, MLA_ROPE_DIM // 2, 1))
            outs.append((c * cos + swapped * sin).astype(BF16))
        return jnp.concatenate(outs, axis=1)

    cq = (_rms(p[:, C_CQ:C_CQ + MLA_Q_RANK], MLA_Q_RANK) * cqg_ref[...]).astype(BF16)
    mq_ref[0] = head_norm_rotary(_dot(cq, wuq_ref[...]), mqg_ref[...])

    ckv = (_rms(p[:, C_CKV:C_CKV + MLA_KV_RANK], MLA_KV_RANK) * ckvg_ref[...]).astype(BF16)
    kcat = jnp.concatenate([ckv, p[:, C_KR:C_KR + MLA_ROPE_DIM].astype(BF16)], axis=1)
    mk_ref[0] = head_norm_rotary(_dot(kcat, wk_ref[...]), mkg_ref[...])
    mv_ref[0] = _dot(ckv, wv_ref[...]).astype(BF16)


def _in_proj(x, g1, w, gsum, dqg, dkg, cqg, wuq, mqg, ckvg, wk, mkg, wv, cos_t, sin_t, ts):
    B, S, _ = x.shape
    full = lambda a: pl.BlockSpec(a.shape, lambda b, i: (0,) * a.ndim)
    tok = lambda width: pl.BlockSpec((1, ts, width), lambda b, i: (b, i, 0))
    out_widths = (2 * LRU_WIDTH, DIFF_QK_WIDTH, DIFF_QK_WIDTH, DIFF_WIDTH, MLA_QK_PAD, MLA_QK_PAD, MLA_WIDTH)
    out_dtypes = (F32, BF16, BF16, BF16, BF16, BF16, BF16)
    return pl.pallas_call(
        _in_proj_kernel,
        grid=(B, S // ts),
        in_specs=[tok(D_MODEL), full(g1), full(w), full(gsum), full(dqg), full(dkg), full(cqg), full(wuq),
                  full(mqg), full(ckvg), full(wk), full(mkg), full(wv),
                  pl.BlockSpec((ts, LANES), lambda b, i: (i, 0)), pl.BlockSpec((ts, LANES), lambda b, i: (i, 0))],
        out_specs=[tok(wd) for wd in out_widths],
        out_shape=[jax.ShapeDtypeStruct((B, S, wd), dt) for wd, dt in zip(out_widths, out_dtypes)],
        compiler_params=_cparams(("parallel", "parallel")),
        name="in_proj",
    )(x, g1, w, gsum, dqg, dkg, cqg, wuq, mqg, ckvg, wk, mkg, wv, cos_t, sin_t)


LRU_HALO = 8


def _lru_kernel(blk_ref, cw_ref, cb_ref, wa_ref, ba_ref, wx_ref, bx_ref, lam_ref, og_ref, y_ref,
                ubuf, a_scr, b_scr, h_scr):
    B, tc, _ = blk_ref.shape
    W = LRU_WIDTH

    @pl.when(pl.program_id(0) == 0)
    def _():
        ubuf[:, 0:LRU_HALO, :] = jnp.zeros((B, LRU_HALO, W), F32)
        h_scr[...] = jnp.zeros(h_scr.shape, F32)

    ubuf[:, LRU_HALO:, :] = blk_ref[:, :, 0:W]
    xc = cb_ref[...][None]
    for j in range(CONV_WIDTH):
        off = LRU_HALO - (CONV_WIDTH - 1) + j
        xc = xc + cw_ref[j:j + 1, :][None] * ubuf[:, off:off + tc, :]
    ubuf[:, 0:LRU_HALO, :] = ubuf[:, tc:tc + LRU_HALO, :]

    x2 = xc.reshape(B * tc, W)
    xb = x2.astype(BF16)
    r = jax.nn.sigmoid(_dot(xb, wa_ref[...]) + ba_ref[...])
    gi = jax.nn.sigmoid(_dot(xb, wx_ref[...]) + bx_ref[...])
    nl = -lam_ref[...]
    softplus = jnp.maximum(nl, 0.0) + jnp.log(1.0 + jnp.exp(-jnp.abs(nl)))
    a = jnp.exp(-RG_C * r * softplus)
    bb = jnp.sqrt(1.0 - a * a) * gi * x2
    a_scr[...] = a.reshape(B, tc, W)
    b_scr[...] = bb.reshape(B, tc, W)

    def step(t, h):
        h = a_scr[:, pl.ds(t, 1), :] * h + b_scr[:, pl.ds(t, 1), :]
        b_scr[:, pl.ds(t, 1), :] = h
        return h

    h_scr[...] = lax.fori_loop(0, tc, step, h_scr[...], unroll=8)

    gate = blk_ref[:, :, W:2 * W]
    gelu = 0.5 * gate * (1.0 + jnp.tanh(math.sqrt(2.0 / math.pi) * (gate + 0.044715 * gate * gate * gate)))
    y = b_scr[...] * gelu
    y_ref[...] = (_rms(y, W) * og_ref[...][None]).astype(y_ref.dtype)


def _lru(lru_in, cw, cb, wa, ba, wx, bx, lam, og, tc):
    B, S, _ = lru_in.shape
    W = LRU_WIDTH
    full = lambda a: pl.BlockSpec(a.shape, lambda i: (0,) * a.ndim)
    return pl.pallas_call(
        _lru_kernel,
        grid=(S // tc,),
        in_specs=[pl.BlockSpec((B, tc, 2 * W), lambda i: (0, i, 0)), full(cw), full(cb), full(wa), full(ba),
                  full(wx), full(bx), full(lam), full(og)],
        out_specs=pl.BlockSpec((B, tc, W), lambda i: (0, i, 0)),
        out_shape=jax.ShapeDtypeStruct((B, S, W), BF16),
        scratch_shapes=[pltpu.VMEM((B, tc + LRU_HALO, W), F32), pltpu.VMEM((B, tc, W), F32),
                        pltpu.VMEM((B, tc, W), F32), pltpu.VMEM((B, 1, W), F32)],
        compiler_params=_cparams(("arbitrary",)),
        name="lru",
    )(lru_in, cw, cb, wa, ba, wx, bx, lam, og)


def _flash_maps(q_list, k_ref, v_ref, k_lanes, bias_fn, tq, tk, m_scr, l_scr, acc_scr):
    qi = pl.program_id(2)
    n_maps = len(q_list)
    row = lax.broadcasted_iota(jnp.int32, (tq, tk), 0)
    col = lax.broadcasted_iota(jnp.int32, (tq, tk), 1)
    rel = (row - col).astype(F32)
    allowed = (col // CHUNK) <= (row // CHUNK)

    def tile(j, diag):
        start = pl.multiple_of(j * tk, tk)
        off = ((qi - j) * tk).astype(F32)
        v_t = v_ref[0, pl.ds(start, tk), :]
        for i in range(n_maps):
            k_t = k_ref[0, pl.ds(start, tk), k_lanes[i]]
            s = lax.dot_general(q_list[i], k_t, (((1,), (1,)), ((), ())), preferred_element_type=F32)
            b = bias_fn(i, rel, off, diag)
            if b is not None:
                s = s + b
            if diag:
                s = jnp.where(allowed, s, NEG_BIG)
                m_new = jnp.max(s, axis=-1, keepdims=True)
                p = jnp.exp(s - m_new)
                l_scr[i] = jnp.sum(p, axis=-1, keepdims=True)
                acc_scr[i] = _dot(p.astype(BF16), v_t)
            else:
                m_old = m_scr[i]
                m_new = jnp.maximum(m_old, jnp.max(s, axis=-1, keepdims=True))
                alpha = jnp.exp(m_old - m_new)
                p = jnp.exp(s - m_new)
                l_scr[i] = alpha * l_scr[i] + jnp.sum(p, axis=-1, keepdims=True)
                acc_scr[i] = alpha * acc_scr[i] + _dot(p.astype(BF16), v_t)
            m_scr[i] = m_new

    tile(qi, True)

    def body(j, carry):
        tile(j, False)
        return carry

    lax.fori_loop(0, qi, body, 0)


def _diff_attn_kernel(q_ref, k_ref, v_ref, slope_ref, lq1_ref, lk1_ref, lq2_ref, lk2_ref, subg_ref, o_ref,
                      m_scr, l_scr, acc_scr, *, lambda_init, tq, tk):
    q = q_ref[0]
    lane = lax.broadcasted_iota(jnp.int32, q.shape, 1)
    zero = jnp.zeros_like(q)
    q_list = [jnp.where(lane // DIFF_HEAD_DIM == i, q, zero) for i in range(4)]
    neg_slope = [-slope_ref[0, 0:1, hh:hh + 1] for hh in range(2)]

    def bias_fn(i, rel, off, diag):
        dist = jnp.abs(rel) if diag else rel + off
        return neg_slope[i // 2] * dist

    _flash_maps(q_list, k_ref, v_ref, [slice(None)] * 4, bias_fn, tq, tk, m_scr, l_scr, acc_scr)

    lam = (jnp.exp(jnp.sum(lq1_ref[...] * lk1_ref[...], axis=-1, keepdims=True))
           - jnp.exp(jnp.sum(lq2_ref[...] * lk2_ref[...], axis=-1, keepdims=True)) + lambda_init)
    olane = lax.broadcasted_iota(jnp.int32, (tq, LANES), 1)
    out = jnp.zeros((tq, LANES), F32)
    for hh in range(2):
        o = acc_scr[2 * hh] / l_scr[2 * hh] - lam * (acc_scr[2 * hh + 1] / l_scr[2 * hh + 1])
        mine = (olane // DIFF_V_DIM) == hh
        ms = jnp.sum(jnp.where(mine, o * o, 0.0), axis=-1, keepdims=True) * (1.0 / DIFF_V_DIM)
        out = jnp.where(mine, o * lax.rsqrt(ms + EPS), out)
    o_ref[0] = (out * subg_ref[...] * (1.0 - lambda_init)).astype(o_ref.dtype)


def _diff_attn(dq, dk, dv, slopes, lq1, lk1, lq2, lk2, subg, lambda_init, tq):
    B, S, _ = dq.shape
    tk = tq
    n_pairs = DIFF_HEADS // 2
    full = lambda a: pl.BlockSpec(a.shape, lambda b, p, i: (0,) * a.ndim)
    kern = functools.partial(_diff_attn_kernel, lambda_init=lambda_init, tq=tq, tk=tk)
    return pl.pallas_call(
        kern,
        grid=(B, n_pairs, S // tq),
        in_specs=[pl.BlockSpec((1, tq, LANES), lambda b, p, i: (b, i, p)),
                  pl.BlockSpec((1, S, LANES), lambda b, p, i: (b, 0, p)),
                  pl.BlockSpec((1, S, LANES), lambda b, p, i: (b, 0, p)),
                  pl.BlockSpec((1, 1, LANES), lambda b, p, i: (p, 0, 0)),
                  full(lq1), full(lk1), full(lq2), full(lk2), full(subg)],
        out_specs=pl.BlockSpec((1, tq, LANES), lambda b, p, i: (b, i, p)),
        out_shape=jax.ShapeDtypeStruct((B, S, DIFF_WIDTH), BF16),
        scratch_shapes=[pltpu.VMEM((4, tq, 1), F32), pltpu.VMEM((4, tq, 1), F32), pltpu.VMEM((4, tq, LANES), F32)],
        compiler_params=_cparams(("parallel", "parallel", "arbitrary")),
        name="diff_attn",
    )(dq, dk, dv, slopes, lq1, lk1, lq2, lk2, subg)


def _mla_attn_kernel(q_ref, k_ref, v_ref, o_ref, m_scr, l_scr, acc_scr, *, tq, tk):
    q = q_ref[0]
    q_list = [q[:, hh * MLA_HEAD_PAD:(hh + 1) * MLA_HEAD_PAD] for hh in range(2)]
    k_lanes = [slice(hh * MLA_HEAD_PAD, (hh + 1) * MLA_HEAD_PAD) for hh in range(2)]
    _flash_maps(q_list, k_ref, v_ref, k_lanes, lambda i, rel, off, diag: None, tq, tk, m_scr, l_scr, acc_scr)
    olane = lax.broadcasted_iota(jnp.int32, (tq, LANES), 1)
    o0 = acc_scr[0] / l_scr[0]
    o1 = acc_scr[1] / l_scr[1]
    o_ref[0] = jnp.where(olane < MLA_V_DIM, o0, o1).astype(o_ref.dtype)


def _mla_attn(mq, mk, mv, tq):
    B, S, _ = mq.shape
    tk = tq
    n_pairs = MLA_HEADS // 2
    kern = functools.partial(_mla_attn_kernel, tq=tq, tk=tk)
    return pl.pallas_call(
        kern,
        grid=(B, n_pairs, S // tq),
        in_specs=[pl.BlockSpec((1, tq, 2 * MLA_HEAD_PAD), lambda b, p, i: (b, i, p)),
                  pl.BlockSpec((1, S, 2 * MLA_HEAD_PAD), lambda b, p, i: (b, 0, p)),
                  pl.BlockSpec((1, S, LANES), lambda b, p, i: (b, 0, p))],
        out_specs=pl.BlockSpec((1, tq, LANES), lambda b, p, i: (b, i, p)),
        out_shape=jax.ShapeDtypeStruct((B, S, MLA_WIDTH), F32),
        scratch_shapes=[pltpu.VMEM((2, tq, 1), F32), pltpu.VMEM((2, tq, 1), F32), pltpu.VMEM((2, tq, LANES), F32)],
        compiler_params=_cparams(("parallel", "parallel", "arbitrary")),
        name="mla_attn",
    )(mq, mk, mv)


ROUTER_PAD = LANES


def _out_proj_kernel(x_ref, ylru_ref, ydiff_ref, ymla_ref, mlag_ref, wo_ref, g2_ref, wrh_ref, wrl_ref, rb_ref,
                     x1_ref, eid_ref, gate_ref, cnt_ref):
    @pl.when(pl.program_id(0) == 0)
    def _():
        cnt_ref[...] = jnp.zeros(cnt_ref.shape, F32)

    ymla = (_rms(ymla_ref[...], MLA_WIDTH) * mlag_ref[...]).astype(BF16)
    x1 = (x_ref[...]
          + _dot(ylru_ref[...], wo_ref[0:LRU_WIDTH, :])
          + _dot(ydiff_ref[...], wo_ref[LRU_WIDTH:LRU_WIDTH + DIFF_WIDTH, :])
          + _dot(ymla, wo_ref[LRU_WIDTH + DIFF_WIDTH:, :]))
    x1_ref[...] = x1

    h2 = _rms(x1, D_MODEL) * g2_ref[...]
    hi, lo = _split_bf16(h2)
    wrh = wrh_ref[...]
    logits = _dot(hi, wrh) + _dot(hi, wrl_ref[...]) + _dot(lo, wrh) + rb_ref[...]

    lane = lax.broadcasted_iota(jnp.int32, logits.shape, 1)
    big = jnp.int32(1 << 20)
    is_group = lane < N_GROUPS
    gl = jnp.where(is_group, logits, NEG_BIG)
    gmax = jnp.max(gl, axis=-1, keepdims=True)
    g_idx = jnp.min(jnp.where(gl == gmax, lane, big), axis=-1, keepdims=True)
    pg_top = 1.0 / jnp.sum(jnp.where(is_group, jnp.exp(gl - gmax), 0.0), axis=-1, keepdims=True)

    e_lane = lane - N_GROUPS
    in_group = (e_lane >= 0) & (e_lane < N_EXPERTS) & ((e_lane // EXPERTS_PER_GROUP) == g_idx)
    el = jnp.where(in_group, logits, NEG_BIG)
    m1 = jnp.max(el, axis=-1, keepdims=True)
    i1 = jnp.min(jnp.where(el == m1, lane, big), axis=-1, keepdims=True)
    el2 = jnp.where(lane == i1, NEG_BIG, el)
    m2 = jnp.max(el2, axis=-1, keepdims=True)
    i2 = jnp.min(jnp.where(el2 == m2, lane, big), axis=-1, keepdims=True)
    e2 = jnp.exp(m2 - m1)
    g1 = pg_top / (1.0 + e2)
    g2 = pg_top * e2 / (1.0 + e2)

    two = lax.broadcasted_iota(jnp.int32, eid_ref.shape, 1)
    eid_ref[...] = jnp.where(two == 0, i1 - N_GROUPS, i2 - N_GROUPS)
    gate_ref[...] = jnp.where(two == 0, g1, g2)
    onehot = ((e_lane == i1 - N_GROUPS) | (e_lane == i2 - N_GROUPS)).astype(F32)
    cnt_ref[...] += jnp.sum(onehot, axis=0, keepdims=True)


def _out_proj(x, ylru, ydiff, ymla, mlag, wo, g2, wrh, wrl, rb, tm):
    N = x.shape[0]
    full = lambda a: pl.BlockSpec(a.shape, lambda i: (0,) * a.ndim)
    tok = lambda width: pl.BlockSpec((tm, width), lambda i: (i, 0))
    return pl.pallas_call(
        _out_proj_kernel,
        grid=(N // tm,),
        in_specs=[tok(D_MODEL), tok(LRU_WIDTH), tok(DIFF_WIDTH), tok(MLA_WIDTH), full(mlag), full(wo), full(g2),
                  full(wrh), full(wrl), full(rb)],
        out_specs=[tok(D_MODEL), tok(2), tok(2), pl.BlockSpec((1, ROUTER_PAD), lambda i: (0, 0))],
        out_shape=[jax.ShapeDtypeStruct((N, D_MODEL), F32), jax.ShapeDtypeStruct((N, 2), jnp.int32),
                   jax.ShapeDtypeStruct((N, 2), F32), jax.ShapeDtypeStruct((1, ROUTER_PAD), F32)],
        compiler_params=_cparams(("arbitrary",)),
        name="out_proj",
    )(x, ylru, ydiff, ymla, mlag, wo, g2, wrh, wrl, rb)


def _rank_kernel(eid_ref, start_ref, dest_ref, carry):
    tm = eid_ref.shape[0]

    @pl.when(pl.program_id(0) == 0)
    def _():
        carry[...] = start_ref[...]

    lane = lax.broadcasted_iota(jnp.int32, (tm, LANES), 1)
    r = lax.broadcasted_iota(jnp.int32, (tm, tm), 0)
    c = lax.broadcasted_iota(jnp.int32, (tm, tm), 1)
    lower = (c < r).astype(BF16)
    eid = eid_ref[...]
    dests = []
    for k in range(2):
        onehot = lane == eid[:, k:k + 1]
        before = _dot(lower, onehot.astype(BF16))
        base = carry[...]
        dests.append(jnp.sum(jnp.where(onehot, before + base, 0.0), axis=-1, keepdims=True))
        carry[...] = base + jnp.sum(onehot.astype(F32), axis=0, keepdims=True)
    two = lax.broadcasted_iota(jnp.int32, dest_ref.shape, 1)
    dest_ref[...] = jnp.where(two == 0, dests[0], dests[1]).astype(jnp.int32)


def _rank(eid, starts, tm):
    N = eid.shape[0]
    return pl.pallas_call(
        _rank_kernel,
        grid=(N // tm,),
        in_specs=[pl.BlockSpec((tm, 2), lambda i: (i, 0)), pl.BlockSpec((1, LANES), lambda i: (0, 0))],
        out_specs=pl.BlockSpec((tm, 2), lambda i: (i, 0)),
        out_shape=jax.ShapeDtypeStruct((N, 2), jnp.int32),
        scratch_shapes=[pltpu.VMEM((1, LANES), F32)],
        compiler_params=_cparams(("arbitrary",)),
        name="rank",
    )(eid, starts)


def _dispatch_kernel(dest_ref, x_hbm, rows_in_hbm, rows_hbm, sem, *, tm):
    del rows_in_hbm
    base = pl.program_id(0) * tm

    def row_copy(t, a):
        return pltpu.make_async_copy(x_hbm.at[pl.ds(t, 1)], rows_hbm.at[pl.ds(dest_ref[a], 1)], sem)

    def issue(t, carry):
        row_copy(base + t, 2 * (base + t)).start()
        row_copy(base + t, 2 * (base + t) + 1).start()
        return carry

    lax.fori_loop(0, tm, issue, 0, unroll=8)

    def drain(t, carry):
        row_copy(base, 2 * base).wait()
        row_copy(base, 2 * base).wait()
        return carry

    lax.fori_loop(0, tm, drain, 0, unroll=8)


def _dispatch(dest_flat, x1, rows_zero, tm):
    N = x1.shape[0]
    return pl.pallas_call(
        functools.partial(_dispatch_kernel, tm=tm),
        grid_spec=pltpu.PrefetchScalarGridSpec(
            num_scalar_prefetch=1,
            grid=(N // tm,),
            in_specs=[pl.BlockSpec(memory_space=pl.ANY), pl.BlockSpec(memory_space=pl.ANY)],
            out_specs=pl.BlockSpec(memory_space=pl.ANY),
            scratch_shapes=[pltpu.SemaphoreType.DMA(())],
        ),
        out_shape=jax.ShapeDtypeStruct(rows_zero.shape, rows_zero.dtype),
        input_output_aliases={2: 0},
        compiler_params=_cparams(("arbitrary",)),
        name="dispatch",
    )(dest_flat, x1, rows_zero)


def _expert_kernel(be_ref, nblk_ref, rows_ref, g2_ref, w1_ref, w3_ref, w2_ref, y_ref):
    del be_ref
    used = pl.program_id(0) < nblk_ref[0]

    @pl.when(used)
    def _():
        h = (_rms(rows_ref[...], D_MODEL) * g2_ref[...]).astype(BF16)
        a = _dot(h, w1_ref[0])
        b = _dot(h, w3_ref[0])
        z = (a * jax.nn.sigmoid(a) * b).astype(BF16)
        y_ref[...] = _dot(z, w2_ref[0])

    @pl.when(jnp.logical_not(used))
    def _():
        y_ref[...] = jnp.zeros(y_ref.shape, y_ref.dtype)


def _experts(blk_expert, n_used, rows, g2, w1, w3, w2):
    R = rows.shape[0]
    return pl.pallas_call(
        _expert_kernel,
        grid_spec=pltpu.PrefetchScalarGridSpec(
            num_scalar_prefetch=2,
            grid=(R // ROW_BLOCK,),
            in_specs=[pl.BlockSpec((ROW_BLOCK, D_MODEL), lambda i, be, nb: (i, 0)),
                      pl.BlockSpec(g2.shape, lambda i, be, nb: (0, 0)),
                      pl.BlockSpec((1, D_MODEL, D_EXPERT), lambda i, be, nb: (be[i], 0, 0)),
                      pl.BlockSpec((1, D_MODEL, D_EXPERT), lambda i, be, nb: (be[i], 0, 0)),
                      pl.BlockSpec((1, D_EXPERT, D_MODEL), lambda i, be, nb: (be[i], 0, 0))],
            out_specs=pl.BlockSpec((ROW_BLOCK, D_MODEL), lambda i, be, nb: (i, 0)),
        ),
        out_shape=jax.ShapeDtypeStruct((R, D_MODEL), F32),
        compiler_params=_cparams(("arbitrary",)),
        name="experts",
    )(blk_expert, n_used, rows, g2, w1, w3, w2)


def _combine_kernel(dest_ref, x1_ref, gate_ref, y_hbm, o_ref, buf, sem, *, tm):
    base = pl.program_id(0) * tm

    def row_copy(a, k, t):
        return pltpu.make_async_copy(y_hbm.at[pl.ds(dest_ref[a], 1)], buf.at[k, pl.ds(t, 1)], sem)

    def issue(t, carry):
        row_copy(2 * (base + t), 0, t).start()
        row_copy(2 * (base + t) + 1, 1, t).start()
        return carry

    lax.fori_loop(0, tm, issue, 0, unroll=8)

    def drain(t, carry):
        row_copy(2 * base, 0, 0).wait()
        row_copy(2 * base, 0, 0).wait()
        return carry

    lax.fori_loop(0, tm, drain, 0, unroll=8)
    gate = gate_ref[...]
    o_ref[...] = x1_ref[...] + gate[:, 0:1] * buf[0] + gate[:, 1:2] * buf[1]


def _combine(dest_flat, x1, gate, y_rows, tm):
    N = x1.shape[0]
    return pl.pallas_call(
        functools.partial(_combine_kernel, tm=tm),
        grid_spec=pltpu.PrefetchScalarGridSpec(
            num_scalar_prefetch=1,
            grid=(N // tm,),
            in_specs=[pl.BlockSpec((tm, D_MODEL), lambda i, d: (i, 0)), pl.BlockSpec((tm, 2), lambda i, d: (i, 0)),
                      pl.BlockSpec(memory_space=pl.ANY)],
            out_specs=pl.BlockSpec((tm, D_MODEL), lambda i, d: (i, 0)),
            scratch_shapes=[pltpu.VMEM((2, tm, D_MODEL), F32), pltpu.SemaphoreType.DMA(())],
        ),
        out_shape=jax.ShapeDtypeStruct((N, D_MODEL), F32),
        compiler_params=_cparams(("arbitrary",)),
        name="combine",
    )(dest_flat, x1, gate, y_rows)


def _block_diag(w):
    n, a, b = w.shape
    out = jnp.zeros((n * a, n * b), w.dtype)
    for i in range(n):
        out = out.at[i * a:(i + 1) * a, i * b:(i + 1) * b].set(w[i])
    return out


def _pad_heads(v, used):
    lead = v.shape[:-1]
    v = v.reshape(lead + (MLA_HEADS, used))
    v = jnp.pad(v, [(0, 0)] * len(lead) + [(0, 0), (0, MLA_HEAD_PAD - used)])
    return v.reshape(lead + (MLA_QK_PAD,))


def _rotary_lane_tables(seq):
    half = MLA_ROPE_DIM // 2
    inv_freq = ROPE_THETA ** (-jnp.arange(half, dtype=F32) / half)
    ang = jnp.arange(seq, dtype=F32)[:, None] * inv_freq[None, :]
    cos, sin = jnp.cos(ang), jnp.sin(ang)
    ones = jnp.ones((seq, MLA_NOPE_DIM), F32)
    tail = MLA_HEAD_PAD - MLA_QK_DIM
    cos_t = jnp.concatenate([ones, cos, cos, jnp.ones((seq, tail), F32)], axis=1)
    sin_t = jnp.concatenate([0.0 * ones, -sin, sin, jnp.zeros((seq, tail), F32)], axis=1)
    return cos_t, sin_t


def _pick_tile(n, pref):
    t = min(n, pref)
    while n % t:
        t //= 2
    return t


def kernel(x, norm1_g, w_in, lru_conv_w, lru_conv_b, lru_wa, lru_ba, lru_wx, lru_bx, lru_lambda, lru_out_g,
           diff_q_g, diff_k_g, diff_lq1, diff_lk1, diff_lq2, diff_lk2, diff_sub_g,
           mla_cq_g, mla_ckv_g, mla_w_uq, mla_w_ukv, mla_q_g, mla_k_g, mla_out_g, w_out, norm2_g,
           router_g_w, router_g_b, router_e_w, router_e_b, exp_w1, exp_w3, exp_w2):
    B, S, D = x.shape
    N = B * S
    depth = w_in.shape[0]
    ts = _pick_tile(S, 512)
    tq = _pick_tile(S, 256)
    tc = _pick_tile(S, 512)
    tm = _pick_tile(N, 512)
    tr = _pick_tile(N, 256)
    tg = _pick_tile(N, 256)

    cos_t, sin_t = _rotary_lane_tables(S)
    gsum = _block_diag(jnp.ones((DIFF_QK_WIDTH // DIFF_HEAD_DIM, DIFF_HEAD_DIM, DIFF_HEAD_DIM), F32)).astype(BF16)
    slopes = jnp.asarray([2.0 ** (-ALIBI_MAX_BIAS * (h + 1) / DIFF_HEADS) for h in range(DIFF_HEADS)], F32)
    slopes = jnp.pad(slopes.reshape(DIFF_HEADS // 2, 1, 2), ((0, 0), (0, 0), (0, LANES - 2)))
    row = lambda v: v.reshape(1, -1).astype(F32)

    n_rows = (-(-2 * N // ROW_BLOCK) + N_EXPERTS) * ROW_BLOCK
    n_blk = n_rows // ROW_BLOCK
    rows_zero = jnp.zeros((n_rows, D), F32)

    for l in range(depth):
        wl = w_in[l]
        zc = lambda n: jnp.zeros((D, n), F32)
        o = np.cumsum([0, 256, 256, 384, 384, 384, 192, 128, 32])
        w_cat = jnp.concatenate([wl[:, o[0]:o[5]], wl[:, o[5]:o[6]], zc(C_CKV - C_CQ - MLA_Q_RANK),
                                 wl[:, o[6]:o[7]], wl[:, o[7]:o[8]], zc(D_IN_PAD - C_KR - MLA_ROPE_DIM)],
                                axis=1).astype(BF16)
        d_scale = DIFF_HEAD_DIM ** -0.5
        dqg = row(jnp.tile(diff_q_g[l], DIFF_QK_WIDTH // DIFF_HEAD_DIM) * d_scale)
        dkg = row(jnp.tile(diff_k_g[l], DIFF_QK_WIDTH // DIFF_HEAD_DIM))
        wuq = _pad_heads(mla_w_uq[l], MLA_QK_DIM).astype(BF16)
        mqg = row(_pad_heads(jnp.tile(mla_q_g[l], MLA_HEADS) * MLA_QK_DIM ** -0.5, MLA_QK_DIM))
        mkg = row(_pad_heads(jnp.tile(mla_k_g[l], MLA_HEADS), MLA_QK_DIM))
        wukv = mla_w_ukv[l].reshape(MLA_KV_RANK, MLA_HEADS, MLA_NOPE_DIM + MLA_V_DIM)
        wk_nope = _pad_heads(wukv[:, :, :MLA_NOPE_DIM].reshape(MLA_KV_RANK, -1), MLA_NOPE_DIM)
        place = jnp.zeros((MLA_ROPE_DIM, MLA_HEADS, MLA_HEAD_PAD), F32)
        place = place.at[jnp.arange(MLA_ROPE_DIM), :, MLA_NOPE_DIM + jnp.arange(MLA_ROPE_DIM)].set(1.0)
        wk = jnp.concatenate([wk_nope, place.reshape(MLA_ROPE_DIM, MLA_QK_PAD)], axis=0).astype(BF16)
        wv = wukv[:, :, MLA_NOPE_DIM:].reshape(MLA_KV_RANK, MLA_WIDTH).astype(BF16)

        lru_in, dq, dk, dv, mq, mk, mv = _in_proj(
            x, row(norm1_g[l]), w_cat, gsum, dqg, dkg, row(mla_cq_g[l]), wuq, mqg, row(mla_ckv_g[l]), wk, mkg, wv,
            cos_t, sin_t, ts)

        y_lru = _lru(lru_in, lru_conv_w[l], row(lru_conv_b[l]), _block_diag(lru_wa[l]).astype(BF16),
                     row(lru_ba[l]), _block_diag(lru_wx[l]).astype(BF16), row(lru_bx[l]), row(lru_lambda[l]),
                     row(lru_out_g[l]), tc)

        lambda_init = 0.8 - 0.6 * math.exp(-0.3 * l)
        y_diff = _diff_attn(dq, dk, dv, slopes, row(diff_lq1[l]), row(diff_lk1[l]), row(diff_lq2[l]),
                            row(diff_lk2[l]), row(jnp.tile(diff_sub_g[l], 2)), lambda_init, tq)
        y_mla = _mla_attn(mq, mk, mv, tq)

        wr = jnp.concatenate([router_g_w[l], router_e_w[l],
                              jnp.zeros((D, ROUTER_PAD - N_GROUPS - N_EXPERTS), F32)], axis=1)
        wrh = wr.astype(BF16)
        wrl = (wr - wrh.astype(F32)).astype(BF16)
        rb = jnp.pad(jnp.concatenate([router_g_b[l], router_e_b[l]]), (0, ROUTER_PAD - N_GROUPS - N_EXPERTS))
        x1, eid, gate, cnt = _out_proj(
            x.reshape(N, D), y_lru.reshape(N, LRU_WIDTH), y_diff.reshape(N, DIFF_WIDTH), y_mla.reshape(N, MLA_WIDTH),
            row(mla_out_g[l]), w_out[l].astype(BF16), row(norm2_g[l]), wrh, wrl, row(rb), tm)

        counts = cnt[0, N_GROUPS:N_GROUPS + N_EXPERTS].astype(jnp.int32)
        padded = (counts + ROW_BLOCK - 1) // ROW_BLOCK * ROW_BLOCK
        pad_ends = jnp.cumsum(padded)
        pad_starts = pad_ends - padded
        blk_expert = jnp.minimum(
            jnp.searchsorted(pad_ends, jnp.arange(n_blk, dtype=jnp.int32) * ROW_BLOCK, side='right'),
            N_EXPERTS - 1).astype(jnp.int32)
        n_used = (pad_ends[-1:] // ROW_BLOCK).astype(jnp.int32)
        starts = jnp.pad(pad_starts.astype(F32), (0, LANES - N_EXPERTS)).reshape(1, LANES)

        dest = _rank(eid, starts, tr).reshape(2 * N)
        rows = _dispatch(dest, x1, rows_zero, tm)
        y_rows = _experts(blk_expert, n_used, rows, row(norm2_g[l]), exp_w1[l].astype(BF16),
                          exp_w3[l].astype(BF16), exp_w2[l].astype(BF16))
        x = _combine(dest, x1, gate, y_rows, tg).reshape(B, S, D)
    return x
```

```python
import functools
import math

import jax
import jax.numpy as jnp
import numpy as np
from jax import lax
from jax.experimental import pallas as pl
from jax.experimental.pallas import tpu as pltpu

F32 = jnp.float32
BF16 = jnp.bfloat16

D_MODEL = 1024
CHUNK = 64
LRU_WIDTH = 256
CONV_WIDTH = 4
RG_C = 8.0
DIFF_HEADS = 6
DIFF_HEAD_DIM = 32
DIFF_V_DIM = 64
DIFF_QK_WIDTH = DIFF_HEADS * 2 * DIFF_HEAD_DIM
DIFF_WIDTH = DIFF_HEADS * DIFF_V_DIM
ALIBI_MAX_BIAS = 8.0
MLA_HEADS = 6
MLA_Q_RANK = 192
MLA_KV_RANK = 128
MLA_NOPE_DIM = 64
MLA_ROPE_DIM = 32
MLA_V_DIM = 64
MLA_QK_DIM = MLA_NOPE_DIM + MLA_ROPE_DIM
MLA_WIDTH = MLA_HEADS * MLA_V_DIM
ROPE_THETA = 10000.0
N_GROUPS = 4
EXPERTS_PER_GROUP = 8
N_EXPERTS = N_GROUPS * EXPERTS_PER_GROUP
D_EXPERT = 256
ROW_BLOCK = 128
EPS = 1e-6

LANES = 128
MLA_HEAD_PAD = LANES
MLA_QK_PAD = MLA_HEADS * MLA_HEAD_PAD
VMEM_LIMIT = 56 * 1024 * 1024
NEG_BIG = -1e30
KEY_TILE = 256
LOG2E = math.log2(math.e)

C_LRU = 0
C_DQ = 512
C_DK = 896
C_DV = 1280
C_CQ = 1664
C_CKV = 1920
C_KR = 2048
D_IN_PAD = 2176


def _cparams(sem):
    return pltpu.CompilerParams(dimension_semantics=sem, vmem_limit_bytes=VMEM_LIMIT)


def _rms(v, width):
    return v * lax.rsqrt(jnp.sum(v * v, axis=-1, keepdims=True) * (1.0 / width) + EPS)


def _dot(a, b):
    return jnp.dot(a, b, preferred_element_type=F32)


def _split_bf16(v):
    hi = v.astype(BF16)
    lo = (v - hi.astype(F32)).astype(BF16)
    return hi, lo


def _in_proj_kernel(x_ref, g1_ref, w_ref, gsum_ref, dqg_ref, dkg_ref, cqg_ref, wuq_ref, mqg_ref, ckvg_ref,
                    wk_ref, mkg_ref, wv_ref, cos_ref, sin_ref,
                    lru_ref, dqt_ref, dk_ref, dvt_ref, mqt_ref, mk_ref, mvt_ref):
    x = x_ref[0]
    hn = (_rms(x, D_MODEL) * g1_ref[...]).astype(BF16)
    p = _dot(hn, w_ref[...])
    lru_ref[0] = p[:, C_LRU:C_LRU + 2 * LRU_WIDTH]

    gsum = gsum_ref[...]

    def group_norm32(v, gain):
        hi, lo = _split_bf16(v * v)
        ss = _dot(hi, gsum) + _dot(lo, gsum)
        return v * lax.rsqrt(ss * (1.0 / DIFF_HEAD_DIM) + EPS) * gain

    def store_tiles(ref, vt):
        for c in range(vt.shape[1] // KEY_TILE):
            ref[0, c] = vt[:, c * KEY_TILE:(c + 1) * KEY_TILE].astype(BF16)

    dqt_ref[0] = group_norm32(p[:, C_DQ:C_DQ + DIFF_QK_WIDTH], dqg_ref[...]).T.astype(BF16)
    dk_ref[0] = group_norm32(p[:, C_DK:C_DK + DIFF_QK_WIDTH], dkg_ref[...]).astype(BF16)
    store_tiles(dvt_ref, p[:, C_DV:C_DV + DIFF_WIDTH].T)

    cos = cos_ref[...]
    sin = sin_ref[...]
    lane = lax.broadcasted_iota(jnp.int32, cos.shape, 1)
    first_half = lane < MLA_NOPE_DIM + MLA_ROPE_DIM // 2

    def head_norm_rotary(v, gain):
        outs = []
        for h in range(MLA_HEADS):
            c = v[:, h * MLA_HEAD_PAD:(h + 1) * MLA_HEAD_PAD]
            c = _rms(c, MLA_QK_DIM) * gain[:, h * MLA_HEAD_PAD:(h + 1) * MLA_HEAD_PAD]
            swapped = jnp.where(first_half, pltpu.roll(c, MLA_HEAD_PAD - MLA_ROPE_DIM // 2, 1),
                                pltpu.roll(c, MLA_ROPE_DIM // 2, 1))
            outs.append(c * cos + swapped * sin)
        return jnp.concatenate(outs, axis=1)

    cq = (_rms(p[:, C_CQ:C_CQ + MLA_Q_RANK], MLA_Q_RANK) * cqg_ref[...]).astype(BF16)
    mqt_ref[0] = head_norm_rotary(_dot(cq, wuq_ref[...]), mqg_ref[...]).T.astype(BF16)

    ckv = (_rms(p[:, C_CKV:C_CKV + MLA_KV_RANK], MLA_KV_RANK) * ckvg_ref[...]).astype(BF16)
    kcat = jnp.concatenate([ckv, p[:, C_KR:C_KR + MLA_ROPE_DIM].astype(BF16)], axis=1)
    mk_ref[0] = head_norm_rotary(_dot(kcat, wk_ref[...]), mkg_ref[...]).astype(BF16)
    store_tiles(mvt_ref, _dot(ckv, wv_ref[...]).T)


def _in_proj(x, g1, w, gsum, dqg, dkg, cqg, wuq, mqg, ckvg, wk, mkg, wv, cos_t, sin_t, ts):
    B, S, _ = x.shape
    full = lambda a: pl.BlockSpec(a.shape, lambda b, i: (0,) * a.ndim)
    tok = lambda width: pl.BlockSpec((1, ts, width), lambda b, i: (b, i, 0))
    tok_t = lambda width: pl.BlockSpec((1, width, ts), lambda b, i: (b, 0, i))
    nkt = ts // KEY_TILE
    tiles_t = lambda width: pl.BlockSpec((1, nkt, width, KEY_TILE), lambda b, i: (b, i, 0, 0))
    shp = lambda shape, dt: jax.ShapeDtypeStruct(shape, dt)
    return pl.pallas_call(
        _in_proj_kernel,
        grid=(B, S // ts),
        in_specs=[tok(D_MODEL), full(g1), full(w), full(gsum), full(dqg), full(dkg), full(cqg), full(wuq),
                  full(mqg), full(ckvg), full(wk), full(mkg), full(wv),
                  pl.BlockSpec((ts, LANES), lambda b, i: (i, 0)), pl.BlockSpec((ts, LANES), lambda b, i: (i, 0))],
        out_specs=[tok(2 * LRU_WIDTH), tok_t(DIFF_QK_WIDTH), tok(DIFF_QK_WIDTH), tiles_t(DIFF_WIDTH),
                   tok_t(MLA_QK_PAD), tok(MLA_QK_PAD), tiles_t(MLA_WIDTH)],
        out_shape=[shp((B, S, 2 * LRU_WIDTH), F32), shp((B, DIFF_QK_WIDTH, S), BF16), shp((B, S, DIFF_QK_WIDTH), BF16),
                   shp((B, S // KEY_TILE, DIFF_WIDTH, KEY_TILE), BF16), shp((B, MLA_QK_PAD, S), BF16),
                   shp((B, S, MLA_QK_PAD), BF16), shp((B, S // KEY_TILE, MLA_WIDTH, KEY_TILE), BF16)],
        compiler_params=_cparams(("parallel", "parallel")),
        name="in_proj",
    )(x, g1, w, gsum, dqg, dkg, cqg, wuq, mqg, ckvg, wk, mkg, wv, cos_t, sin_t)


LRU_HALO = 8


def _lru_kernel(blk_ref, cw_ref, cb_ref, wa_ref, ba_ref, wx_ref, bx_ref, lam_ref, og_ref, y_ref,
                ubuf, a_scr, b_scr, h_scr):
    B, tc, _ = blk_ref.shape
    W = LRU_WIDTH

    @pl.when(pl.program_id(0) == 0)
    def _():
        ubuf[:, 0:LRU_HALO, :] = jnp.zeros((B, LRU_HALO, W), F32)
        h_scr[...] = jnp.zeros(h_scr.shape, F32)

    ubuf[:, LRU_HALO:, :] = blk_ref[:, :, 0:W]
    xc = cb_ref[...][None]
    for j in range(CONV_WIDTH):
        off = LRU_HALO - (CONV_WIDTH - 1) + j
        xc = xc + cw_ref[j:j + 1, :][None] * ubuf[:, off:off + tc, :]
    ubuf[:, 0:LRU_HALO, :] = ubuf[:, tc:tc + LRU_HALO, :]

    x2 = xc.reshape(B * tc, W)
    xb = x2.astype(BF16)
    r = jax.nn.sigmoid(_dot(xb, wa_ref[...]) + ba_ref[...])
    gi = jax.nn.sigmoid(_dot(xb, wx_ref[...]) + bx_ref[...])
    nl = -lam_ref[...]
    softplus = jnp.maximum(nl, 0.0) + jnp.log(1.0 + jnp.exp(-jnp.abs(nl)))
    a = jnp.exp(-RG_C * r * softplus)
    bb = jnp.sqrt(1.0 - a * a) * gi * x2
    a_scr[...] = a.reshape(B, tc, W)
    b_scr[...] = bb.reshape(B, tc, W)

    def step(t, h):
        h = a_scr[:, pl.ds(t, 1), :] * h + b_scr[:, pl.ds(t, 1), :]
        b_scr[:, pl.ds(t, 1), :] = h
        return h

    h_scr[...] = lax.fori_loop(0, tc, step, h_scr[...], unroll=8)

    gate = blk_ref[:, :, W:2 * W]
    gelu = 0.5 * gate * (1.0 + jnp.tanh(math.sqrt(2.0 / math.pi) * (gate + 0.044715 * gate * gate * gate)))
    y = b_scr[...] * gelu
    y_ref[...] = (_rms(y, W) * og_ref[...][None]).astype(y_ref.dtype)


def _lru(lru_in, cw, cb, wa, ba, wx, bx, lam, og, tc):
    B, S, _ = lru_in.shape
    W = LRU_WIDTH
    full = lambda a: pl.BlockSpec(a.shape, lambda i: (0,) * a.ndim)
    return pl.pallas_call(
        _lru_kernel,
        grid=(S // tc,),
        in_specs=[pl.BlockSpec((B, tc, 2 * W), lambda i: (0, i, 0)), full(cw), full(cb), full(wa), full(ba),
                  full(wx), full(bx), full(lam), full(og)],
        out_specs=pl.BlockSpec((B, tc, W), lambda i: (0, i, 0)),
        out_shape=jax.ShapeDtypeStruct((B, S, W), BF16),
        scratch_shapes=[pltpu.VMEM((B, tc + LRU_HALO, W), F32), pltpu.VMEM((B, tc, W), F32),
                        pltpu.VMEM((B, tc, W), F32), pltpu.VMEM((B, 1, W), F32)],
        compiler_params=_cparams(("arbitrary",)),
        name="lru",
    )(lru_in, cw, cb, wa, ba, wx, bx, lam, og)


def _flash_maps(qt_list, k_ref, vt_ref, k_lanes, diag_bias, past_bias, tile_shift, m_scr, l_scr, acc_scr):
    qi = pl.program_id(2)
    n_maps = len(qt_list)

    def scores(i, j):
        start = pl.multiple_of(j * KEY_TILE, KEY_TILE)
        return _dot(k_ref[0, pl.ds(start, KEY_TILE), k_lanes[i]], qt_list[i])

    vt = vt_ref[0, qi]
    for i in range(n_maps):
        s = scores(i, qi) + diag_bias[i]
        m = jnp.max(s, axis=0, keepdims=True)
        p = jnp.exp2(s - m)
        m_scr[i] = m
        l_scr[i] = jnp.sum(p, axis=0, keepdims=True)
        acc_scr[i] = _dot(vt, p.astype(BF16))

    def body(j, s_cur):
        vt = vt_ref[0, j]
        s_next = tuple(scores(i, j + 1) for i in range(n_maps))
        for i in range(n_maps):
            s = s_cur[i]
            if past_bias[i] is not None:
                s = s + past_bias[i]
            c = tile_shift(i, qi - j)
            m_old = m_scr[i]
            m_new = jnp.maximum(m_old, jnp.max(s, axis=0, keepdims=True) + c)
            alpha = jnp.exp2(m_old - m_new)
            p = jnp.exp2(s - (m_new - c))
            m_scr[i] = m_new
            l_scr[i] = alpha * l_scr[i] + jnp.sum(p, axis=0, keepdims=True)
            acc_scr[i] = alpha * acc_scr[i] + _dot(vt, p.astype(BF16))
        return s_next

    lax.fori_loop(0, qi, body, tuple(scores(i, 0) for i in range(n_maps)))


def _chunk_allowed_t(tq):
    krow = lax.broadcasted_iota(jnp.int32, (KEY_TILE, tq), 0)
    qcol = lax.broadcasted_iota(jnp.int32, (KEY_TILE, tq), 1)
    return krow, qcol, (krow // CHUNK) <= (qcol // CHUNK)


def _diff_attn_kernel(qt_ref, k_ref, vt_ref, slope_ref, lq1_ref, lk1_ref, lq2_ref, lk2_ref, subg_ref, o_ref,
                      m_scr, l_scr, acc_scr, *, lambda_init, tq):
    qt = qt_ref[0]
    feat = lax.broadcasted_iota(jnp.int32, qt.shape, 0)
    zero = jnp.zeros_like(qt)
    qt_list = [jnp.where(feat // DIFF_HEAD_DIM == i, qt, zero) for i in range(4)]
    slope = [slope_ref[0, 0:1, hh:hh + 1] for hh in range(2)]

    krow, qcol, allowed = _chunk_allowed_t(tq)
    kf, qf = krow.astype(F32), qcol.astype(F32)
    self_dist = qf - jnp.abs(qf - kf)
    diag_bias = [jnp.where(allowed, slope[i // 2] * self_dist, NEG_BIG) for i in range(4)]
    past_bias = [slope[i // 2] * kf for i in range(4)]

    def tile_shift(i, n_tiles):
        return -slope[i // 2] * (n_tiles * KEY_TILE).astype(F32)

    _flash_maps(qt_list, k_ref, vt_ref, [slice(None)] * 4, diag_bias, past_bias, tile_shift, m_scr, l_scr, acc_scr)

    lam = (jnp.exp(jnp.sum(lq1_ref[...] * lk1_ref[...], axis=-1, keepdims=True))
           - jnp.exp(jnp.sum(lq2_ref[...] * lk2_ref[...], axis=-1, keepdims=True)) + lambda_init)
    orow = lax.broadcasted_iota(jnp.int32, (LANES, tq), 0)
    out_t = jnp.zeros((LANES, tq), F32)
    for hh in range(2):
        o = acc_scr[2 * hh] / l_scr[2 * hh] - lam * (acc_scr[2 * hh + 1] / l_scr[2 * hh + 1])
        mine = (orow // DIFF_V_DIM) == hh
        ms = jnp.sum(jnp.where(mine, o * o, 0.0), axis=0, keepdims=True) * (1.0 / DIFF_V_DIM)
        out_t = jnp.where(mine, o * lax.rsqrt(ms + EPS), out_t)
    o_ref[0] = (out_t.T * subg_ref[...] * (1.0 - lambda_init)).astype(o_ref.dtype)


def _diff_attn(dqt, dk, dvt, slopes, lq1, lk1, lq2, lk2, subg, lambda_init):
    B, S, _ = dk.shape
    tq = KEY_TILE
    n_pairs = DIFF_HEADS // 2
    full = lambda a: pl.BlockSpec(a.shape, lambda b, p, i: (0,) * a.ndim)
    kern = functools.partial(_diff_attn_kernel, lambda_init=lambda_init, tq=tq)
    return pl.pallas_call(
        kern,
        grid=(B, n_pairs, S // tq),
        in_specs=[pl.BlockSpec((1, LANES, tq), lambda b, p, i: (b, p, i)),
                  pl.BlockSpec((1, S, LANES), lambda b, p, i: (b, 0, p)),
                  pl.BlockSpec((1, S // KEY_TILE, LANES, KEY_TILE), lambda b, p, i: (b, 0, p, 0)),
                  pl.BlockSpec((1, 1, LANES), lambda b, p, i: (p, 0, 0)),
                  full(lq1), full(lk1), full(lq2), full(lk2), full(subg)],
        out_specs=pl.BlockSpec((1, tq, LANES), lambda b, p, i: (b, i, p)),
        out_shape=jax.ShapeDtypeStruct((B, S, DIFF_WIDTH), BF16),
        scratch_shapes=[pltpu.VMEM((4, 1, tq), F32), pltpu.VMEM((4, 1, tq), F32), pltpu.VMEM((4, LANES, tq), F32)],
        compiler_params=_cparams(("parallel", "parallel", "arbitrary")),
        name="diff_attn",
    )(dqt, dk, dvt, slopes, lq1, lk1, lq2, lk2, subg)


def _mla_attn_kernel(qt_ref, k_ref, vt_ref, o_ref, m_scr, l_scr, acc_scr, *, tq):
    qt = qt_ref[0]
    qt_list = [qt[hh * MLA_HEAD_PAD:(hh + 1) * MLA_HEAD_PAD, :] for hh in range(2)]
    k_lanes = [slice(hh * MLA_HEAD_PAD, (hh + 1) * MLA_HEAD_PAD) for hh in range(2)]
    _, _, allowed = _chunk_allowed_t(tq)
    mask_bias = jnp.where(allowed, 0.0, NEG_BIG)
    _flash_maps(qt_list, k_ref, vt_ref, k_lanes, [mask_bias] * 2, [None] * 2, lambda i, n: 0.0,
                m_scr, l_scr, acc_scr)
    orow = lax.broadcasted_iota(jnp.int32, (LANES, tq), 0)
    out_t = jnp.where(orow < MLA_V_DIM, acc_scr[0] / l_scr[0], acc_scr[1] / l_scr[1])
    o_ref[0] = out_t.T.astype(o_ref.dtype)


def _mla_attn(mqt, mk, mvt):
    B, S, _ = mk.shape
    tq = KEY_TILE
    n_pairs = MLA_HEADS // 2
    kern = functools.partial(_mla_attn_kernel, tq=tq)
    return pl.pallas_call(
        kern,
        grid=(B, n_pairs, S // tq),
        in_specs=[pl.BlockSpec((1, 2 * MLA_HEAD_PAD, tq), lambda b, p, i: (b, p, i)),
                  pl.BlockSpec((1, S, 2 * MLA_HEAD_PAD), lambda b, p, i: (b, 0, p)),
                  pl.BlockSpec((1, S // KEY_TILE, LANES, KEY_TILE), lambda b, p, i: (b, 0, p, 0))],
        out_specs=pl.BlockSpec((1, tq, LANES), lambda b, p, i: (b, i, p)),
        out_shape=jax.ShapeDtypeStruct((B, S, MLA_WIDTH), F32),
        scratch_shapes=[pltpu.VMEM((2, 1, tq), F32), pltpu.VMEM((2, 1, tq), F32), pltpu.VMEM((2, LANES, tq), F32)],
        compiler_params=_cparams(("parallel", "parallel", "arbitrary")),
        name="mla_attn",
    )(mqt, mk, mvt)


ROUTER_PAD = LANES


def _out_proj_kernel(x_ref, ylru_ref, ydiff_ref, ymla_ref, mlag_ref, wo_ref, g2_ref, wrh_ref, wrl_ref, rb_ref,
                     x1_ref, eid_ref, gate_ref, cnt_ref):
    @pl.when(pl.program_id(0) == 0)
    def _():
        cnt_ref[...] = jnp.zeros(cnt_ref.shape, F32)

    ymla = (_rms(ymla_ref[...], MLA_WIDTH) * mlag_ref[...]).astype(BF16)
    x1 = (x_ref[...]
          + _dot(ylru_ref[...], wo_ref[0:LRU_WIDTH, :])
          + _dot(ydiff_ref[...], wo_ref[LRU_WIDTH:LRU_WIDTH + DIFF_WIDTH, :])
          + _dot(ymla, wo_ref[LRU_WIDTH + DIFF_WIDTH:, :]))
    x1_ref[...] = x1

    h2 = _rms(x1, D_MODEL) * g2_ref[...]
    hi, lo = _split_bf16(h2)
    wrh = wrh_ref[...]
    logits = _dot(hi, wrh) + _dot(hi, wrl_ref[...]) + _dot(lo, wrh) + rb_ref[...]

    lane = lax.broadcasted_iota(jnp.int32, logits.shape, 1)
    big = jnp.int32(1 << 20)
    is_group = lane < N_GROUPS
    gl = jnp.where(is_group, logits, NEG_BIG)
    gmax = jnp.max(gl, axis=-1, keepdims=True)
    g_idx = jnp.min(jnp.where(gl == gmax, lane, big), axis=-1, keepdims=True)
    pg_top = 1.0 / jnp.sum(jnp.where(is_group, jnp.exp(gl - gmax), 0.0), axis=-1, keepdims=True)

    e_lane = lane - N_GROUPS
    in_group = (e_lane >= 0) & (e_lane < N_EXPERTS) & ((e_lane // EXPERTS_PER_GROUP) == g_idx)
    el = jnp.where(in_group, logits, NEG_BIG)
    m1 = jnp.max(el, axis=-1, keepdims=True)
    i1 = jnp.min(jnp.where(el == m1, lane, big), axis=-1, keepdims=True)
    el2 = jnp.where(lane == i1, NEG_BIG, el)
    m2 = jnp.max(el2, axis=-1, keepdims=True)
    i2 = jnp.min(jnp.where(el2 == m2, lane, big), axis=-1, keepdims=True)
    e2 = jnp.exp(m2 - m1)
    g1 = pg_top / (1.0 + e2)
    g2 = pg_top * e2 / (1.0 + e2)

    two = lax.broadcasted_iota(jnp.int32, eid_ref.shape, 1)
    eid_ref[...] = jnp.where(two == 0, i1 - N_GROUPS, i2 - N_GROUPS)
    gate_ref[...] = jnp.where(two == 0, g1, g2)
    onehot = ((e_lane == i1 - N_GROUPS) | (e_lane == i2 - N_GROUPS)).astype(F32)
    cnt_ref[...] += jnp.sum(onehot, axis=0, keepdims=True)


def _out_proj(x, ylru, ydiff, ymla, mlag, wo, g2, wrh, wrl, rb, tm):
    N = x.shape[0]
    full = lambda a: pl.BlockSpec(a.shape, lambda i: (0,) * a.ndim)
    tok = lambda width: pl.BlockSpec((tm, width), lambda i: (i, 0))
    return pl.pallas_call(
        _out_proj_kernel,
        grid=(N // tm,),
        in_specs=[tok(D_MODEL), tok(LRU_WIDTH), tok(DIFF_WIDTH), tok(MLA_WIDTH), full(mlag), full(wo), full(g2),
                  full(wrh), full(wrl), full(rb)],
        out_specs=[tok(D_MODEL), tok(2), tok(2), pl.BlockSpec((1, ROUTER_PAD), lambda i: (0, 0))],
        out_shape=[jax.ShapeDtypeStruct((N, D_MODEL), F32), jax.ShapeDtypeStruct((N, 2), jnp.int32),
                   jax.ShapeDtypeStruct((N, 2), F32), jax.ShapeDtypeStruct((1, ROUTER_PAD), F32)],
        compiler_params=_cparams(("arbitrary",)),
        name="out_proj",
    )(x, ylru, ydiff, ymla, mlag, wo, g2, wrh, wrl, rb)


def _rank_kernel(eid_ref, start_ref, dest_ref, carry):
    tm = eid_ref.shape[0]

    @pl.when(pl.program_id(0) == 0)
    def _():
        carry[...] = start_ref[...]

    lane = lax.broadcasted_iota(jnp.int32, (tm, LANES), 1)
    r = lax.broadcasted_iota(jnp.int32, (tm, tm), 0)
    c = lax.broadcasted_iota(jnp.int32, (tm, tm), 1)
    lower = (c < r).astype(BF16)
    eid = eid_ref[...]
    dests = []
    for k in range(2):
        onehot = lane == eid[:, k:k + 1]
        before = _dot(lower, onehot.astype(BF16))
        base = carry[...]
        dests.append(jnp.sum(jnp.where(onehot, before + base, 0.0), axis=-1, keepdims=True))
        carry[...] = base + jnp.sum(onehot.astype(F32), axis=0, keepdims=True)
    two = lax.broadcasted_iota(jnp.int32, dest_ref.shape, 1)
    dest_ref[...] = jnp.where(two == 0, dests[0], dests[1]).astype(jnp.int32)


def _rank(eid, starts, tm):
    N = eid.shape[0]
    return pl.pallas_call(
        _rank_kernel,
        grid=(N // tm,),
        in_specs=[pl.BlockSpec((tm, 2), lambda i: (i, 0)), pl.BlockSpec((1, LANES), lambda i: (0, 0))],
        out_specs=pl.BlockSpec((tm, 2), lambda i: (i, 0)),
        out_shape=jax.ShapeDtypeStruct((N, 2), jnp.int32),
        scratch_shapes=[pltpu.VMEM((1, LANES), F32)],
        compiler_params=_cparams(("arbitrary",)),
        name="rank",
    )(eid, starts)


def _dispatch_kernel(dest_ref, x_ref, rows_in_hbm, rows_hbm, sem, *, tm):
    del rows_in_hbm
    base = pl.program_id(0) * tm

    def row_copy(t, a):
        return pltpu.make_async_copy(x_ref.at[pl.ds(t, 1)], rows_hbm.at[pl.ds(dest_ref[a], 1)], sem)

    def issue(t, carry):
        row_copy(t, 2 * (base + t)).start()
        row_copy(t, 2 * (base + t) + 1).start()
        return carry

    lax.fori_loop(0, tm, issue, 0, unroll=8)

    def drain(t, carry):
        row_copy(0, 2 * base).wait()
        row_copy(0, 2 * base).wait()
        return carry

    lax.fori_loop(0, tm, drain, 0, unroll=8)


def _dispatch(dest_flat, x1, rows_zero, tm):
    N = x1.shape[0]
    return pl.pallas_call(
        functools.partial(_dispatch_kernel, tm=tm),
        grid_spec=pltpu.PrefetchScalarGridSpec(
            num_scalar_prefetch=1,
            grid=(N // tm,),
            in_specs=[pl.BlockSpec((tm, D_MODEL), lambda i, d: (i, 0)), pl.BlockSpec(memory_space=pl.ANY)],
            out_specs=pl.BlockSpec(memory_space=pl.ANY),
            scratch_shapes=[pltpu.SemaphoreType.DMA(())],
        ),
        out_shape=jax.ShapeDtypeStruct(rows_zero.shape, rows_zero.dtype),
        input_output_aliases={2: 0},
        compiler_params=_cparams(("arbitrary",)),
        name="dispatch",
    )(dest_flat, x1, rows_zero)


def _expert_kernel(be_ref, nblk_ref, rows_ref, g2_ref, w1_ref, w3_ref, w2_ref, y_ref):
    del be_ref
    used = pl.program_id(0) < nblk_ref[0]

    @pl.when(used)
    def _():
        h = (_rms(rows_ref[...], D_MODEL) * g2_ref[...]).astype(BF16)
        a = _dot(h, w1_ref[0])
        b = _dot(h, w3_ref[0])
        z = (a * jax.nn.sigmoid(a) * b).astype(BF16)
        y_ref[...] = _dot(z, w2_ref[0])

    @pl.when(jnp.logical_not(used))
    def _():
        y_ref[...] = jnp.zeros(y_ref.shape, y_ref.dtype)


def _experts(blk_expert, n_used, rows, g2, w1, w3, w2):
    R = rows.shape[0]
    return pl.pallas_call(
        _expert_kernel,
        grid_spec=pltpu.PrefetchScalarGridSpec(
            num_scalar_prefetch=2,
            grid=(R // ROW_BLOCK,),
            in_specs=[pl.BlockSpec((ROW_BLOCK, D_MODEL), lambda i, be, nb: (i, 0)),
                      pl.BlockSpec(g2.shape, lambda i, be, nb: (0, 0)),
                      pl.BlockSpec((1, D_MODEL, D_EXPERT), lambda i, be, nb: (be[i], 0, 0)),
                      pl.BlockSpec((1, D_MODEL, D_EXPERT), lambda i, be, nb: (be[i], 0, 0)),
                      pl.BlockSpec((1, D_EXPERT, D_MODEL), lambda i, be, nb: (be[i], 0, 0))],
            out_specs=pl.BlockSpec((ROW_BLOCK, D_MODEL), lambda i, be, nb: (i, 0)),
        ),
        out_shape=jax.ShapeDtypeStruct((R, D_MODEL), F32),
        compiler_params=_cparams(("arbitrary",)),
        name="experts",
    )(blk_expert, n_used, rows, g2, w1, w3, w2)


def _combine_kernel(dest_ref, x1_ref, gate_ref, y_hbm, o_ref, buf, sem, *, tm):
    base = pl.program_id(0) * tm

    def row_copy(a, k, t):
        return pltpu.make_async_copy(y_hbm.at[pl.ds(dest_ref[a], 1)], buf.at[k, pl.ds(t, 1)], sem)

    def issue(t, carry):
        row_copy(2 * (base + t), 0, t).start()
        row_copy(2 * (base + t) + 1, 1, t).start()
        return carry

    lax.fori_loop(0, tm, issue, 0, unroll=8)

    def drain(t, carry):
        row_copy(2 * base, 0, 0).wait()
        row_copy(2 * base, 0, 0).wait()
        return carry

    lax.fori_loop(0, tm, drain, 0, unroll=8)
    gate = gate_ref[...]
    o_ref[...] = x1_ref[...] + gate[:, 0:1] * buf[0] + gate[:, 1:2] * buf[1]


def _combine(dest_flat, x1, gate, y_rows, tm):
    N = x1.shape[0]
    return pl.pallas_call(
        functools.partial(_combine_kernel, tm=tm),
        grid_spec=pltpu.PrefetchScalarGridSpec(
            num_scalar_prefetch=1,
            grid=(N // tm,),
            in_specs=[pl.BlockSpec((tm, D_MODEL), lambda i, d: (i, 0)), pl.BlockSpec((tm, 2), lambda i, d: (i, 0)),
                      pl.BlockSpec(memory_space=pl.ANY)],
            out_specs=pl.BlockSpec((tm, D_MODEL), lambda i, d: (i, 0)),
            scratch_shapes=[pltpu.VMEM((2, tm, D_MODEL), F32), pltpu.SemaphoreType.DMA(())],
        ),
        out_shape=jax.ShapeDtypeStruct((N, D_MODEL), F32),
        compiler_params=_cparams(("arbitrary",)),
        name="combine",
    )(dest_flat, x1, gate, y_rows)


def _block_diag(w):
    n, a, b = w.shape
    eye = np.eye(n, dtype=np.float32)
    return (w[:, :, None, :] * eye[:, None, :, None]).reshape(n * a, n * b)


def _pad_heads(v, used):
    lead = v.shape[:-1]
    v = v.reshape(lead + (MLA_HEADS, used))
    v = jnp.pad(v, [(0, 0)] * len(lead) + [(0, 0), (0, MLA_HEAD_PAD - used)])
    return v.reshape(lead + (MLA_QK_PAD,))


def _rotary_lane_tables(seq):
    half = MLA_ROPE_DIM // 2
    inv_freq = ROPE_THETA ** (-jnp.arange(half, dtype=F32) / half)
    ang = jnp.arange(seq, dtype=F32)[:, None] * inv_freq[None, :]
    cos, sin = jnp.cos(ang), jnp.sin(ang)
    ones = jnp.ones((seq, MLA_NOPE_DIM), F32)
    tail = MLA_HEAD_PAD - MLA_QK_DIM
    cos_t = jnp.concatenate([ones, cos, cos, jnp.ones((seq, tail), F32)], axis=1)
    sin_t = jnp.concatenate([0.0 * ones, -sin, sin, jnp.zeros((seq, tail), F32)], axis=1)
    return cos_t, sin_t


def _pick_tile(n, pref):
    t = min(n, pref)
    while n % t:
        t //= 2
    return t


def kernel(x, norm1_g, w_in, lru_conv_w, lru_conv_b, lru_wa, lru_ba, lru_wx, lru_bx, lru_lambda, lru_out_g,
           diff_q_g, diff_k_g, diff_lq1, diff_lk1, diff_lq2, diff_lk2, diff_sub_g,
           mla_cq_g, mla_ckv_g, mla_w_uq, mla_w_ukv, mla_q_g, mla_k_g, mla_out_g, w_out, norm2_g,
           router_g_w, router_g_b, router_e_w, router_e_b, exp_w1, exp_w3, exp_w2):
    B, S, D = x.shape
    N = B * S
    depth = w_in.shape[0]
    assert S % KEY_TILE == 0 and D == D_MODEL
    ts = _pick_tile(S, 512)
    tc = _pick_tile(S, 512)
    tm = _pick_tile(N, 512)
    tr = _pick_tile(N, 256)
    tg = _pick_tile(N, 256)

    cos_t, sin_t = _rotary_lane_tables(S)
    n_dgroups = DIFF_QK_WIDTH // DIFF_HEAD_DIM
    gsum = jnp.asarray(np.kron(np.eye(n_dgroups), np.ones((DIFF_HEAD_DIM, DIFF_HEAD_DIM))), BF16)
    slopes = np.asarray([2.0 ** (-ALIBI_MAX_BIAS * (h + 1) / DIFF_HEADS) * LOG2E for h in range(DIFF_HEADS)], np.float32)
    slopes = jnp.asarray(np.pad(slopes.reshape(DIFF_HEADS // 2, 1, 2), ((0, 0), (0, 0), (0, LANES - 2))))
    place = np.zeros((MLA_ROPE_DIM, MLA_HEADS, MLA_HEAD_PAD), np.float32)
    place[np.arange(MLA_ROPE_DIM), :, MLA_NOPE_DIM + np.arange(MLA_ROPE_DIM)] = 1.0
    place = jnp.asarray(place.reshape(MLA_ROPE_DIM, MLA_QK_PAD))
    row = lambda v: v.reshape(1, -1).astype(F32)

    n_rows = (-(-2 * N // ROW_BLOCK) + N_EXPERTS) * ROW_BLOCK
    n_blk = n_rows // ROW_BLOCK
    rows_zero = jnp.zeros((n_rows, D), F32)

    for l in range(depth):
        wl = w_in[l]
        zc = lambda n: jnp.zeros((D, n), F32)
        o = np.cumsum([0, 256, 256, 384, 384, 384, 192, 128, 32])
        w_cat = jnp.concatenate([wl[:, o[0]:o[5]], wl[:, o[5]:o[6]], zc(C_CKV - C_CQ - MLA_Q_RANK),
                                 wl[:, o[6]:o[7]], wl[:, o[7]:o[8]], zc(D_IN_PAD - C_KR - MLA_ROPE_DIM)],
                                axis=1).astype(BF16)
        dqg = row(jnp.tile(diff_q_g[l], n_dgroups) * (DIFF_HEAD_DIM ** -0.5 * LOG2E))
        dkg = row(jnp.tile(diff_k_g[l], n_dgroups))
        wuq = _pad_heads(mla_w_uq[l], MLA_QK_DIM).astype(BF16)
        mqg = row(_pad_heads(jnp.tile(mla_q_g[l], MLA_HEADS) * (MLA_QK_DIM ** -0.5 * LOG2E), MLA_QK_DIM))
        mkg = row(_pad_heads(jnp.tile(mla_k_g[l], MLA_HEADS), MLA_QK_DIM))
        wukv = mla_w_ukv[l].reshape(MLA_KV_RANK, MLA_HEADS, MLA_NOPE_DIM + MLA_V_DIM)
        wk_nope = _pad_heads(wukv[:, :, :MLA_NOPE_DIM].reshape(MLA_KV_RANK, -1), MLA_NOPE_DIM)
        wk = jnp.concatenate([wk_nope, place], axis=0).astype(BF16)
        wv = wukv[:, :, MLA_NOPE_DIM:].reshape(MLA_KV_RANK, MLA_WIDTH).astype(BF16)

        lru_in, dq, dk, dv, mq, mk, mv = _in_proj(
            x, row(norm1_g[l]), w_cat, gsum, dqg, dkg, row(mla_cq_g[l]), wuq, mqg, row(mla_ckv_g[l]), wk, mkg, wv,
            cos_t, sin_t, ts)

        y_lru = _lru(lru_in, lru_conv_w[l], row(lru_conv_b[l]), _block_diag(lru_wa[l]).astype(BF16),
                     row(lru_ba[l]), _block_diag(lru_wx[l]).astype(BF16), row(lru_bx[l]), row(lru_lambda[l]),
                     row(lru_out_g[l]), tc)

        lambda_init = 0.8 - 0.6 * math.exp(-0.3 * l)
        y_diff = _diff_attn(dq, dk, dv, slopes, row(diff_lq1[l]), row(diff_lk1[l]), row(diff_lq2[l]),
                            row(diff_lk2[l]), row(jnp.tile(diff_sub_g[l], 2)), lambda_init)
        y_mla = _mla_attn(mq, mk, mv)

        wr = jnp.concatenate([router_g_w[l], router_e_w[l],
                              jnp.zeros((D, ROUTER_PAD - N_GROUPS - N_EXPERTS), F32)], axis=1)
        wrh = wr.astype(BF16)
        wrl = (wr - wrh.astype(F32)).astype(BF16)
        rb = jnp.pad(jnp.concatenate([router_g_b[l], router_e_b[l]]), (0, ROUTER_PAD - N_GROUPS - N_EXPERTS))
        x1, eid, gate, cnt = _out_proj(
            x.reshape(N, D), y_lru.reshape(N, LRU_WIDTH), y_diff.reshape(N, DIFF_WIDTH), y_mla.reshape(N, MLA_WIDTH),
            row(mla_out_g[l]), w_out[l].astype(BF16), row(norm2_g[l]), wrh, wrl, row(rb), tm)

        counts = cnt[0, N_GROUPS:N_GROUPS + N_EXPERTS].astype(jnp.int32)
        padded = (counts + ROW_BLOCK - 1) // ROW_BLOCK * ROW_BLOCK
        pad_ends = jnp.cumsum(padded)
        pad_starts = pad_ends - padded
        blk_row0 = jnp.arange(n_blk, dtype=jnp.int32) * ROW_BLOCK
        blk_expert = jnp.minimum(jnp.sum((pad_ends[None, :] <= blk_row0[:, None]).astype(jnp.int32), axis=1),
                                 N_EXPERTS - 1)
        n_used = (pad_ends[-1:] // ROW_BLOCK).astype(jnp.int32)
        starts = jnp.pad(pad_starts.astype(F32), (0, LANES - N_EXPERTS)).reshape(1, LANES)

        dest = _rank(eid, starts, tr).reshape(2 * N)
        rows = _dispatch(dest, x1, rows_zero, tm)
        y_rows = _experts(blk_expert, n_used, rows, row(norm2_g[l]), exp_w1[l].astype(BF16),
                          exp_w3[l].astype(BF16), exp_w2[l].astype(BF16))
        x = _combine(dest, x1, gate, y_rows, tg).reshape(B, S, D)
    return x
```

```python
import functools
import math

import jax
import jax.numpy as jnp
import numpy as np
from jax import lax
from jax.experimental import pallas as pl
from jax.experimental.pallas import tpu as pltpu

F32 = jnp.float32
BF16 = jnp.bfloat16

D_MODEL = 1024
CHUNK = 64
LRU_WIDTH = 256
CONV_WIDTH = 4
RG_C = 8.0
DIFF_HEADS = 6
DIFF_HEAD_DIM = 32
DIFF_V_DIM = 64
DIFF_QK_WIDTH = DIFF_HEADS * 2 * DIFF_HEAD_DIM
DIFF_WIDTH = DIFF_HEADS * DIFF_V_DIM
ALIBI_MAX_BIAS = 8.0
MLA_HEADS = 6
MLA_Q_RANK = 192
MLA_KV_RANK = 128
MLA_NOPE_DIM = 64
MLA_ROPE_DIM = 32
MLA_V_DIM = 64
MLA_QK_DIM = MLA_NOPE_DIM + MLA_ROPE_DIM
MLA_WIDTH = MLA_HEADS * MLA_V_DIM
ROPE_THETA = 10000.0
N_GROUPS = 4
EXPERTS_PER_GROUP = 8
N_EXPERTS = N_GROUPS * EXPERTS_PER_GROUP
D_EXPERT = 256
ROW_BLOCK = 256
EPS = 1e-6

LANES = 128
ROW_TILES = D_MODEL // LANES
MLA_HEAD_PAD = LANES
MLA_QK_PAD = MLA_HEADS * MLA_HEAD_PAD
VMEM_LIMIT = 56 * 1024 * 1024
NEG_BIG = -1e30
KEY_TILE = 256
LOG2E = math.log2(math.e)

C_LRU = 0
C_DQ = 512
C_DK = 896
C_DV = 1280
C_CQ = 1664
C_CKV = 1920
C_KR = 2048
D_IN_PAD = 2176


def _cparams(sem):
    return pltpu.CompilerParams(dimension_semantics=sem, vmem_limit_bytes=VMEM_LIMIT)


def _rms(v, width):
    return v * lax.rsqrt(jnp.sum(v * v, axis=-1, keepdims=True) * (1.0 / width) + EPS)


def _dot(a, b):
    return jnp.dot(a, b, preferred_element_type=F32)


def _split_bf16(v):
    hi = v.astype(BF16)
    lo = (v - hi.astype(F32)).astype(BF16)
    return hi, lo


def _in_proj_kernel(x_ref, g1_ref, w_ref, gsum_ref, dqg_ref, dkg_ref, cqg_ref, wuq_ref, mqg_ref, ckvg_ref,
                    wk_ref, mkg_ref, wv_ref, cos_ref, sin_ref,
                    lru_ref, dqt_ref, dk_ref, dvt_ref, mqt_ref, mk_ref, mvt_ref):
    x = x_ref[0]
    hn = (_rms(x, D_MODEL) * g1_ref[...]).astype(BF16)
    p = _dot(hn, w_ref[...])
    lru_ref[0] = p[:, C_LRU:C_LRU + 2 * LRU_WIDTH]

    gsum = gsum_ref[...]

    def group_norm32(v, gain):
        hi, lo = _split_bf16(v * v)
        ss = _dot(hi, gsum) + _dot(lo, gsum)
        return v * lax.rsqrt(ss * (1.0 / DIFF_HEAD_DIM) + EPS) * gain

    def store_tiles(ref, vt):
        for c in range(vt.shape[1] // KEY_TILE):
            ref[0, c] = vt[:, c * KEY_TILE:(c + 1) * KEY_TILE].astype(BF16)

    dqt_ref[0] = group_norm32(p[:, C_DQ:C_DQ + DIFF_QK_WIDTH], dqg_ref[...]).T.astype(BF16)
    dk_ref[0] = group_norm32(p[:, C_DK:C_DK + DIFF_QK_WIDTH], dkg_ref[...]).astype(BF16)
    store_tiles(dvt_ref, p[:, C_DV:C_DV + DIFF_WIDTH].T)

    cos = cos_ref[...]
    sin = sin_ref[...]
    lane = lax.broadcasted_iota(jnp.int32, cos.shape, 1)
    first_half = lane < MLA_NOPE_DIM + MLA_ROPE_DIM // 2

    def head_norm_rotary(v, gain):
        outs = []
        for h in range(MLA_HEADS):
            c = v[:, h * MLA_HEAD_PAD:(h + 1) * MLA_HEAD_PAD]
            c = _rms(c, MLA_QK_DIM) * gain[:, h * MLA_HEAD_PAD:(h + 1) * MLA_HEAD_PAD]
            swapped = jnp.where(first_half, pltpu.roll(c, MLA_HEAD_PAD - MLA_ROPE_DIM // 2, 1),
                                pltpu.roll(c, MLA_ROPE_DIM // 2, 1))
            outs.append(c * cos + swapped * sin)
        return jnp.concatenate(outs, axis=1)

    cq = (_rms(p[:, C_CQ:C_CQ + MLA_Q_RANK], MLA_Q_RANK) * cqg_ref[...]).astype(BF16)
    mqt_ref[0] = head_norm_rotary(_dot(cq, wuq_ref[...]), mqg_ref[...]).T.astype(BF16)

    ckv = (_rms(p[:, C_CKV:C_CKV + MLA_KV_RANK], MLA_KV_RANK) * ckvg_ref[...]).astype(BF16)
    kcat = jnp.concatenate([ckv, p[:, C_KR:C_KR + MLA_ROPE_DIM].astype(BF16)], axis=1)
    mk_ref[0] = head_norm_rotary(_dot(kcat, wk_ref[...]), mkg_ref[...]).astype(BF16)
    store_tiles(mvt_ref, _dot(ckv, wv_ref[...]).T)


def _in_proj(x, g1, w, gsum, dqg, dkg, cqg, wuq, mqg, ckvg, wk, mkg, wv, cos_t, sin_t, ts):
    B, S, _ = x.shape
    full = lambda a: pl.BlockSpec(a.shape, lambda b, i: (0,) * a.ndim)
    tok = lambda width: pl.BlockSpec((1, ts, width), lambda b, i: (b, i, 0))
    tok_t = lambda width: pl.BlockSpec((1, width, ts), lambda b, i: (b, 0, i))
    nkt = ts // KEY_TILE
    tiles_t = lambda width: pl.BlockSpec((1, nkt, width, KEY_TILE), lambda b, i: (b, i, 0, 0))
    shp = lambda shape, dt: jax.ShapeDtypeStruct(shape, dt)
    return pl.pallas_call(
        _in_proj_kernel,
        grid=(B, S // ts),
        in_specs=[tok(D_MODEL), full(g1), full(w), full(gsum), full(dqg), full(dkg), full(cqg), full(wuq),
                  full(mqg), full(ckvg), full(wk), full(mkg), full(wv),
                  pl.BlockSpec((ts, LANES), lambda b, i: (i, 0)), pl.BlockSpec((ts, LANES), lambda b, i: (i, 0))],
        out_specs=[tok(2 * LRU_WIDTH), tok_t(DIFF_QK_WIDTH), tok(DIFF_QK_WIDTH), tiles_t(DIFF_WIDTH),
                   tok_t(MLA_QK_PAD), tok(MLA_QK_PAD), tiles_t(MLA_WIDTH)],
        out_shape=[shp((B, S, 2 * LRU_WIDTH), F32), shp((B, DIFF_QK_WIDTH, S), BF16), shp((B, S, DIFF_QK_WIDTH), BF16),
                   shp((B, S // KEY_TILE, DIFF_WIDTH, KEY_TILE), BF16), shp((B, MLA_QK_PAD, S), BF16),
                   shp((B, S, MLA_QK_PAD), BF16), shp((B, S // KEY_TILE, MLA_WIDTH, KEY_TILE), BF16)],
        compiler_params=_cparams(("parallel", "parallel")),
        name="in_proj",
    )(x, g1, w, gsum, dqg, dkg, cqg, wuq, mqg, ckvg, wk, mkg, wv, cos_t, sin_t)


LRU_HALO = 8


def _lru_kernel(blk_ref, cw_ref, cb_ref, wa_ref, ba_ref, wx_ref, bx_ref, lam_ref, og_ref, y_ref,
                ubuf, a_scr, b_scr, h_scr):
    B, tc, _ = blk_ref.shape
    W = LRU_WIDTH

    @pl.when(pl.program_id(0) == 0)
    def _():
        ubuf[:, 0:LRU_HALO, :] = jnp.zeros((B, LRU_HALO, W), F32)
        h_scr[...] = jnp.zeros(h_scr.shape, F32)

    ubuf[:, LRU_HALO:, :] = blk_ref[:, :, 0:W]
    xc = cb_ref[...][None]
    for j in range(CONV_WIDTH):
        off = LRU_HALO - (CONV_WIDTH - 1) + j
        xc = xc + cw_ref[j:j + 1, :][None] * ubuf[:, off:off + tc, :]
    ubuf[:, 0:LRU_HALO, :] = ubuf[:, tc:tc + LRU_HALO, :]

    x2 = xc.reshape(B * tc, W)
    xb = x2.astype(BF16)
    r = jax.nn.sigmoid(_dot(xb, wa_ref[...]) + ba_ref[...])
    gi = jax.nn.sigmoid(_dot(xb, wx_ref[...]) + bx_ref[...])
    nl = -lam_ref[...]
    softplus = jnp.maximum(nl, 0.0) + jnp.log(1.0 + jnp.exp(-jnp.abs(nl)))
    a = jnp.exp(-RG_C * r * softplus)
    bb = jnp.sqrt(1.0 - a * a) * gi * x2
    a_scr[...] = a.reshape(B, tc, W)
    b_scr[...] = bb.reshape(B, tc, W)

    def step(t, h):
        h = a_scr[:, pl.ds(t, 1), :] * h + b_scr[:, pl.ds(t, 1), :]
        b_scr[:, pl.ds(t, 1), :] = h
        return h

    h_scr[...] = lax.fori_loop(0, tc, step, h_scr[...], unroll=8)

    gate = blk_ref[:, :, W:2 * W]
    gelu = 0.5 * gate * (1.0 + jnp.tanh(math.sqrt(2.0 / math.pi) * (gate + 0.044715 * gate * gate * gate)))
    y = b_scr[...] * gelu
    y_ref[...] = (_rms(y, W) * og_ref[...][None]).astype(y_ref.dtype)


def _lru(lru_in, cw, cb, wa, ba, wx, bx, lam, og, tc):
    B, S, _ = lru_in.shape
    W = LRU_WIDTH
    full = lambda a: pl.BlockSpec(a.shape, lambda i: (0,) * a.ndim)
    return pl.pallas_call(
        _lru_kernel,
        grid=(S // tc,),
        in_specs=[pl.BlockSpec((B, tc, 2 * W), lambda i: (0, i, 0)), full(cw), full(cb), full(wa), full(ba),
                  full(wx), full(bx), full(lam), full(og)],
        out_specs=pl.BlockSpec((B, tc, W), lambda i: (0, i, 0)),
        out_shape=jax.ShapeDtypeStruct((B, S, W), BF16),
        scratch_shapes=[pltpu.VMEM((B, tc + LRU_HALO, W), F32), pltpu.VMEM((B, tc, W), F32),
                        pltpu.VMEM((B, tc, W), F32), pltpu.VMEM((B, 1, W), F32)],
        compiler_params=_cparams(("arbitrary",)),
        name="lru",
    )(lru_in, cw, cb, wa, ba, wx, bx, lam, og)


def _flash_maps(qt_list, k_ref, vt_ref, k_lanes, bias_of, tile_shift, m_scr, l_scr, acc_scr):
    qi = pl.program_id(2)
    n_maps = len(qt_list)
    tq = qt_list[0].shape[1]
    maps = range(n_maps)

    def biased_scores(j):
        jc = jnp.minimum(j, qi)
        start = pl.multiple_of(jc * KEY_TILE, KEY_TILE)
        is_own = (j == qi).astype(jnp.int32)
        return tuple(_dot(k_ref[0, pl.ds(start, KEY_TILE), k_lanes[i]], qt_list[i]) + bias_of(i, is_own)
                     for i in maps)

    for i in maps:
        m_scr[i] = jnp.full((1, tq), NEG_BIG, F32)
        l_scr[i] = jnp.zeros((1, tq), F32)
        acc_scr[i] = jnp.zeros(acc_scr.shape[1:], F32)

    def body(j, s_cur):
        s_next = biased_scores(j + 1)
        vt = vt_ref[0, j]
        for i in maps:
            s = s_cur[i]
            c = tile_shift(i, qi - j)
            m_old = m_scr[i]
            m_new = jnp.maximum(m_old, jnp.max(s, axis=0, keepdims=True) + c)
            alpha = jnp.exp2(m_old - m_new)
            p = jnp.exp2(s - (m_new - c))
            m_scr[i] = m_new
            l_scr[i] = alpha * l_scr[i] + jnp.sum(p, axis=0, keepdims=True)
            acc_scr[i] = alpha * acc_scr[i] + _dot(vt, p.astype(BF16))
        return s_next

    lax.fori_loop(0, qi + 1, body, biased_scores(0))


def _chunk_allowed_t(tq):
    krow = lax.broadcasted_iota(jnp.int32, (KEY_TILE, tq), 0)
    qcol = lax.broadcasted_iota(jnp.int32, (KEY_TILE, tq), 1)
    return krow, qcol, (krow // CHUNK) <= (qcol // CHUNK)


def _diff_attn_kernel(qt_ref, k_ref, vt_ref, slope_ref, lq1_ref, lk1_ref, lq2_ref, lk2_ref, subg_ref, o_ref,
                      m_scr, l_scr, acc_scr, bias_scr, *, lambda_init, tq):
    qt = qt_ref[0]
    feat = lax.broadcasted_iota(jnp.int32, qt.shape, 0)
    zero = jnp.zeros_like(qt)
    qt_list = [jnp.where(feat // DIFF_HEAD_DIM == i, qt, zero) for i in range(4)]
    slope = [slope_ref[0, 0:1, hh:hh + 1] for hh in range(2)]

    @pl.when(pl.program_id(2) == 0)
    def _():
        krow, qcol, allowed = _chunk_allowed_t(tq)
        kf, qf = krow.astype(F32), qcol.astype(F32)
        self_dist = qf - jnp.abs(qf - kf)
        for hh in range(2):
            bias_scr[hh, 0] = slope[hh] * kf
            bias_scr[hh, 1] = jnp.where(allowed, slope[hh] * self_dist, NEG_BIG)

    def tile_shift(i, n_tiles):
        return -slope[i // 2] * (n_tiles * KEY_TILE).astype(F32)

    _flash_maps(qt_list, k_ref, vt_ref, [slice(None)] * 4, lambda i, is_own: bias_scr[i // 2, is_own],
                tile_shift, m_scr, l_scr, acc_scr)

    lam = (jnp.exp(jnp.sum(lq1_ref[...] * lk1_ref[...], axis=-1, keepdims=True))
           - jnp.exp(jnp.sum(lq2_ref[...] * lk2_ref[...], axis=-1, keepdims=True)) + lambda_init)
    orow = lax.broadcasted_iota(jnp.int32, (LANES, tq), 0)
    out_t = jnp.zeros((LANES, tq), F32)
    for hh in range(2):
        o = acc_scr[2 * hh] / l_scr[2 * hh] - lam * (acc_scr[2 * hh + 1] / l_scr[2 * hh + 1])
        mine = (orow // DIFF_V_DIM) == hh
        ms = jnp.sum(jnp.where(mine, o * o, 0.0), axis=0, keepdims=True) * (1.0 / DIFF_V_DIM)
        out_t = jnp.where(mine, o * lax.rsqrt(ms + EPS), out_t)
    o_ref[0] = (out_t.T * subg_ref[...] * (1.0 - lambda_init)).astype(o_ref.dtype)


def _diff_attn(dqt, dk, dvt, slopes, lq1, lk1, lq2, lk2, subg, lambda_init):
    B, S, _ = dk.shape
    tq = KEY_TILE
    n_pairs = DIFF_HEADS // 2
    full = lambda a: pl.BlockSpec(a.shape, lambda b, p, i: (0,) * a.ndim)
    kern = functools.partial(_diff_attn_kernel, lambda_init=lambda_init, tq=tq)
    return pl.pallas_call(
        kern,
        grid=(B, n_pairs, S // tq),
        in_specs=[pl.BlockSpec((1, LANES, tq), lambda b, p, i: (b, p, i)),
                  pl.BlockSpec((1, S, LANES), lambda b, p, i: (b, 0, p)),
                  pl.BlockSpec((1, S // KEY_TILE, LANES, KEY_TILE), lambda b, p, i: (b, 0, p, 0)),
                  pl.BlockSpec((1, 1, LANES), lambda b, p, i: (p, 0, 0)),
                  full(lq1), full(lk1), full(lq2), full(lk2), full(subg)],
        out_specs=pl.BlockSpec((1, tq, LANES), lambda b, p, i: (b, i, p)),
        out_shape=jax.ShapeDtypeStruct((B, S, DIFF_WIDTH), BF16),
        scratch_shapes=[pltpu.VMEM((4, 1, tq), F32), pltpu.VMEM((4, 1, tq), F32), pltpu.VMEM((4, LANES, tq), F32),
                        pltpu.VMEM((2, 2, KEY_TILE, tq), F32)],
        compiler_params=_cparams(("parallel", "parallel", "arbitrary")),
        name="diff_attn",
    )(dqt, dk, dvt, slopes, lq1, lk1, lq2, lk2, subg)


def _mla_attn_kernel(qt_ref, k_ref, vt_ref, o_ref, m_scr, l_scr, acc_scr, bias_scr, *, tq):
    qt = qt_ref[0]
    qt_list = [qt[hh * MLA_HEAD_PAD:(hh + 1) * MLA_HEAD_PAD, :] for hh in range(2)]
    k_lanes = [slice(hh * MLA_HEAD_PAD, (hh + 1) * MLA_HEAD_PAD) for hh in range(2)]

    @pl.when(pl.program_id(2) == 0)
    def _():
        _, _, allowed = _chunk_allowed_t(tq)
        bias_scr[0] = jnp.zeros((KEY_TILE, tq), F32)
        bias_scr[1] = jnp.where(allowed, 0.0, NEG_BIG)

    _flash_maps(qt_list, k_ref, vt_ref, k_lanes, lambda i, is_own: bias_scr[is_own], lambda i, n: 0.0,
                m_scr, l_scr, acc_scr)
    orow = lax.broadcasted_iota(jnp.int32, (LANES, tq), 0)
    out_t = jnp.where(orow < MLA_V_DIM, acc_scr[0] / l_scr[0], acc_scr[1] / l_scr[1])
    o_ref[0] = out_t.T.astype(o_ref.dtype)


def _mla_attn(mqt, mk, mvt):
    B, S, _ = mk.shape
    tq = KEY_TILE
    n_pairs = MLA_HEADS // 2
    kern = functools.partial(_mla_attn_kernel, tq=tq)
    return pl.pallas_call(
        kern,
        grid=(B, n_pairs, S // tq),
        in_specs=[pl.BlockSpec((1, 2 * MLA_HEAD_PAD, tq), lambda b, p, i: (b, p, i)),
                  pl.BlockSpec((1, S, 2 * MLA_HEAD_PAD), lambda b, p, i: (b, 0, p)),
                  pl.BlockSpec((1, S // KEY_TILE, LANES, KEY_TILE), lambda b, p, i: (b, 0, p, 0))],
        out_specs=pl.BlockSpec((1, tq, LANES), lambda b, p, i: (b, i, p)),
        out_shape=jax.ShapeDtypeStruct((B, S, MLA_WIDTH), F32),
        scratch_shapes=[pltpu.VMEM((2, 1, tq), F32), pltpu.VMEM((2, 1, tq), F32), pltpu.VMEM((2, LANES, tq), F32),
                        pltpu.VMEM((2, KEY_TILE, tq), F32)],
        compiler_params=_cparams(("parallel", "parallel", "arbitrary")),
        name="mla_attn",
    )(mqt, mk, mvt)


ROUTER_PAD = LANES


def _out_proj_kernel(x_ref, ylru_ref, ydiff_ref, ymla_ref, mlag_ref, wo_ref, g2_ref, wrh_ref, wrl_ref, rb_ref,
                     x1_ref, eid_ref, gate_ref, cnt_ref):
    @pl.when(pl.program_id(0) == 0)
    def _():
        cnt_ref[...] = jnp.zeros(cnt_ref.shape, F32)

    ymla = (_rms(ymla_ref[...], MLA_WIDTH) * mlag_ref[...]).astype(BF16)
    x1 = (x_ref[...]
          + _dot(ylru_ref[...], wo_ref[0:LRU_WIDTH, :])
          + _dot(ydiff_ref[...], wo_ref[LRU_WIDTH:LRU_WIDTH + DIFF_WIDTH, :])
          + _dot(ymla, wo_ref[LRU_WIDTH + DIFF_WIDTH:, :]))
    tm = x1.shape[0]
    for c in range(ROW_TILES):
        x1_ref[pl.ds(c, tm, stride=ROW_TILES), :] = x1[:, c * LANES:(c + 1) * LANES]

    h2 = _rms(x1, D_MODEL) * g2_ref[...]
    hi, lo = _split_bf16(h2)
    wrh = wrh_ref[...]
    logits = _dot(hi, wrh) + _dot(hi, wrl_ref[...]) + _dot(lo, wrh) + rb_ref[...]

    lane = lax.broadcasted_iota(jnp.int32, logits.shape, 1)
    big = jnp.int32(1 << 20)
    is_group = lane < N_GROUPS
    gl = jnp.where(is_group, logits, NEG_BIG)
    gmax = jnp.max(gl, axis=-1, keepdims=True)
    g_idx = jnp.min(jnp.where(gl == gmax, lane, big), axis=-1, keepdims=True)
    pg_top = 1.0 / jnp.sum(jnp.where(is_group, jnp.exp(gl - gmax), 0.0), axis=-1, keepdims=True)

    e_lane = lane - N_GROUPS
    in_group = (e_lane >= 0) & (e_lane < N_EXPERTS) & ((e_lane // EXPERTS_PER_GROUP) == g_idx)
    el = jnp.where(in_group, logits, NEG_BIG)
    m1 = jnp.max(el, axis=-1, keepdims=True)
    i1 = jnp.min(jnp.where(el == m1, lane, big), axis=-1, keepdims=True)
    el2 = jnp.where(lane == i1, NEG_BIG, el)
    m2 = jnp.max(el2, axis=-1, keepdims=True)
    i2 = jnp.min(jnp.where(el2 == m2, lane, big), axis=-1, keepdims=True)
    e2 = jnp.exp(m2 - m1)
    g1 = pg_top / (1.0 + e2)
    g2 = pg_top * e2 / (1.0 + e2)

    two = lax.broadcasted_iota(jnp.int32, eid_ref.shape, 1)
    eid_ref[...] = jnp.where(two == 0, i1 - N_GROUPS, i2 - N_GROUPS)
    gate_ref[...] = jnp.where(two == 0, g1, g2)
    onehot = ((e_lane == i1 - N_GROUPS) | (e_lane == i2 - N_GROUPS)).astype(F32)
    cnt_ref[...] += jnp.sum(onehot, axis=0, keepdims=True)


def _out_proj(x, ylru, ydiff, ymla, mlag, wo, g2, wrh, wrl, rb, tm):
    N = x.shape[0]
    full = lambda a: pl.BlockSpec(a.shape, lambda i: (0,) * a.ndim)
    tok = lambda width: pl.BlockSpec((tm, width), lambda i: (i, 0))
    return pl.pallas_call(
        _out_proj_kernel,
        grid=(N // tm,),
        in_specs=[tok(D_MODEL), tok(LRU_WIDTH), tok(DIFF_WIDTH), tok(MLA_WIDTH), full(mlag), full(wo), full(g2),
                  full(wrh), full(wrl), full(rb)],
        out_specs=[pl.BlockSpec((tm * ROW_TILES, LANES), lambda i: (i, 0)), tok(2), tok(2),
                   pl.BlockSpec((1, ROUTER_PAD), lambda i: (0, 0))],
        out_shape=[jax.ShapeDtypeStruct((N * ROW_TILES, LANES), F32), jax.ShapeDtypeStruct((N, 2), jnp.int32),
                   jax.ShapeDtypeStruct((N, 2), F32), jax.ShapeDtypeStruct((1, ROUTER_PAD), F32)],
        compiler_params=_cparams(("arbitrary",)),
        name="out_proj",
    )(x, ylru, ydiff, ymla, mlag, wo, g2, wrh, wrl, rb)


def _rank_kernel(eid_ref, start_ref, dest_ref, carry):
    tm = eid_ref.shape[0]

    @pl.when(pl.program_id(0) == 0)
    def _():
        carry[...] = start_ref[...]

    lane = lax.broadcasted_iota(jnp.int32, (tm, LANES), 1)
    r = lax.broadcasted_iota(jnp.int32, (tm, tm), 0)
    c = lax.broadcasted_iota(jnp.int32, (tm, tm), 1)
    lower = (c < r).astype(BF16)
    eid = eid_ref[...]
    dests = []
    for k in range(2):
        onehot = lane == eid[:, k:k + 1]
        before = _dot(lower, onehot.astype(BF16))
        base = carry[...]
        dests.append(jnp.sum(jnp.where(onehot, before + base, 0.0), axis=-1, keepdims=True))
        carry[...] = base + jnp.sum(onehot.astype(F32), axis=0, keepdims=True)
    two = lax.broadcasted_iota(jnp.int32, dest_ref.shape, 1)
    dest_ref[...] = jnp.where(two == 0, dests[0], dests[1]).astype(jnp.int32)


def _rank(eid, starts, tm):
    N = eid.shape[0]
    return pl.pallas_call(
        _rank_kernel,
        grid=(N // tm,),
        in_specs=[pl.BlockSpec((tm, 2), lambda i: (i, 0)), pl.BlockSpec((1, LANES), lambda i: (0, 0))],
        out_specs=pl.BlockSpec((tm, 2), lambda i: (i, 0)),
        out_shape=jax.ShapeDtypeStruct((N, 2), jnp.int32),
        scratch_shapes=[pltpu.VMEM((1, LANES), F32)],
        compiler_params=_cparams(("arbitrary",)),
        name="rank",
    )(eid, starts)


def _row_tokens_kernel(dest_ref, tok_ref):
    def clear(r, carry):
        tok_ref[r] = 0
        return carry

    lax.fori_loop(0, tok_ref.shape[0], clear, 0, unroll=8)

    def place(a, carry):
        tok_ref[dest_ref[a]] = lax.shift_right_logical(a, 1)
        return carry

    lax.fori_loop(0, dest_ref.shape[0], place, 0, unroll=8)


def _row_tokens(dest_flat, n_rows):
    return pl.pallas_call(
        _row_tokens_kernel,
        grid_spec=pltpu.PrefetchScalarGridSpec(
            num_scalar_prefetch=1,
            grid=(1,),
            in_specs=[],
            out_specs=pl.BlockSpec(memory_space=pltpu.SMEM),
        ),
        out_shape=jax.ShapeDtypeStruct((n_rows,), jnp.int32),
        compiler_params=_cparams(("arbitrary",)),
        name="row_tokens",
    )(dest_flat)


def _expert_kernel(be_ref, nused_ref, tok_ref, x_hbm, g2_ref, w1_ref, w3_ref, w2_ref, y_ref, xbuf, sems):
    del be_ref
    i = pl.program_id(0)
    n_used = nused_ref[0]
    slot = lax.rem(i, 2)

    def row_copy(blk, r, s):
        src = pl.multiple_of(tok_ref[blk * ROW_BLOCK + r] * ROW_TILES, ROW_TILES)
        dst = pl.multiple_of((s * ROW_BLOCK + r) * ROW_TILES, ROW_TILES)
        return pltpu.make_async_copy(x_hbm.at[pl.ds(src, ROW_TILES)], xbuf.at[pl.ds(dst, ROW_TILES)], sems.at[s])

    def issue(blk, s):
        for r in range(ROW_BLOCK):
            row_copy(blk, r, s).start()

    def drain(s):
        def one(r, carry):
            row_copy(0, 0, s).wait()
            return carry

        lax.fori_loop(0, ROW_BLOCK, one, 0, unroll=8)

    @pl.when(i == 0)
    def _():
        issue(0, 0)

    @pl.when(i < n_used)
    def _():
        issue(i + 1, 1 - slot)
        drain(slot)
        x = jnp.concatenate([xbuf[pl.ds(slot * (ROW_BLOCK * ROW_TILES) + c, ROW_BLOCK, stride=ROW_TILES), :]
                             for c in range(ROW_TILES)], axis=1)
        h = (_rms(x, D_MODEL) * g2_ref[...]).astype(BF16)
        a = _dot(h, w1_ref[0].astype(BF16))
        b = _dot(h, w3_ref[0].astype(BF16))
        z = (a * jax.nn.sigmoid(a) * b).astype(BF16)
        y = _dot(z, w2_ref[0].astype(BF16))
        for c in range(ROW_TILES):
            y_ref[pl.ds(c, ROW_BLOCK, stride=ROW_TILES), :] = y[:, c * LANES:(c + 1) * LANES]

    @pl.when(i >= n_used)
    def _():
        @pl.when(i == n_used)
        def _():
            drain(slot)

        y_ref[...] = jnp.zeros(y_ref.shape, y_ref.dtype)


def _experts(blk_expert, n_used, row_tok, x1, g2, w1, w3, w2):
    n_blk = row_tok.shape[0] // ROW_BLOCK
    wspec = lambda shape: pl.BlockSpec((1,) + shape, lambda i, be, nu, rt: (be[i], 0, 0))
    return pl.pallas_call(
        _expert_kernel,
        grid_spec=pltpu.PrefetchScalarGridSpec(
            num_scalar_prefetch=3,
            grid=(n_blk,),
            in_specs=[pl.BlockSpec(memory_space=pl.ANY),
                      pl.BlockSpec(g2.shape, lambda i, be, nu, rt: (0, 0)),
                      wspec((D_MODEL, D_EXPERT)), wspec((D_MODEL, D_EXPERT)), wspec((D_EXPERT, D_MODEL))],
            out_specs=pl.BlockSpec((ROW_BLOCK * ROW_TILES, LANES), lambda i, be, nu, rt: (i, 0)),
            scratch_shapes=[pltpu.VMEM((2 * ROW_BLOCK * ROW_TILES, LANES), F32), pltpu.SemaphoreType.DMA((2,))],
        ),
        out_shape=jax.ShapeDtypeStruct((n_blk * ROW_BLOCK * ROW_TILES, LANES), F32),
        compiler_params=_cparams(("arbitrary",)),
        name="experts",
    )(blk_expert, n_used, row_tok, x1, g2, w1, w3, w2)


def _combine_kernel(dest_ref, x1_ref, gate_ref, y_hbm, o_ref, buf, sem, *, tm):
    base = pl.program_id(0) * tm

    def row_copy(a, k, t):
        src = pl.multiple_of(dest_ref[a] * ROW_TILES, ROW_TILES)
        dst = pl.multiple_of((k * tm + t) * ROW_TILES, ROW_TILES)
        return pltpu.make_async_copy(y_hbm.at[pl.ds(src, ROW_TILES)], buf.at[pl.ds(dst, ROW_TILES)], sem)

    def issue(t, carry):
        row_copy(2 * (base + t), 0, t).start()
        row_copy(2 * (base + t) + 1, 1, t).start()
        return carry

    lax.fori_loop(0, tm, issue, 0, unroll=8)

    def drain(t, carry):
        row_copy(2 * base, 0, 0).wait()
        row_copy(2 * base, 0, 0).wait()
        return carry

    lax.fori_loop(0, tm, drain, 0, unroll=8)
    gate = gate_ref[...]
    g0, g1 = gate[:, 0:1], gate[:, 1:2]
    for c in range(ROW_TILES):
        lanes = slice(c * LANES, (c + 1) * LANES)
        y0 = buf[pl.ds(c, tm, stride=ROW_TILES), :]
        y1 = buf[pl.ds(tm * ROW_TILES + c, tm, stride=ROW_TILES), :]
        o_ref[:, lanes] = x1_ref[pl.ds(c, tm, stride=ROW_TILES), :] + g0 * y0 + g1 * y1


def _combine(dest_flat, x1, gate, y_rows, tm):
    N = gate.shape[0]
    return pl.pallas_call(
        functools.partial(_combine_kernel, tm=tm),
        grid_spec=pltpu.PrefetchScalarGridSpec(
            num_scalar_prefetch=1,
            grid=(N // tm,),
            in_specs=[pl.BlockSpec((tm * ROW_TILES, LANES), lambda i, d: (i, 0)),
                      pl.BlockSpec((tm, 2), lambda i, d: (i, 0)),
                      pl.BlockSpec(memory_space=pl.ANY)],
            out_specs=pl.BlockSpec((tm, D_MODEL), lambda i, d: (i, 0)),
            scratch_shapes=[pltpu.VMEM((2 * tm * ROW_TILES, LANES), F32), pltpu.SemaphoreType.DMA(())],
        ),
        out_shape=jax.ShapeDtypeStruct((N, D_MODEL), F32),
        compiler_params=_cparams(("arbitrary",)),
        name="combine",
    )(dest_flat, x1, gate, y_rows)


def _block_diag(w):
    n, a, b = w.shape
    eye = np.eye(n, dtype=np.float32)
    return (w[:, :, None, :] * eye[:, None, :, None]).reshape(n * a, n * b)


def _pad_heads(v, used):
    lead = v.shape[:-1]
    v = v.reshape(lead + (MLA_HEADS, used))
    v = jnp.pad(v, [(0, 0)] * len(lead) + [(0, 0), (0, MLA_HEAD_PAD - used)])
    return v.reshape(lead + (MLA_QK_PAD,))


def _rotary_lane_tables(seq):
    half = MLA_ROPE_DIM // 2
    inv_freq = ROPE_THETA ** (-jnp.arange(half, dtype=F32) / half)
    ang = jnp.arange(seq, dtype=F32)[:, None] * inv_freq[None, :]
    cos, sin = jnp.cos(ang), jnp.sin(ang)
    ones = jnp.ones((seq, MLA_NOPE_DIM), F32)
    tail = MLA_HEAD_PAD - MLA_QK_DIM
    cos_t = jnp.concatenate([ones, cos, cos, jnp.ones((seq, tail), F32)], axis=1)
    sin_t = jnp.concatenate([0.0 * ones, -sin, sin, jnp.zeros((seq, tail), F32)], axis=1)
    return cos_t, sin_t


def _pick_tile(n, pref):
    t = min(n, pref)
    while n % t:
        t //= 2
    return t


def kernel(x, norm1_g, w_in, lru_conv_w, lru_conv_b, lru_wa, lru_ba, lru_wx, lru_bx, lru_lambda, lru_out_g,
           diff_q_g, diff_k_g, diff_lq1, diff_lk1, diff_lq2, diff_lk2, diff_sub_g,
           mla_cq_g, mla_ckv_g, mla_w_uq, mla_w_ukv, mla_q_g, mla_k_g, mla_out_g, w_out, norm2_g,
           router_g_w, router_g_b, router_e_w, router_e_b, exp_w1, exp_w3, exp_w2):
    B, S, D = x.shape
    N = B * S
    depth = w_in.shape[0]
    assert S % KEY_TILE == 0 and D == D_MODEL
    ts = _pick_tile(S, 512)
    tc = _pick_tile(S, 512)
    tm = _pick_tile(N, 512)
    tr = _pick_tile(N, 256)
    tg = _pick_tile(N, 256)

    cos_t, sin_t = _rotary_lane_tables(S)
    n_dgroups = DIFF_QK_WIDTH // DIFF_HEAD_DIM
    gsum = jnp.asarray(np.kron(np.eye(n_dgroups), np.ones((DIFF_HEAD_DIM, DIFF_HEAD_DIM))), BF16)
    slopes = np.asarray([2.0 ** (-ALIBI_MAX_BIAS * (h + 1) / DIFF_HEADS) * LOG2E for h in range(DIFF_HEADS)], np.float32)
    slopes = jnp.asarray(np.pad(slopes.reshape(DIFF_HEADS // 2, 1, 2), ((0, 0), (0, 0), (0, LANES - 2))))
    place = np.zeros((MLA_ROPE_DIM, MLA_HEADS, MLA_HEAD_PAD), np.float32)
    place[np.arange(MLA_ROPE_DIM), :, MLA_NOPE_DIM + np.arange(MLA_ROPE_DIM)] = 1.0
    place = jnp.asarray(place.reshape(MLA_ROPE_DIM, MLA_QK_PAD))
    row = lambda v: v.reshape(1, -1).astype(F32)

    n_rows = (-(-2 * N // ROW_BLOCK) + N_EXPERTS) * ROW_BLOCK
    n_blk = n_rows // ROW_BLOCK

    for l in range(depth):
        wl = w_in[l]
        zc = lambda n: jnp.zeros((D, n), F32)
        o = np.cumsum([0, 256, 256, 384, 384, 384, 192, 128, 32])
        w_cat = jnp.concatenate([wl[:, o[0]:o[5]], wl[:, o[5]:o[6]], zc(C_CKV - C_CQ - MLA_Q_RANK),
                                 wl[:, o[6]:o[7]], wl[:, o[7]:o[8]], zc(D_IN_PAD - C_KR - MLA_ROPE_DIM)],
                                axis=1).astype(BF16)
        dqg = row(jnp.tile(diff_q_g[l], n_dgroups) * (DIFF_HEAD_DIM ** -0.5 * LOG2E))
        dkg = row(jnp.tile(diff_k_g[l], n_dgroups))
        wuq = _pad_heads(mla_w_uq[l], MLA_QK_DIM).astype(BF16)
        mqg = row(_pad_heads(jnp.tile(mla_q_g[l], MLA_HEADS) * (MLA_QK_DIM ** -0.5 * LOG2E), MLA_QK_DIM))
        mkg = row(_pad_heads(jnp.tile(mla_k_g[l], MLA_HEADS), MLA_QK_DIM))
        wukv = mla_w_ukv[l].reshape(MLA_KV_RANK, MLA_HEADS, MLA_NOPE_DIM + MLA_V_DIM)
        wk_nope = _pad_heads(wukv[:, :, :MLA_NOPE_DIM].reshape(MLA_KV_RANK, -1), MLA_NOPE_DIM)
        wk = jnp.concatenate([wk_nope, place], axis=0).astype(BF16)
        wv = wukv[:, :, MLA_NOPE_DIM:].reshape(MLA_KV_RANK, MLA_WIDTH).astype(BF16)

        lru_in, dq, dk, dv, mq, mk, mv = _in_proj(
            x, row(norm1_g[l]), w_cat, gsum, dqg, dkg, row(mla_cq_g[l]), wuq, mqg, row(mla_ckv_g[l]), wk, mkg, wv,
            cos_t, sin_t, ts)

        y_lru = _lru(lru_in, lru_conv_w[l], row(lru_conv_b[l]), _block_diag(lru_wa[l]).astype(BF16),
                     row(lru_ba[l]), _block_diag(lru_wx[l]).astype(BF16), row(lru_bx[l]), row(lru_lambda[l]),
                     row(lru_out_g[l]), tc)

        lambda_init = 0.8 - 0.6 * math.exp(-0.3 * l)
        y_diff = _diff_attn(dq, dk, dv, slopes, row(diff_lq1[l]), row(diff_lk1[l]), row(diff_lq2[l]),
                            row(diff_lk2[l]), row(jnp.tile(diff_sub_g[l], 2)), lambda_init)
        y_mla = _mla_attn(mq, mk, mv)

        wr = jnp.concatenate([router_g_w[l], router_e_w[l],
                              jnp.zeros((D, ROUTER_PAD - N_GROUPS - N_EXPERTS), F32)], axis=1)
        wrh = wr.astype(BF16)
        wrl = (wr - wrh.astype(F32)).astype(BF16)
        rb = jnp.pad(jnp.concatenate([router_g_b[l], router_e_b[l]]), (0, ROUTER_PAD - N_GROUPS - N_EXPERTS))
        x1, eid, gate, cnt = _out_proj(
            x.reshape(N, D), y_lru.reshape(N, LRU_WIDTH), y_diff.reshape(N, DIFF_WIDTH), y_mla.reshape(N, MLA_WIDTH),
            row(mla_out_g[l]), w_out[l].astype(BF16), row(norm2_g[l]), wrh, wrl, row(rb), tm)

        counts = cnt[0, N_GROUPS:N_GROUPS + N_EXPERTS].astype(jnp.int32)
        padded = (counts + ROW_BLOCK - 1) // ROW_BLOCK * ROW_BLOCK
        pad_ends = jnp.cumsum(padded)
        pad_starts = pad_ends - padded
        blk_row0 = jnp.arange(n_blk, dtype=jnp.int32) * ROW_BLOCK
        blk_expert = jnp.minimum(jnp.sum((pad_ends[None, :] <= blk_row0[:, None]).astype(jnp.int32), axis=1),
                                 N_EXPERTS - 1)
        n_used = (pad_ends[-1:] // ROW_BLOCK).astype(jnp.int32)
        starts = jnp.pad(pad_starts.astype(F32), (0, LANES - N_EXPERTS)).reshape(1, LANES)

        dest = _rank(eid, starts, tr).reshape(2 * N)
        row_tok = _row_tokens(dest, n_rows)
        y_rows = _experts(blk_expert, n_used, row_tok, x1, row(norm2_g[l]), exp_w1[l], exp_w3[l], exp_w2[l])
        x = _combine(dest, x1, gate, y_rows, tg).reshape(B, S, D)
    return x
```

```python
import functools
import math

import jax
import jax.numpy as jnp
import numpy as np
from jax import lax
from jax.experimental import pallas as pl
from jax.experimental.pallas import tpu as pltpu

F32 = jnp.float32
BF16 = jnp.bfloat16

D_MODEL = 1024
CHUNK = 64
LRU_WIDTH = 256
CONV_WIDTH = 4
RG_C = 8.0
DIFF_HEADS = 6
DIFF_HEAD_DIM = 32
DIFF_V_DIM = 64
DIFF_QK_WIDTH = DIFF_HEADS * 2 * DIFF_HEAD_DIM
DIFF_WIDTH = DIFF_HEADS * DIFF_V_DIM
ALIBI_MAX_BIAS = 8.0
MLA_HEADS = 6
MLA_Q_RANK = 192
MLA_KV_RANK = 128
MLA_NOPE_DIM = 64
MLA_ROPE_DIM = 32
MLA_V_DIM = 64
MLA_QK_DIM = MLA_NOPE_DIM + MLA_ROPE_DIM
MLA_WIDTH = MLA_HEADS * MLA_V_DIM
ROPE_THETA = 10000.0
N_GROUPS = 4
EXPERTS_PER_GROUP = 8
N_EXPERTS = N_GROUPS * EXPERTS_PER_GROUP
D_EXPERT = 256
ROW_BLOCK = 256
EPS = 1e-6

LANES = 128
ROW_TILES = D_MODEL // LANES
MLA_HEAD_PAD = LANES
MLA_QK_PAD = MLA_HEADS * MLA_HEAD_PAD
VMEM_LIMIT = 56 * 1024 * 1024
NEG_BIG = -1e30
KEY_TILE = 256
LOG2E = math.log2(math.e)

C_LRU = 0
C_DQ = 512
C_DK = 896
C_DV = 1280
C_CQ = 1664
C_CKV = 1920
C_KR = 2048
D_IN_PAD = 2176


def _cparams(sem):
    return pltpu.CompilerParams(dimension_semantics=sem, vmem_limit_bytes=VMEM_LIMIT)


def _rms(v, width):
    return v * lax.rsqrt(jnp.sum(v * v, axis=-1, keepdims=True) * (1.0 / width) + EPS)


def _dot(a, b):
    return jnp.dot(a, b, preferred_element_type=F32)


def _split_bf16(v):
    hi = v.astype(BF16)
    lo = (v - hi.astype(F32)).astype(BF16)
    return hi, lo


def _in_proj_kernel(x_ref, g1_ref, w_ref, gsum_ref, dqg_ref, dkg_ref, cqg_ref, wuq_ref, mqg_ref, ckvg_ref,
                    wk_ref, mkg_ref, wv_ref, cos_ref, sin_ref,
                    lru_ref, dqt_ref, dk_ref, dvt_ref, mqt_ref, mk_ref, mvt_ref):
    x = x_ref[0]
    hn = (_rms(x, D_MODEL) * g1_ref[...]).astype(BF16)
    p = _dot(hn, w_ref[...])
    lru_ref[0] = p[:, C_LRU:C_LRU + 2 * LRU_WIDTH]

    gsum = gsum_ref[...]

    def group_norm32(v, gain):
        hi, lo = _split_bf16(v * v)
        ss = _dot(hi, gsum) + _dot(lo, gsum)
        return v * lax.rsqrt(ss * (1.0 / DIFF_HEAD_DIM) + EPS) * gain

    def store_tiles(ref, vt):
        for c in range(vt.shape[1] // KEY_TILE):
            ref[0, c] = vt[:, c * KEY_TILE:(c + 1) * KEY_TILE].astype(BF16)

    dqt_ref[0] = group_norm32(p[:, C_DQ:C_DQ + DIFF_QK_WIDTH], dqg_ref[...]).T.astype(BF16)
    dk_ref[0] = group_norm32(p[:, C_DK:C_DK + DIFF_QK_WIDTH], dkg_ref[...]).astype(BF16)
    store_tiles(dvt_ref, p[:, C_DV:C_DV + DIFF_WIDTH].T)

    cos = cos_ref[...]
    sin = sin_ref[...]
    lane = lax.broadcasted_iota(jnp.int32, cos.shape, 1)
    first_half = lane < MLA_NOPE_DIM + MLA_ROPE_DIM // 2

    def head_norm_rotary(v, gain):
        outs = []
        for h in range(MLA_HEADS):
            c = v[:, h * MLA_HEAD_PAD:(h + 1) * MLA_HEAD_PAD]
            c = _rms(c, MLA_QK_DIM) * gain[:, h * MLA_HEAD_PAD:(h + 1) * MLA_HEAD_PAD]
            swapped = jnp.where(first_half, pltpu.roll(c, MLA_HEAD_PAD - MLA_ROPE_DIM // 2, 1),
                                pltpu.roll(c, MLA_ROPE_DIM // 2, 1))
            outs.append(c * cos + swapped * sin)
        return jnp.concatenate(outs, axis=1)

    cq = (_rms(p[:, C_CQ:C_CQ + MLA_Q_RANK], MLA_Q_RANK) * cqg_ref[...]).astype(BF16)
    mqt_ref[0] = head_norm_rotary(_dot(cq, wuq_ref[...]), mqg_ref[...]).T.astype(BF16)

    ckv = (_rms(p[:, C_CKV:C_CKV + MLA_KV_RANK], MLA_KV_RANK) * ckvg_ref[...]).astype(BF16)
    kcat = jnp.concatenate([ckv, p[:, C_KR:C_KR + MLA_ROPE_DIM].astype(BF16)], axis=1)
    mk_ref[0] = head_norm_rotary(_dot(kcat, wk_ref[...]), mkg_ref[...]).astype(BF16)
    store_tiles(mvt_ref, _dot(ckv, wv_ref[...]).T)


def _in_proj(x, g1, w, gsum, dqg, dkg, cqg, wuq, mqg, ckvg, wk, mkg, wv, cos_t, sin_t, ts):
    B, S, _ = x.shape
    full = lambda a: pl.BlockSpec(a.shape, lambda b, i: (0,) * a.ndim)
    tok = lambda width: pl.BlockSpec((1, ts, width), lambda b, i: (b, i, 0))
    tok_t = lambda width: pl.BlockSpec((1, width, ts), lambda b, i: (b, 0, i))
    nkt = ts // KEY_TILE
    tiles_t = lambda width: pl.BlockSpec((1, nkt, width, KEY_TILE), lambda b, i: (b, i, 0, 0))
    shp = lambda shape, dt: jax.ShapeDtypeStruct(shape, dt)
    return pl.pallas_call(
        _in_proj_kernel,
        grid=(B, S // ts),
        in_specs=[tok(D_MODEL), full(g1), full(w), full(gsum), full(dqg), full(dkg), full(cqg), full(wuq),
                  full(mqg), full(ckvg), full(wk), full(mkg), full(wv),
                  pl.BlockSpec((ts, LANES), lambda b, i: (i, 0)), pl.BlockSpec((ts, LANES), lambda b, i: (i, 0))],
        out_specs=[tok(2 * LRU_WIDTH), tok_t(DIFF_QK_WIDTH), tok(DIFF_QK_WIDTH), tiles_t(DIFF_WIDTH),
                   tok_t(MLA_QK_PAD), tok(MLA_QK_PAD), tiles_t(MLA_WIDTH)],
        out_shape=[shp((B, S, 2 * LRU_WIDTH), F32), shp((B, DIFF_QK_WIDTH, S), BF16), shp((B, S, DIFF_QK_WIDTH), BF16),
                   shp((B, S // KEY_TILE, DIFF_WIDTH, KEY_TILE), BF16), shp((B, MLA_QK_PAD, S), BF16),
                   shp((B, S, MLA_QK_PAD), BF16), shp((B, S // KEY_TILE, MLA_WIDTH, KEY_TILE), BF16)],
        compiler_params=_cparams(("parallel", "parallel")),
        name="in_proj",
    )(x, g1, w, gsum, dqg, dkg, cqg, wuq, mqg, ckvg, wk, mkg, wv, cos_t, sin_t)


LRU_HALO = 8


def _lru_kernel(blk_ref, cw_ref, cb_ref, wa_ref, ba_ref, wx_ref, bx_ref, lam_ref, og_ref, y_ref,
                ubuf, a_scr, b_scr, h_scr):
    B, tc, _ = blk_ref.shape
    W = LRU_WIDTH

    @pl.when(pl.program_id(0) == 0)
    def _():
        ubuf[:, 0:LRU_HALO, :] = jnp.zeros((B, LRU_HALO, W), F32)
        h_scr[...] = jnp.zeros(h_scr.shape, F32)

    ubuf[:, LRU_HALO:, :] = blk_ref[:, :, 0:W]
    xc = cb_ref[...][None]
    for j in range(CONV_WIDTH):
        off = LRU_HALO - (CONV_WIDTH - 1) + j
        xc = xc + cw_ref[j:j + 1, :][None] * ubuf[:, off:off + tc, :]
    ubuf[:, 0:LRU_HALO, :] = ubuf[:, tc:tc + LRU_HALO, :]

    x2 = xc.reshape(B * tc, W)
    xb = x2.astype(BF16)
    r = jax.nn.sigmoid(_dot(xb, wa_ref[...]) + ba_ref[...])
    gi = jax.nn.sigmoid(_dot(xb, wx_ref[...]) + bx_ref[...])
    nl = -lam_ref[...]
    softplus = jnp.maximum(nl, 0.0) + jnp.log(1.0 + jnp.exp(-jnp.abs(nl)))
    a = jnp.exp(-RG_C * r * softplus)
    bb = jnp.sqrt(1.0 - a * a) * gi * x2
    a_scr[...] = a.reshape(B, tc, W)
    b_scr[...] = bb.reshape(B, tc, W)

    def step(t, h):
        h = a_scr[:, pl.ds(t, 1), :] * h + b_scr[:, pl.ds(t, 1), :]
        b_scr[:, pl.ds(t, 1), :] = h
        return h

    h_scr[...] = lax.fori_loop(0, tc, step, h_scr[...], unroll=8)

    gate = blk_ref[:, :, W:2 * W]
    gelu = 0.5 * gate * (1.0 + jnp.tanh(math.sqrt(2.0 / math.pi) * (gate + 0.044715 * gate * gate * gate)))
    y = b_scr[...] * gelu
    y_ref[...] = (_rms(y, W) * og_ref[...][None]).astype(y_ref.dtype)


def _lru(lru_in, cw, cb, wa, ba, wx, bx, lam, og, tc):
    B, S, _ = lru_in.shape
    W = LRU_WIDTH
    full = lambda a: pl.BlockSpec(a.shape, lambda i: (0,) * a.ndim)
    return pl.pallas_call(
        _lru_kernel,
        grid=(S // tc,),
        in_specs=[pl.BlockSpec((B, tc, 2 * W), lambda i: (0, i, 0)), full(cw), full(cb), full(wa), full(ba),
                  full(wx), full(bx), full(lam), full(og)],
        out_specs=pl.BlockSpec((B, tc, W), lambda i: (0, i, 0)),
        out_shape=jax.ShapeDtypeStruct((B, S, W), BF16),
        scratch_shapes=[pltpu.VMEM((B, tc + LRU_HALO, W), F32), pltpu.VMEM((B, tc, W), F32),
                        pltpu.VMEM((B, tc, W), F32), pltpu.VMEM((B, 1, W), F32)],
        compiler_params=_cparams(("arbitrary",)),
        name="lru",
    )(lru_in, cw, cb, wa, ba, wx, bx, lam, og)


def _flash_maps(qt_list, k_ref, vt_ref, k_lanes, bias_of, tile_shift, m_scr, l_scr, acc_scr):
    qi = pl.program_id(2)
    n_maps = len(qt_list)
    tq = qt_list[0].shape[1]
    maps = range(n_maps)

    def biased_scores(j):
        jc = jnp.minimum(j, qi)
        start = pl.multiple_of(jc * KEY_TILE, KEY_TILE)
        is_own = (j == qi).astype(jnp.int32)
        return tuple(_dot(k_ref[0, pl.ds(start, KEY_TILE), k_lanes[i]], qt_list[i]) + bias_of(i, is_own)
                     for i in maps)

    for i in maps:
        m_scr[i] = jnp.full((1, tq), NEG_BIG, F32)
        l_scr[i] = jnp.zeros((1, tq), F32)
        acc_scr[i] = jnp.zeros(acc_scr.shape[1:], F32)

    def body(j, s_cur):
        s_next = biased_scores(j + 1)
        vt = vt_ref[0, j]
        for i in maps:
            s = s_cur[i]
            c = tile_shift(i, qi - j)
            m_old = m_scr[i]
            m_new = jnp.maximum(m_old, jnp.max(s, axis=0, keepdims=True) + c)
            alpha = jnp.exp2(m_old - m_new)
            p = jnp.exp2(s - (m_new - c))
            m_scr[i] = m_new
            l_scr[i] = alpha * l_scr[i] + jnp.sum(p, axis=0, keepdims=True)
            acc_scr[i] = alpha * acc_scr[i] + _dot(vt, p.astype(BF16))
        return s_next

    lax.fori_loop(0, qi + 1, body, biased_scores(0))


def _chunk_allowed_t(tq):
    krow = lax.broadcasted_iota(jnp.int32, (KEY_TILE, tq), 0)
    qcol = lax.broadcasted_iota(jnp.int32, (KEY_TILE, tq), 1)
    return krow, qcol, (krow // CHUNK) <= (qcol // CHUNK)


def _diff_attn_kernel(qt_ref, k_ref, vt_ref, slope_ref, lq1_ref, lk1_ref, lq2_ref, lk2_ref, subg_ref, o_ref,
                      m_scr, l_scr, acc_scr, bias_scr, *, lambda_init, tq):
    qt = qt_ref[0]
    feat = lax.broadcasted_iota(jnp.int32, qt.shape, 0)
    zero = jnp.zeros_like(qt)
    qt_list = [jnp.where(feat // DIFF_HEAD_DIM == i, qt, zero) for i in range(4)]
    slope = [slope_ref[0, 0:1, hh:hh + 1] for hh in range(2)]

    @pl.when(pl.program_id(2) == 0)
    def _():
        krow, qcol, allowed = _chunk_allowed_t(tq)
        kf, qf = krow.astype(F32), qcol.astype(F32)
        self_dist = qf - jnp.abs(qf - kf)
        for hh in range(2):
            bias_scr[hh, 0] = slope[hh] * kf
            bias_scr[hh, 1] = jnp.where(allowed, slope[hh] * self_dist, NEG_BIG)

    def tile_shift(i, n_tiles):
        return -slope[i // 2] * (n_tiles * KEY_TILE).astype(F32)

    _flash_maps(qt_list, k_ref, vt_ref, [slice(None)] * 4, lambda i, is_own: bias_scr[i // 2, is_own],
                tile_shift, m_scr, l_scr, acc_scr)

    lam = (jnp.exp(jnp.sum(lq1_ref[...] * lk1_ref[...], axis=-1, keepdims=True))
           - jnp.exp(jnp.sum(lq2_ref[...] * lk2_ref[...], axis=-1, keepdims=True)) + lambda_init)
    orow = lax.broadcasted_iota(jnp.int32, (LANES, tq), 0)
    out_t = jnp.zeros((LANES, tq), F32)
    for hh in range(2):
        o = acc_scr[2 * hh] / l_scr[2 * hh] - lam * (acc_scr[2 * hh + 1] / l_scr[2 * hh + 1])
        mine = (orow // DIFF_V_DIM) == hh
        ms = jnp.sum(jnp.where(mine, o * o, 0.0), axis=0, keepdims=True) * (1.0 / DIFF_V_DIM)
        out_t = jnp.where(mine, o * lax.rsqrt(ms + EPS), out_t)
    o_ref[0] = (out_t.T * subg_ref[...] * (1.0 - lambda_init)).astype(o_ref.dtype)


def _diff_attn(dqt, dk, dvt, slopes, lq1, lk1, lq2, lk2, subg, lambda_init):
    B, S, _ = dk.shape
    tq = KEY_TILE
    n_pairs = DIFF_HEADS // 2
    full = lambda a: pl.BlockSpec(a.shape, lambda b, p, i: (0,) * a.ndim)
    kern = functools.partial(_diff_attn_kernel, lambda_init=lambda_init, tq=tq)
    return pl.pallas_call(
        kern,
        grid=(B, n_pairs, S // tq),
        in_specs=[pl.BlockSpec((1, LANES, tq), lambda b, p, i: (b, p, i)),
                  pl.BlockSpec((1, S, LANES), lambda b, p, i: (b, 0, p)),
                  pl.BlockSpec((1, S // KEY_TILE, LANES, KEY_TILE), lambda b, p, i: (b, 0, p, 0)),
                  pl.BlockSpec((1, 1, LANES), lambda b, p, i: (p, 0, 0)),
                  full(lq1), full(lk1), full(lq2), full(lk2), full(subg)],
        out_specs=pl.BlockSpec((1, tq, LANES), lambda b, p, i: (b, i, p)),
        out_shape=jax.ShapeDtypeStruct((B, S, DIFF_WIDTH), BF16),
        scratch_shapes=[pltpu.VMEM((4, 1, tq), F32), pltpu.VMEM((4, 1, tq), F32), pltpu.VMEM((4, LANES, tq), F32),
                        pltpu.VMEM((2, 2, KEY_TILE, tq), F32)],
        compiler_params=_cparams(("parallel", "parallel", "arbitrary")),
        name="diff_attn",
    )(dqt, dk, dvt, slopes, lq1, lk1, lq2, lk2, subg)


def _mla_attn_kernel(qt_ref, k_ref, vt_ref, o_ref, m_scr, l_scr, acc_scr, bias_scr, *, tq):
    qt = qt_ref[0]
    qt_list = [qt[hh * MLA_HEAD_PAD:(hh + 1) * MLA_HEAD_PAD, :] for hh in range(2)]
    k_lanes = [slice(hh * MLA_HEAD_PAD, (hh + 1) * MLA_HEAD_PAD) for hh in range(2)]

    @pl.when(pl.program_id(2) == 0)
    def _():
        _, _, allowed = _chunk_allowed_t(tq)
        bias_scr[0] = jnp.zeros((KEY_TILE, tq), F32)
        bias_scr[1] = jnp.where(allowed, 0.0, NEG_BIG)

    _flash_maps(qt_list, k_ref, vt_ref, k_lanes, lambda i, is_own: bias_scr[is_own], lambda i, n: 0.0,
                m_scr, l_scr, acc_scr)
    orow = lax.broadcasted_iota(jnp.int32, (LANES, tq), 0)
    out_t = jnp.where(orow < MLA_V_DIM, acc_scr[0] / l_scr[0], acc_scr[1] / l_scr[1])
    o_ref[0] = out_t.T.astype(o_ref.dtype)


def _mla_attn(mqt, mk, mvt):
    B, S, _ = mk.shape
    tq = KEY_TILE
    n_pairs = MLA_HEADS // 2
    kern = functools.partial(_mla_attn_kernel, tq=tq)
    return pl.pallas_call(
        kern,
        grid=(B, n_pairs, S // tq),
        in_specs=[pl.BlockSpec((1, 2 * MLA_HEAD_PAD, tq), lambda b, p, i: (b, p, i)),
                  pl.BlockSpec((1, S, 2 * MLA_HEAD_PAD), lambda b, p, i: (b, 0, p)),
                  pl.BlockSpec((1, S // KEY_TILE, LANES, KEY_TILE), lambda b, p, i: (b, 0, p, 0))],
        out_specs=pl.BlockSpec((1, tq, LANES), lambda b, p, i: (b, i, p)),
        out_shape=jax.ShapeDtypeStruct((B, S, MLA_WIDTH), F32),
        scratch_shapes=[pltpu.VMEM((2, 1, tq), F32), pltpu.VMEM((2, 1, tq), F32), pltpu.VMEM((2, LANES, tq), F32),
                        pltpu.VMEM((2, KEY_TILE, tq), F32)],
        compiler_params=_cparams(("parallel", "parallel", "arbitrary")),
        name="mla_attn",
    )(mqt, mk, mvt)


ROUTER_PAD = LANES


def _out_proj_kernel(x_ref, ylru_ref, ydiff_ref, ymla_ref, mlag_ref, wo_ref, g2_ref, wrh_ref, wrl_ref, rb_ref,
                     x1_ref, h2_ref, eid_ref, gate_ref, cnt_ref):
    @pl.when(pl.program_id(0) == 0)
    def _():
        cnt_ref[...] = jnp.zeros(cnt_ref.shape, F32)

    ymla = (_rms(ymla_ref[...], MLA_WIDTH) * mlag_ref[...]).astype(BF16)
    x1 = (x_ref[...]
          + _dot(ylru_ref[...], wo_ref[0:LRU_WIDTH, :])
          + _dot(ydiff_ref[...], wo_ref[LRU_WIDTH:LRU_WIDTH + DIFF_WIDTH, :])
          + _dot(ymla, wo_ref[LRU_WIDTH + DIFF_WIDTH:, :]))
    x1_ref[...] = x1

    h2 = _rms(x1, D_MODEL) * g2_ref[...]
    hi, lo = _split_bf16(h2)
    h2_ref[...] = hi
    wrh = wrh_ref[...]
    logits = _dot(hi, wrh) + _dot(hi, wrl_ref[...]) + _dot(lo, wrh) + rb_ref[...]

    lane = lax.broadcasted_iota(jnp.int32, logits.shape, 1)
    big = jnp.int32(1 << 20)
    is_group = lane < N_GROUPS
    gl = jnp.where(is_group, logits, NEG_BIG)
    gmax = jnp.max(gl, axis=-1, keepdims=True)
    g_idx = jnp.min(jnp.where(gl == gmax, lane, big), axis=-1, keepdims=True)
    pg_top = 1.0 / jnp.sum(jnp.where(is_group, jnp.exp(gl - gmax), 0.0), axis=-1, keepdims=True)

    e_lane = lane - N_GROUPS
    in_group = (e_lane >= 0) & (e_lane < N_EXPERTS) & ((e_lane // EXPERTS_PER_GROUP) == g_idx)
    el = jnp.where(in_group, logits, NEG_BIG)
    m1 = jnp.max(el, axis=-1, keepdims=True)
    i1 = jnp.min(jnp.where(el == m1, lane, big), axis=-1, keepdims=True)
    el2 = jnp.where(lane == i1, NEG_BIG, el)
    m2 = jnp.max(el2, axis=-1, keepdims=True)
    i2 = jnp.min(jnp.where(el2 == m2, lane, big), axis=-1, keepdims=True)
    e2 = jnp.exp(m2 - m1)
    g1 = pg_top / (1.0 + e2)
    g2 = pg_top * e2 / (1.0 + e2)

    two = lax.broadcasted_iota(jnp.int32, eid_ref.shape, 1)
    eid_ref[...] = jnp.where(two == 0, i1 - N_GROUPS, i2 - N_GROUPS)
    gate_ref[...] = jnp.where(two == 0, g1, g2)
    onehot = ((e_lane == i1 - N_GROUPS) | (e_lane == i2 - N_GROUPS)).astype(F32)
    cnt_ref[...] += jnp.sum(onehot, axis=0, keepdims=True)


def _out_proj(x, ylru, ydiff, ymla, mlag, wo, g2, wrh, wrl, rb, tm):
    N = x.shape[0]
    full = lambda a: pl.BlockSpec(a.shape, lambda i: (0,) * a.ndim)
    tok = lambda width: pl.BlockSpec((tm, width), lambda i: (i, 0))
    return pl.pallas_call(
        _out_proj_kernel,
        grid=(N // tm,),
        in_specs=[tok(D_MODEL), tok(LRU_WIDTH), tok(DIFF_WIDTH), tok(MLA_WIDTH), full(mlag), full(wo), full(g2),
                  full(wrh), full(wrl), full(rb)],
        out_specs=[tok(D_MODEL), tok(D_MODEL), tok(2), tok(2), pl.BlockSpec((1, ROUTER_PAD), lambda i: (0, 0))],
        out_shape=[jax.ShapeDtypeStruct((N, D_MODEL), F32), jax.ShapeDtypeStruct((N, D_MODEL), BF16),
                   jax.ShapeDtypeStruct((N, 2), jnp.int32),
                   jax.ShapeDtypeStruct((N, 2), F32), jax.ShapeDtypeStruct((1, ROUTER_PAD), F32)],
        compiler_params=_cparams(("arbitrary",)),
        name="out_proj",
    )(x, ylru, ydiff, ymla, mlag, wo, g2, wrh, wrl, rb)


TOK_TILE = 256
RUN_CHUNK = 16
TAB_ROWS = 8


def _rank_kernel(eid_ref, start_ref, loc_ref, tab_ref, carry):
    tm = eid_ref.shape[0]

    @pl.when(pl.program_id(0) == 0)
    def _():
        carry[...] = start_ref[...]

    lane = lax.broadcasted_iota(jnp.int32, (tm, LANES), 1)
    r = lax.broadcasted_iota(jnp.int32, (tm, tm), 0)
    c = lax.broadcasted_iota(jnp.int32, (tm, tm), 1)
    lower = (c < r).astype(BF16)
    eid = eid_ref[...]
    onehot = [lane == eid[:, k:k + 1] for k in range(2)]
    cnt = [jnp.sum(oh.astype(F32), axis=0, keepdims=True) for oh in onehot]
    cnt_tile = cnt[0] + cnt[1]

    chunks = jnp.floor((cnt_tile + (RUN_CHUNK - 1)) * (1.0 / RUN_CHUNK))
    lane8 = lax.broadcasted_iota(jnp.int32, (TAB_ROWS, LANES), 1)
    incl = jnp.broadcast_to(chunks, (TAB_ROWS, LANES))
    shift = 1
    while shift < LANES:
        incl = incl + jnp.where(lane8 >= shift, pltpu.roll(incl, shift, 1), 0.0)
        shift *= 2
    offs = (incl[0:1] - chunks) * RUN_CHUNK

    locs = []
    for k in range(2):
        before = _dot(lower, onehot[k].astype(BF16))
        first = offs if k == 0 else offs + cnt[0]
        locs.append(jnp.sum(jnp.where(onehot[k], before + first, 0.0), axis=-1, keepdims=True))
    two = lax.broadcasted_iota(jnp.int32, loc_ref.shape, 1)
    loc_ref[...] = jnp.where(two == 0, locs[0], locs[1]).astype(jnp.int32)

    row8 = lax.broadcasted_iota(jnp.int32, (TAB_ROWS, LANES), 0)
    base = carry[...]
    tab = jnp.where(row8 == 0, cnt_tile, jnp.where(row8 == 1, offs, jnp.where(row8 == 2, base, 0.0)))
    tab_ref[...] = tab.astype(jnp.int32)
    carry[...] = base + cnt_tile


def _rank(eid, starts):
    N = eid.shape[0]
    n_tiles = N // TOK_TILE
    return pl.pallas_call(
        _rank_kernel,
        grid=(n_tiles,),
        in_specs=[pl.BlockSpec((TOK_TILE, 2), lambda i: (i, 0)), pl.BlockSpec((1, LANES), lambda i: (0, 0))],
        out_specs=[pl.BlockSpec((TOK_TILE, 2), lambda i: (i, 0)), pl.BlockSpec((TAB_ROWS, LANES), lambda i: (i, 0))],
        out_shape=[jax.ShapeDtypeStruct((N, 2), jnp.int32),
                   jax.ShapeDtypeStruct((n_tiles * TAB_ROWS, LANES), jnp.int32)],
        scratch_shapes=[pltpu.VMEM((1, LANES), F32)],
        compiler_params=_cparams(("arbitrary",)),
        name="rank",
    )(eid, starts)


SORT_ROWS = 2 * TOK_TILE + N_EXPERTS * RUN_CHUNK


def _run_chunks(tab_ref, tile, e):
    cnt = tab_ref[(tile * 3 + 0) * N_EXPERTS + e]
    offs = tab_ref[(tile * 3 + 1) * N_EXPERTS + e]
    base = tab_ref[(tile * 3 + 2) * N_EXPERTS + e]
    return lax.shift_right_logical(cnt + (RUN_CHUNK - 1), RUN_CHUNK.bit_length() - 1), offs, base


def _rows(ref, row, n):
    return ref.at[pl.ds(pl.multiple_of(row * ROW_TILES, ROW_TILES), n * ROW_TILES)]


def _for_each_chunk(tab_ref, tile, fn):
    for e in range(N_EXPERTS):
        chunks, offs, base = _run_chunks(tab_ref, tile, e)

        def one(k, carry):
            fn(offs + k * RUN_CHUNK, base + k * RUN_CHUNK)
            return carry

        lax.fori_loop(0, chunks, one, 0)


def _drain_chunks(tab_ref, tile, wait_one):
    total = 0
    for e in range(N_EXPERTS):
        total = total + _run_chunks(tab_ref, tile, e)[0]

    def one(k, carry):
        wait_one()
        return carry

    lax.fori_loop(0, total, one, 0)


def _dispatch_kernel(tab_ref, ztab_ref, h_ref, loc_ref, rows_hbm, sbuf, zbuf, sems, zsem):
    i = pl.program_id(0)
    n_tiles = pl.num_programs(0)
    slot = lax.rem(i, 2)
    T = TOK_TILE

    def run_copy(s, offs_row, base_row):
        return pltpu.make_async_copy(_rows(sbuf, s * SORT_ROWS + offs_row, RUN_CHUNK),
                                     _rows(rows_hbm, base_row, RUN_CHUNK), sems.at[s])

    @pl.when(i == 0)
    def _():
        zbuf[...] = jnp.zeros(zbuf.shape, F32)

        def zero_chunk(first_row):
            return pltpu.make_async_copy(zbuf, _rows(rows_hbm, first_row, RUN_CHUNK), zsem)

        def done(k, carry):
            zero_chunk(0).wait()
            return carry

        for e in range(N_EXPERTS):
            zero_chunk(ztab_ref[e]).start()
        lax.fori_loop(0, N_EXPERTS, done, 0)
        total = 0
        for e in range(N_EXPERTS):
            start, chunks = ztab_ref[N_EXPERTS + e], ztab_ref[2 * N_EXPERTS + e]
            total = total + chunks

            def fill(k, carry, start=start):
                zero_chunk(start + k * RUN_CHUNK).start()
                return carry

            lax.fori_loop(0, chunks, fill, 0)
        lax.fori_loop(0, total, done, 0)

    @pl.when(i > 0)
    def _():
        _drain_chunks(tab_ref, i - 1, lambda: run_copy(1 - slot, 0, 0).wait())

    locf = loc_ref[...].astype(F32)
    loc_lanes = [jnp.broadcast_to(locf[:, k:k + 1], (T, LANES)).T[0:1, :] for k in range(2)]
    pos = lax.broadcasted_iota(jnp.int32, (SORT_ROWS, T), 0).astype(F32)
    perm = jnp.where((pos == loc_lanes[0]) | (pos == loc_lanes[1]), 1.0, 0.0).astype(BF16)
    sorted_rows = _dot(perm, h_ref[...])
    for c in range(ROW_TILES):
        sbuf[pl.ds(slot * (SORT_ROWS * ROW_TILES) + c, SORT_ROWS, stride=ROW_TILES), :] = (
            sorted_rows[:, c * LANES:(c + 1) * LANES])

    _for_each_chunk(tab_ref, i, lambda offs_row, base_row: run_copy(slot, offs_row, base_row).start())

    @pl.when(i == n_tiles - 1)
    def _():
        _drain_chunks(tab_ref, i, lambda: run_copy(slot, 0, 0).wait())


def _dispatch(tabs, ztab, h2, loc, n_rows):
    N = h2.shape[0]
    return pl.pallas_call(
        _dispatch_kernel,
        grid_spec=pltpu.PrefetchScalarGridSpec(
            num_scalar_prefetch=2,
            grid=(N // TOK_TILE,),
            in_specs=[pl.BlockSpec((TOK_TILE, D_MODEL), lambda i, t, z: (i, 0)),
                      pl.BlockSpec((TOK_TILE, 2), lambda i, t, z: (i, 0))],
            out_specs=pl.BlockSpec(memory_space=pl.ANY),
            scratch_shapes=[pltpu.VMEM((2 * SORT_ROWS * ROW_TILES, LANES), F32),
                            pltpu.VMEM((RUN_CHUNK * ROW_TILES, LANES), F32),
                            pltpu.SemaphoreType.DMA((2,)), pltpu.SemaphoreType.DMA(())],
        ),
        out_shape=jax.ShapeDtypeStruct((n_rows * ROW_TILES, LANES), F32),
        compiler_params=_cparams(("arbitrary",)),
        name="dispatch",
    )(tabs, ztab, h2, loc)


def _expert_kernel(be_ref, nused_ref, rows_ref, w1_ref, w3_ref, w2_ref, y_ref):
    del be_ref

    @pl.when(pl.program_id(0) < nused_ref[0])
    def _():
        h = jnp.concatenate([rows_ref[pl.ds(c, ROW_BLOCK, stride=ROW_TILES), :] for c in range(ROW_TILES)],
                            axis=1).astype(BF16)
        a = _dot(h, w1_ref[0].astype(BF16))
        b = _dot(h, w3_ref[0].astype(BF16))
        z = (a * jax.nn.sigmoid(a) * b).astype(BF16)
        y = _dot(z, w2_ref[0].astype(BF16))
        for c in range(ROW_TILES):
            y_ref[pl.ds(c, ROW_BLOCK, stride=ROW_TILES), :] = y[:, c * LANES:(c + 1) * LANES]

    @pl.when(pl.program_id(0) >= nused_ref[0])
    def _():
        y_ref[...] = jnp.zeros(y_ref.shape, y_ref.dtype)


def _experts(blk_expert, n_used, rows, w1, w3, w2):
    n_blk = blk_expert.shape[0]
    row_blk = lambda i, be, nu: (jnp.minimum(i, nu[0] - 1), 0)
    wspec = lambda shape: pl.BlockSpec((1,) + shape, lambda i, be, nu: (be[i], 0, 0))
    return pl.pallas_call(
        _expert_kernel,
        grid_spec=pltpu.PrefetchScalarGridSpec(
            num_scalar_prefetch=2,
            grid=(n_blk,),
            in_specs=[pl.BlockSpec((ROW_BLOCK * ROW_TILES, LANES), row_blk),
                      wspec((D_MODEL, D_EXPERT)), wspec((D_MODEL, D_EXPERT)), wspec((D_EXPERT, D_MODEL))],
            out_specs=pl.BlockSpec((ROW_BLOCK * ROW_TILES, LANES), lambda i, be, nu: (i, 0)),
        ),
        out_shape=jax.ShapeDtypeStruct((n_blk * ROW_BLOCK * ROW_TILES, LANES), F32),
        compiler_params=_cparams(("arbitrary",)),
        name="experts",
    )(blk_expert, n_used, rows, w1, w3, w2)


def _combine_kernel(tab_ref, x1_ref, gate_ref, loc_ref, y_hbm, o_ref, ybuf, sems):
    i = pl.program_id(0)
    n_tiles = pl.num_programs(0)
    slot = lax.rem(i, 2)
    T = TOK_TILE

    def run_copy(s, offs_row, base_row):
        return pltpu.make_async_copy(_rows(y_hbm, base_row, RUN_CHUNK),
                                     _rows(ybuf, s * SORT_ROWS + offs_row, RUN_CHUNK), sems.at[s])

    def fetch(tile, s):
        _for_each_chunk(tab_ref, tile, lambda offs_row, base_row: run_copy(s, offs_row, base_row).start())

    @pl.when(i == 0)
    def _():
        ybuf[...] = jnp.zeros(ybuf.shape, F32)
        fetch(0, 0)

    @pl.when(i + 1 < n_tiles)
    def _():
        fetch(i + 1, 1 - slot)

    _drain_chunks(tab_ref, i, lambda: run_copy(slot, 0, 0).wait())

    y_sorted = jnp.concatenate([ybuf[pl.ds(slot * (SORT_ROWS * ROW_TILES) + c, SORT_ROWS, stride=ROW_TILES), :]
                                for c in range(ROW_TILES)], axis=1).astype(BF16)
    gate = gate_ref[...]
    loc = loc_ref[...]
    pos = lax.broadcasted_iota(jnp.int32, (T, SORT_ROWS), 1)
    g = (jnp.where(pos == loc[:, 0:1], gate[:, 0:1], 0.0) + jnp.where(pos == loc[:, 1:2], gate[:, 1:2], 0.0))
    g_hi, g_lo = _split_bf16(g)
    o_ref[...] = x1_ref[...] + _dot(g_hi, y_sorted) + _dot(g_lo, y_sorted)


def _combine(tabs, x1, gate, loc, y_rows):
    N = gate.shape[0]
    tok = lambda width: pl.BlockSpec((TOK_TILE, width), lambda i, t: (i, 0))
    return pl.pallas_call(
        _combine_kernel,
        grid_spec=pltpu.PrefetchScalarGridSpec(
            num_scalar_prefetch=1,
            grid=(N // TOK_TILE,),
            in_specs=[tok(D_MODEL), tok(2), tok(2), pl.BlockSpec(memory_space=pl.ANY)],
            out_specs=tok(D_MODEL),
            scratch_shapes=[pltpu.VMEM((2 * SORT_ROWS * ROW_TILES, LANES), F32), pltpu.SemaphoreType.DMA((2,))],
        ),
        out_shape=jax.ShapeDtypeStruct((N, D_MODEL), F32),
        compiler_params=_cparams(("arbitrary",)),
        name="combine",
    )(tabs, x1, gate, loc, y_rows)


def _block_diag(w):
    n, a, b = w.shape
    eye = np.eye(n, dtype=np.float32)
    return (w[:, :, None, :] * eye[:, None, :, None]).reshape(n * a, n * b)


def _pad_heads(v, used):
    lead = v.shape[:-1]
    v = v.reshape(lead + (MLA_HEADS, used))
    v = jnp.pad(v, [(0, 0)] * len(lead) + [(0, 0), (0, MLA_HEAD_PAD - used)])
    return v.reshape(lead + (MLA_QK_PAD,))


def _rotary_lane_tables(seq):
    half = MLA_ROPE_DIM // 2
    inv_freq = ROPE_THETA ** (-jnp.arange(half, dtype=F32) / half)
    ang = jnp.arange(seq, dtype=F32)[:, None] * inv_freq[None, :]
    cos, sin = jnp.cos(ang), jnp.sin(ang)
    ones = jnp.ones((seq, MLA_NOPE_DIM), F32)
    tail = MLA_HEAD_PAD - MLA_QK_DIM
    cos_t = jnp.concatenate([ones, cos, cos, jnp.ones((seq, tail), F32)], axis=1)
    sin_t = jnp.concatenate([0.0 * ones, -sin, sin, jnp.zeros((seq, tail), F32)], axis=1)
    return cos_t, sin_t


def _pick_tile(n, pref):
    t = min(n, pref)
    while n % t:
        t //= 2
    return t


def kernel(x, norm1_g, w_in, lru_conv_w, lru_conv_b, lru_wa, lru_ba, lru_wx, lru_bx, lru_lambda, lru_out_g,
           diff_q_g, diff_k_g, diff_lq1, diff_lk1, diff_lq2, diff_lk2, diff_sub_g,
           mla_cq_g, mla_ckv_g, mla_w_uq, mla_w_ukv, mla_q_g, mla_k_g, mla_out_g, w_out, norm2_g,
           router_g_w, router_g_b, router_e_w, router_e_b, exp_w1, exp_w3, exp_w2):
    B, S, D = x.shape
    N = B * S
    depth = w_in.shape[0]
    assert S % KEY_TILE == 0 and D == D_MODEL
    ts = _pick_tile(S, 512)
    tc = _pick_tile(S, 512)
    assert N % TOK_TILE == 0
    tm = _pick_tile(N, 512)

    cos_t, sin_t = _rotary_lane_tables(S)
    n_dgroups = DIFF_QK_WIDTH // DIFF_HEAD_DIM
    gsum = jnp.asarray(np.kron(np.eye(n_dgroups), np.ones((DIFF_HEAD_DIM, DIFF_HEAD_DIM))), BF16)
    slopes = np.asarray([2.0 ** (-ALIBI_MAX_BIAS * (h + 1) / DIFF_HEADS) * LOG2E for h in range(DIFF_HEADS)], np.float32)
    slopes = jnp.asarray(np.pad(slopes.reshape(DIFF_HEADS // 2, 1, 2), ((0, 0), (0, 0), (0, LANES - 2))))
    place = np.zeros((MLA_ROPE_DIM, MLA_HEADS, MLA_HEAD_PAD), np.float32)
    place[np.arange(MLA_ROPE_DIM), :, MLA_NOPE_DIM + np.arange(MLA_ROPE_DIM)] = 1.0
    place = jnp.asarray(place.reshape(MLA_ROPE_DIM, MLA_QK_PAD))
    row = lambda v: v.reshape(1, -1).astype(F32)

    n_blk = -(-(2 * N + N_EXPERTS * (RUN_CHUNK + ROW_BLOCK)) // ROW_BLOCK) + 1
    n_rows = n_blk * ROW_BLOCK

    for l in range(depth):
        wl = w_in[l]
        zc = lambda n: jnp.zeros((D, n), F32)
        o = np.cumsum([0, 256, 256, 384, 384, 384, 192, 128, 32])
        w_cat = jnp.concatenate([wl[:, o[0]:o[5]], wl[:, o[5]:o[6]], zc(C_CKV - C_CQ - MLA_Q_RANK),
                                 wl[:, o[6]:o[7]], wl[:, o[7]:o[8]], zc(D_IN_PAD - C_KR - MLA_ROPE_DIM)],
                                axis=1).astype(BF16)
        dqg = row(jnp.tile(diff_q_g[l], n_dgroups) * (DIFF_HEAD_DIM ** -0.5 * LOG2E))
        dkg = row(jnp.tile(diff_k_g[l], n_dgroups))
        wuq = _pad_heads(mla_w_uq[l], MLA_QK_DIM).astype(BF16)
        mqg = row(_pad_heads(jnp.tile(mla_q_g[l], MLA_HEADS) * (MLA_QK_DIM ** -0.5 * LOG2E), MLA_QK_DIM))
        mkg = row(_pad_heads(jnp.tile(mla_k_g[l], MLA_HEADS), MLA_QK_DIM))
        wukv = mla_w_ukv[l].reshape(MLA_KV_RANK, MLA_HEADS, MLA_NOPE_DIM + MLA_V_DIM)
        wk_nope = _pad_heads(wukv[:, :, :MLA_NOPE_DIM].reshape(MLA_KV_RANK, -1), MLA_NOPE_DIM)
        wk = jnp.concatenate([wk_nope, place], axis=0).astype(BF16)
        wv = wukv[:, :, MLA_NOPE_DIM:].reshape(MLA_KV_RANK, MLA_WIDTH).astype(BF16)

        lru_in, dq, dk, dv, mq, mk, mv = _in_proj(
            x, row(norm1_g[l]), w_cat, gsum, dqg, dkg, row(mla_cq_g[l]), wuq, mqg, row(mla_ckv_g[l]), wk, mkg, wv,
            cos_t, sin_t, ts)

        y_lru = _lru(lru_in, lru_conv_w[l], row(lru_conv_b[l]), _block_diag(lru_wa[l]).astype(BF16),
                     row(lru_ba[l]), _block_diag(lru_wx[l]).astype(BF16), row(lru_bx[l]), row(lru_lambda[l]),
                     row(lru_out_g[l]), tc)

        lambda_init = 0.8 - 0.6 * math.exp(-0.3 * l)
        y_diff = _diff_attn(dq, dk, dv, slopes, row(diff_lq1[l]), row(diff_lk1[l]), row(diff_lq2[l]),
                            row(diff_lk2[l]), row(jnp.tile(diff_sub_g[l], 2)), lambda_init)
        y_mla = _mla_attn(mq, mk, mv)

        wr = jnp.concatenate([router_g_w[l], router_e_w[l],
                              jnp.zeros((D, ROUTER_PAD - N_GROUPS - N_EXPERTS), F32)], axis=1)
        wrh = wr.astype(BF16)
        wrl = (wr - wrh.astype(F32)).astype(BF16)
        rb = jnp.pad(jnp.concatenate([router_g_b[l], router_e_b[l]]), (0, ROUTER_PAD - N_GROUPS - N_EXPERTS))
        x1, h2, eid, gate, cnt = _out_proj(
            x.reshape(N, D), y_lru.reshape(N, LRU_WIDTH), y_diff.reshape(N, DIFF_WIDTH), y_mla.reshape(N, MLA_WIDTH),
            row(mla_out_g[l]), w_out[l].astype(BF16), row(norm2_g[l]), wrh, wrl, row(rb), tm)

        counts = cnt[0, N_GROUPS:N_GROUPS + N_EXPERTS].astype(jnp.int32)
        padded = (counts + RUN_CHUNK + ROW_BLOCK - 1) // ROW_BLOCK * ROW_BLOCK
        pad_ends = jnp.cumsum(padded)
        pad_starts = pad_ends - padded
        blk_row0 = jnp.arange(n_blk, dtype=jnp.int32) * ROW_BLOCK
        blk_expert = jnp.minimum(jnp.sum((pad_ends[None, :] <= blk_row0[:, None]).astype(jnp.int32), axis=1),
                                 N_EXPERTS - 1)
        n_used = (pad_ends[-1:] // ROW_BLOCK).astype(jnp.int32)
        starts = jnp.pad(pad_starts.astype(F32), (0, LANES - N_EXPERTS)).reshape(1, LANES)
        tail_start = pad_starts + counts
        last = jnp.arange(N_EXPERTS) == N_EXPERTS - 1
        tail_ends = jnp.where(last, n_rows, pad_ends)
        aligned_start = (tail_start + RUN_CHUNK - 1) // RUN_CHUNK * RUN_CHUNK
        ztab = jnp.concatenate([tail_start, aligned_start, (tail_ends - aligned_start) // RUN_CHUNK]).astype(jnp.int32)

        loc, tab = _rank(eid, starts)
        tabs = tab.reshape(N // TOK_TILE, TAB_ROWS, LANES)[:, :3, :N_EXPERTS].reshape(-1)
        rows = _dispatch(tabs, ztab, h2, loc, n_rows)
        y_rows = _experts(blk_expert, n_used, rows, exp_w1[l], exp_w3[l], exp_w2[l])
        x = _combine(tabs, x1, gate, loc, y_rows).reshape(B, S, D)
    return x
```

```python
import functools
import math

import jax
import jax.numpy as jnp
import numpy as np
from jax import lax
from jax.experimental import pallas as pl
from jax.experimental.pallas import tpu as pltpu

F32 = jnp.float32
BF16 = jnp.bfloat16

D_MODEL = 1024
CHUNK = 64
LRU_WIDTH = 256
CONV_WIDTH = 4
RG_C = 8.0
DIFF_HEADS = 6
DIFF_HEAD_DIM = 32
DIFF_V_DIM = 64
DIFF_QK_WIDTH = DIFF_HEADS * 2 * DIFF_HEAD_DIM
DIFF_WIDTH = DIFF_HEADS * DIFF_V_DIM
ALIBI_MAX_BIAS = 8.0
MLA_HEADS = 6
MLA_Q_RANK = 192
MLA_KV_RANK = 128
MLA_NOPE_DIM = 64
MLA_ROPE_DIM = 32
MLA_V_DIM = 64
MLA_QK_DIM = MLA_NOPE_DIM + MLA_ROPE_DIM
MLA_WIDTH = MLA_HEADS * MLA_V_DIM
ROPE_THETA = 10000.0
N_GROUPS = 4
EXPERTS_PER_GROUP = 8
N_EXPERTS = N_GROUPS * EXPERTS_PER_GROUP
D_EXPERT = 256
ROW_BLOCK = 256
EPS = 1e-6

LANES = 128
ROW_TILES = D_MODEL // LANES
MLA_HEAD_PAD = LANES
MLA_QK_PAD = MLA_HEADS * MLA_HEAD_PAD
VMEM_LIMIT = 56 * 1024 * 1024
NEG_BIG = -1e30
KEY_TILE = 256
DIFF_TILES_PER_STEP = 2
MLA_TILES_PER_STEP = 4
LOG2E = math.log2(math.e)

C_LRU = 0
C_DQ = 512
C_DK = 896
C_DV = 1280
C_CQ = 1664
C_CKV = 1856
C_KR = 1984
D_IN = 2016
D_IN_PAD = 2048


def _cparams(sem):
    return pltpu.CompilerParams(dimension_semantics=sem, vmem_limit_bytes=VMEM_LIMIT)


def _rms(v, width):
    return v * lax.rsqrt(jnp.sum(v * v, axis=-1, keepdims=True) * (1.0 / width) + EPS)


def _dot(a, b):
    return jnp.dot(a, b, preferred_element_type=F32)


def _split_bf16(v):
    hi = v.astype(BF16)
    lo = (v - hi.astype(F32)).astype(BF16)
    return hi, lo


def _in_proj_kernel(x_ref, g1_ref, w_ref, gsum_ref, dqg_ref, dkg_ref, cqg_ref, wuq_ref, mqg_ref, ckvg_ref,
                    wk_ref, mkg_ref, wv_ref, cos_ref, sin_ref,
                    lru_ref, dqt_ref, dk_ref, dvt_ref, mqt_ref, mk_ref, mvt_ref, w_bf):
    @pl.when((pl.program_id(0) == 0) & (pl.program_id(1) == 0))
    def _():
        w_bf[:, 0:D_IN] = w_ref[...].astype(BF16)
        w_bf[:, D_IN:] = jnp.zeros((D_MODEL, D_IN_PAD - D_IN), BF16)

    x = x_ref[0]
    hn = (_rms(x, D_MODEL) * g1_ref[...]).astype(BF16)
    p = _dot(hn, w_bf[...])
    lru_ref[0] = p[:, C_LRU:C_LRU + 2 * LRU_WIDTH]

    gsum = gsum_ref[...]

    def group_norm32(v, gain):
        hi, lo = _split_bf16(v * v)
        ss = _dot(hi, gsum) + _dot(lo, gsum)
        return v * lax.rsqrt(ss * (1.0 / DIFF_HEAD_DIM) + EPS) * gain

    def store_tiles(ref, vt):
        for c in range(vt.shape[1] // KEY_TILE):
            ref[0, c] = vt[:, c * KEY_TILE:(c + 1) * KEY_TILE].astype(BF16)

    dqt_ref[0] = group_norm32(p[:, C_DQ:C_DQ + DIFF_QK_WIDTH], dqg_ref[...]).T.astype(BF16)
    dk_ref[0] = group_norm32(p[:, C_DK:C_DK + DIFF_QK_WIDTH], dkg_ref[...]).astype(BF16)
    store_tiles(dvt_ref, p[:, C_DV:C_DV + DIFF_WIDTH].T)

    cos = cos_ref[...]
    sin = sin_ref[...]
    lane = lax.broadcasted_iota(jnp.int32, cos.shape, 1)
    first_half = lane < MLA_NOPE_DIM + MLA_ROPE_DIM // 2

    def head_norm_rotary(v, gain):
        outs = []
        for h in range(MLA_HEADS):
            c = v[:, h * MLA_HEAD_PAD:(h + 1) * MLA_HEAD_PAD]
            c = _rms(c, MLA_QK_DIM) * gain[:, h * MLA_HEAD_PAD:(h + 1) * MLA_HEAD_PAD]
            swapped = jnp.where(first_half, pltpu.roll(c, MLA_HEAD_PAD - MLA_ROPE_DIM // 2, 1),
                                pltpu.roll(c, MLA_ROPE_DIM // 2, 1))
            outs.append(c * cos + swapped * sin)
        return jnp.concatenate(outs, axis=1)

    cq = (_rms(p[:, C_CQ:C_CQ + MLA_Q_RANK], MLA_Q_RANK) * cqg_ref[...]).astype(BF16)
    mqt_ref[0] = head_norm_rotary(_dot(cq, wuq_ref[...]), mqg_ref[...]).T.astype(BF16)

    ckv = (_rms(p[:, C_CKV:C_CKV + MLA_KV_RANK], MLA_KV_RANK) * ckvg_ref[...]).astype(BF16)
    kcat = jnp.concatenate([ckv, p[:, C_KR:C_KR + MLA_ROPE_DIM].astype(BF16)], axis=1)
    mk_ref[0] = head_norm_rotary(_dot(kcat, wk_ref[...]), mkg_ref[...]).astype(BF16)
    store_tiles(mvt_ref, _dot(ckv, wv_ref[...]).T)


def _in_proj(x, g1, w, gsum, dqg, dkg, cqg, wuq, mqg, ckvg, wk, mkg, wv, cos_t, sin_t, ts):
    B, S, _ = x.shape
    full = lambda a: pl.BlockSpec(a.shape, lambda b, i: (0,) * a.ndim)
    tok = lambda width: pl.BlockSpec((1, ts, width), lambda b, i: (b, i, 0))
    tok_t = lambda width: pl.BlockSpec((1, width, ts), lambda b, i: (b, 0, i))
    nkt = ts // KEY_TILE
    tiles_t = lambda width: pl.BlockSpec((1, nkt, width, KEY_TILE), lambda b, i: (b, i, 0, 0))
    shp = lambda shape, dt: jax.ShapeDtypeStruct(shape, dt)
    return pl.pallas_call(
        _in_proj_kernel,
        grid=(B, S // ts),
        in_specs=[tok(D_MODEL), full(g1), full(w), full(gsum), full(dqg), full(dkg), full(cqg), full(wuq),
                  full(mqg), full(ckvg), full(wk), full(mkg), full(wv),
                  pl.BlockSpec((ts, LANES), lambda b, i: (i, 0)), pl.BlockSpec((ts, LANES), lambda b, i: (i, 0))],
        out_specs=[tok(2 * LRU_WIDTH), tok_t(DIFF_QK_WIDTH), tok(DIFF_QK_WIDTH), tiles_t(DIFF_WIDTH),
                   tok_t(MLA_QK_PAD), tok(MLA_QK_PAD), tiles_t(MLA_WIDTH)],
        out_shape=[shp((B, S, 2 * LRU_WIDTH), F32), shp((B, DIFF_QK_WIDTH, S), BF16), shp((B, S, DIFF_QK_WIDTH), BF16),
                   shp((B, S // KEY_TILE, DIFF_WIDTH, KEY_TILE), BF16), shp((B, MLA_QK_PAD, S), BF16),
                   shp((B, S, MLA_QK_PAD), BF16), shp((B, S // KEY_TILE, MLA_WIDTH, KEY_TILE), BF16)],
        scratch_shapes=[pltpu.VMEM((D_MODEL, D_IN_PAD), BF16)],
        compiler_params=_cparams(("arbitrary", "arbitrary")),
        name="in_proj",
    )(x, g1, w, gsum, dqg, dkg, cqg, wuq, mqg, ckvg, wk, mkg, wv, cos_t, sin_t)


LRU_HALO = 8


def _lru_kernel(blk_ref, cw_ref, cb_ref, wa_ref, ba_ref, wx_ref, bx_ref, lam_ref, og_ref, y_ref,
                ubuf, a_scr, b_scr, h_scr):
    B, tc, _ = blk_ref.shape
    W = LRU_WIDTH

    @pl.when(pl.program_id(0) == 0)
    def _():
        ubuf[:, 0:LRU_HALO, :] = jnp.zeros((B, LRU_HALO, W), F32)
        h_scr[...] = jnp.zeros(h_scr.shape, F32)

    ubuf[:, LRU_HALO:, :] = blk_ref[:, :, 0:W]
    xc = cb_ref[...][None]
    for j in range(CONV_WIDTH):
        off = LRU_HALO - (CONV_WIDTH - 1) + j
        xc = xc + cw_ref[j:j + 1, :][None] * ubuf[:, off:off + tc, :]
    ubuf[:, 0:LRU_HALO, :] = ubuf[:, tc:tc + LRU_HALO, :]

    x2 = xc.reshape(B * tc, W)
    xb = x2.astype(BF16)
    r = jax.nn.sigmoid(_dot(xb, wa_ref[...]) + ba_ref[...])
    gi = jax.nn.sigmoid(_dot(xb, wx_ref[...]) + bx_ref[...])
    nl = -lam_ref[...]
    softplus = jnp.maximum(nl, 0.0) + jnp.log(1.0 + jnp.exp(-jnp.abs(nl)))
    a = jnp.exp(-RG_C * r * softplus)
    bb = jnp.sqrt(1.0 - a * a) * gi * x2
    a_scr[...] = a.reshape(B, tc, W)
    b_scr[...] = bb.reshape(B, tc, W)

    def step(t, h):
        h = a_scr[:, pl.ds(t, 1), :] * h + b_scr[:, pl.ds(t, 1), :]
        b_scr[:, pl.ds(t, 1), :] = h
        return h

    h_scr[...] = lax.fori_loop(0, tc, step, h_scr[...], unroll=8)

    gate = blk_ref[:, :, W:2 * W]
    gelu = 0.5 * gate * (1.0 + jnp.tanh(math.sqrt(2.0 / math.pi) * (gate + 0.044715 * gate * gate * gate)))
    y = b_scr[...] * gelu
    y_ref[...] = (_rms(y, W) * og_ref[...][None]).astype(y_ref.dtype)


def _lru(lru_in, cw, cb, wa, ba, wx, bx, lam, og, tc):
    B, S, _ = lru_in.shape
    W = LRU_WIDTH
    full = lambda a: pl.BlockSpec(a.shape, lambda i: (0,) * a.ndim)
    return pl.pallas_call(
        _lru_kernel,
        grid=(S // tc,),
        in_specs=[pl.BlockSpec((B, tc, 2 * W), lambda i: (0, i, 0)), full(cw), full(cb), full(wa), full(ba),
                  full(wx), full(bx), full(lam), full(og)],
        out_specs=pl.BlockSpec((B, tc, W), lambda i: (0, i, 0)),
        out_shape=jax.ShapeDtypeStruct((B, S, W), BF16),
        scratch_shapes=[pltpu.VMEM((B, tc + LRU_HALO, W), F32), pltpu.VMEM((B, tc, W), F32),
                        pltpu.VMEM((B, tc, W), F32), pltpu.VMEM((B, 1, W), F32)],
        compiler_params=_cparams(("arbitrary",)),
        name="lru",
    )(lru_in, cw, cb, wa, ba, wx, bx, lam, og)


def _flash_maps(qt_list, k_ref, vt_ref, k_lanes, bias_of, tile_shift, m_scr, l_scr, acc_scr, tiles_per_step):
    qi = pl.program_id(2)
    n_maps = len(qt_list)
    tq = qt_list[0].shape[1]
    maps = range(n_maps)

    def biased_scores(j):
        jc = jnp.minimum(j, qi)
        start = pl.multiple_of(jc * KEY_TILE, KEY_TILE)
        kind = (j >= qi).astype(jnp.int32) + (j > qi).astype(jnp.int32)
        return tuple(_dot(k_ref[0, pl.ds(start, KEY_TILE), k_lanes[i]], qt_list[i]) + bias_of(i, kind)
                     for i in maps)

    for i in maps:
        m_scr[i] = jnp.full((1, tq), NEG_BIG, F32)
        l_scr[i] = jnp.zeros((1, tq), F32)
        acc_scr[i] = jnp.zeros(acc_scr.shape[1:], F32)

    def update(j, s_tile):
        vt = vt_ref[0, jnp.minimum(j, qi)]
        for i in maps:
            s = s_tile[i]
            c = tile_shift(i, qi - j)
            m_old = m_scr[i]
            m_new = jnp.maximum(m_old, jnp.max(s, axis=0, keepdims=True) + c)
            alpha = jnp.exp2(m_old - m_new)
            p = jnp.exp2(s - (m_new - c))
            m_scr[i] = m_new
            l_scr[i] = alpha * l_scr[i] + jnp.sum(p, axis=0, keepdims=True)
            acc_scr[i] = alpha * acc_scr[i] + _dot(vt, p.astype(BF16))

    def body(t, s_tile):
        for u in range(tiles_per_step):
            j = tiles_per_step * t + u
            s_ahead = biased_scores(j + 1)
            update(j, s_tile)
            s_tile = s_ahead
        return s_tile

    lax.fori_loop(0, qi // tiles_per_step + 1, body, biased_scores(0))


def _chunk_allowed_t(tq):
    krow = lax.broadcasted_iota(jnp.int32, (KEY_TILE, tq), 0)
    qcol = lax.broadcasted_iota(jnp.int32, (KEY_TILE, tq), 1)
    return krow, qcol, (krow // CHUNK) <= (qcol // CHUNK)


def _diff_attn_kernel(qt_ref, k_ref, vt_ref, slope_ref, lq1_ref, lk1_ref, lq2_ref, lk2_ref, subg_ref, o_ref,
                      m_scr, l_scr, acc_scr, bias_scr, *, lambda_init, tq):
    qt = qt_ref[0]
    feat = lax.broadcasted_iota(jnp.int32, qt.shape, 0)
    zero = jnp.zeros_like(qt)
    qt_list = [jnp.where(feat // DIFF_HEAD_DIM == i, qt, zero) for i in range(4)]
    slope = [slope_ref[0, 0:1, hh:hh + 1] for hh in range(2)]

    @pl.when(pl.program_id(2) == 0)
    def _():
        krow, qcol, allowed = _chunk_allowed_t(tq)
        kf, qf = krow.astype(F32), qcol.astype(F32)
        self_dist = qf - jnp.abs(qf - kf)
        for hh in range(2):
            bias_scr[hh, 0] = slope[hh] * kf
            bias_scr[hh, 1] = jnp.where(allowed, slope[hh] * self_dist, NEG_BIG)
            bias_scr[hh, 2] = jnp.full((KEY_TILE, tq), NEG_BIG, F32)

    def tile_shift(i, n_tiles):
        return -slope[i // 2] * (n_tiles * KEY_TILE).astype(F32)

    _flash_maps(qt_list, k_ref, vt_ref, [slice(None)] * 4, lambda i, kind: bias_scr[i // 2, kind],
                tile_shift, m_scr, l_scr, acc_scr, DIFF_TILES_PER_STEP)

    lam = (jnp.exp(jnp.sum(lq1_ref[...] * lk1_ref[...], axis=-1, keepdims=True))
           - jnp.exp(jnp.sum(lq2_ref[...] * lk2_ref[...], axis=-1, keepdims=True)) + lambda_init)
    orow = lax.broadcasted_iota(jnp.int32, (LANES, tq), 0)
    out_t = jnp.zeros((LANES, tq), F32)
    for hh in range(2):
        o = acc_scr[2 * hh] / l_scr[2 * hh] - lam * (acc_scr[2 * hh + 1] / l_scr[2 * hh + 1])
        mine = (orow // DIFF_V_DIM) == hh
        ms = jnp.sum(jnp.where(mine, o * o, 0.0), axis=0, keepdims=True) * (1.0 / DIFF_V_DIM)
        out_t = jnp.where(mine, o * lax.rsqrt(ms + EPS), out_t)
    o_ref[0] = (out_t.T * subg_ref[...] * (1.0 - lambda_init)).astype(o_ref.dtype)


def _diff_attn(dqt, dk, dvt, slopes, lq1, lk1, lq2, lk2, subg, lambda_init):
    B, S, _ = dk.shape
    tq = KEY_TILE
    n_pairs = DIFF_HEADS // 2
    full = lambda a: pl.BlockSpec(a.shape, lambda b, p, i: (0,) * a.ndim)
    kern = functools.partial(_diff_attn_kernel, lambda_init=lambda_init, tq=tq)
    return pl.pallas_call(
        kern,
        grid=(B, n_pairs, S // tq),
        in_specs=[pl.BlockSpec((1, LANES, tq), lambda b, p, i: (b, p, i)),
                  pl.BlockSpec((1, S, LANES), lambda b, p, i: (b, 0, p)),
                  pl.BlockSpec((1, S // KEY_TILE, LANES, KEY_TILE), lambda b, p, i: (b, 0, p, 0)),
                  pl.BlockSpec((1, 1, LANES), lambda b, p, i: (p, 0, 0)),
                  full(lq1), full(lk1), full(lq2), full(lk2), full(subg)],
        out_specs=pl.BlockSpec((1, tq, LANES), lambda b, p, i: (b, i, p)),
        out_shape=jax.ShapeDtypeStruct((B, S, DIFF_WIDTH), BF16),
        scratch_shapes=[pltpu.VMEM((4, 1, tq), F32), pltpu.VMEM((4, 1, tq), F32), pltpu.VMEM((4, LANES, tq), F32),
                        pltpu.VMEM((2, 3, KEY_TILE, tq), F32)],
        compiler_params=_cparams(("parallel", "parallel", "arbitrary")),
        name="diff_attn",
    )(dqt, dk, dvt, slopes, lq1, lk1, lq2, lk2, subg)


def _mla_attn_kernel(qt_ref, k_ref, vt_ref, o_ref, m_scr, l_scr, acc_scr, bias_scr, *, tq):
    qt = qt_ref[0]
    qt_list = [qt[hh * MLA_HEAD_PAD:(hh + 1) * MLA_HEAD_PAD, :] for hh in range(2)]
    k_lanes = [slice(hh * MLA_HEAD_PAD, (hh + 1) * MLA_HEAD_PAD) for hh in range(2)]

    @pl.when(pl.program_id(2) == 0)
    def _():
        _, _, allowed = _chunk_allowed_t(tq)
        bias_scr[0] = jnp.zeros((KEY_TILE, tq), F32)
        bias_scr[1] = jnp.where(allowed, 0.0, NEG_BIG)
        bias_scr[2] = jnp.full((KEY_TILE, tq), NEG_BIG, F32)

    _flash_maps(qt_list, k_ref, vt_ref, k_lanes, lambda i, kind: bias_scr[kind], lambda i, n: 0.0,
                m_scr, l_scr, acc_scr, MLA_TILES_PER_STEP)
    orow = lax.broadcasted_iota(jnp.int32, (LANES, tq), 0)
    out_t = jnp.where(orow < MLA_V_DIM, acc_scr[0] / l_scr[0], acc_scr[1] / l_scr[1])
    o_ref[0] = out_t.T.astype(o_ref.dtype)


def _mla_attn(mqt, mk, mvt):
    B, S, _ = mk.shape
    tq = KEY_TILE
    n_pairs = MLA_HEADS // 2
    kern = functools.partial(_mla_attn_kernel, tq=tq)
    return pl.pallas_call(
        kern,
        grid=(B, n_pairs, S // tq),
        in_specs=[pl.BlockSpec((1, 2 * MLA_HEAD_PAD, tq), lambda b, p, i: (b, p, i)),
                  pl.BlockSpec((1, S, 2 * MLA_HEAD_PAD), lambda b, p, i: (b, 0, p)),
                  pl.BlockSpec((1, S // KEY_TILE, LANES, KEY_TILE), lambda b, p, i: (b, 0, p, 0))],
        out_specs=pl.BlockSpec((1, tq, LANES), lambda b, p, i: (b, i, p)),
        out_shape=jax.ShapeDtypeStruct((B, S, MLA_WIDTH), F32),
        scratch_shapes=[pltpu.VMEM((2, 1, tq), F32), pltpu.VMEM((2, 1, tq), F32), pltpu.VMEM((2, LANES, tq), F32),
                        pltpu.VMEM((3, KEY_TILE, tq), F32)],
        compiler_params=_cparams(("parallel", "parallel", "arbitrary")),
        name="mla_attn",
    )(mqt, mk, mvt)


ROUTER_PAD = LANES


def _out_proj_kernel(x_ref, ylru_ref, ydiff_ref, ymla_ref, mlag_ref, wo_ref, g2_ref, wrh_ref, wrl_ref, rb_ref,
                     x1_ref, h2_ref, eid_ref, gate_ref, cnt_ref):
    @pl.when(pl.program_id(0) == 0)
    def _():
        cnt_ref[...] = jnp.zeros(cnt_ref.shape, F32)

    ymla = (_rms(ymla_ref[...], MLA_WIDTH) * mlag_ref[...]).astype(BF16)
    x1 = (x_ref[...]
          + _dot(ylru_ref[...], wo_ref[0:LRU_WIDTH, :])
          + _dot(ydiff_ref[...], wo_ref[LRU_WIDTH:LRU_WIDTH + DIFF_WIDTH, :])
          + _dot(ymla, wo_ref[LRU_WIDTH + DIFF_WIDTH:, :]))
    x1_ref[...] = x1

    h2 = _rms(x1, D_MODEL) * g2_ref[...]
    hi, lo = _split_bf16(h2)
    h2_ref[...] = hi
    wrh = wrh_ref[...]
    logits = _dot(hi, wrh) + _dot(hi, wrl_ref[...]) + _dot(lo, wrh) + rb_ref[...]

    lane = lax.broadcasted_iota(jnp.int32, logits.shape, 1)
    big = jnp.int32(1 << 20)
    is_group = lane < N_GROUPS
    gl = jnp.where(is_group, logits, NEG_BIG)
    gmax = jnp.max(gl, axis=-1, keepdims=True)
    g_idx = jnp.min(jnp.where(gl == gmax, lane, big), axis=-1, keepdims=True)
    pg_top = 1.0 / jnp.sum(jnp.where(is_group, jnp.exp(gl - gmax), 0.0), axis=-1, keepdims=True)

    e_lane = lane - N_GROUPS
    in_group = (e_lane >= 0) & (e_lane < N_EXPERTS) & ((e_lane // EXPERTS_PER_GROUP) == g_idx)
    el = jnp.where(in_group, logits, NEG_BIG)
    m1 = jnp.max(el, axis=-1, keepdims=True)
    i1 = jnp.min(jnp.where(el == m1, lane, big), axis=-1, keepdims=True)
    el2 = jnp.where(lane == i1, NEG_BIG, el)
    m2 = jnp.max(el2, axis=-1, keepdims=True)
    i2 = jnp.min(jnp.where(el2 == m2, lane, big), axis=-1, keepdims=True)
    e2 = jnp.exp(m2 - m1)
    g1 = pg_top / (1.0 + e2)
    g2 = pg_top * e2 / (1.0 + e2)

    two = lax.broadcasted_iota(jnp.int32, eid_ref.shape, 1)
    eid_ref[...] = jnp.where(two == 0, i1 - N_GROUPS, i2 - N_GROUPS)
    gate_ref[...] = jnp.where(two == 0, g1, g2)
    onehot = ((e_lane == i1 - N_GROUPS) | (e_lane == i2 - N_GROUPS)).astype(F32)
    cnt_ref[...] += jnp.sum(onehot, axis=0, keepdims=True)


def _out_proj(x, ylru, ydiff, ymla, mlag, wo, g2, wrh, wrl, rb, tm):
    N = x.shape[0]
    full = lambda a: pl.BlockSpec(a.shape, lambda i: (0,) * a.ndim)
    tok = lambda width: pl.BlockSpec((tm, width), lambda i: (i, 0))
    return pl.pallas_call(
        _out_proj_kernel,
        grid=(N // tm,),
        in_specs=[tok(D_MODEL), tok(LRU_WIDTH), tok(DIFF_WIDTH), tok(MLA_WIDTH), full(mlag), full(wo), full(g2),
                  full(wrh), full(wrl), full(rb)],
        out_specs=[tok(D_MODEL), tok(D_MODEL), tok(2), tok(2), pl.BlockSpec((1, ROUTER_PAD), lambda i: (0, 0))],
        out_shape=[jax.ShapeDtypeStruct((N, D_MODEL), F32), jax.ShapeDtypeStruct((N, D_MODEL), BF16),
                   jax.ShapeDtypeStruct((N, 2), jnp.int32),
                   jax.ShapeDtypeStruct((N, 2), F32), jax.ShapeDtypeStruct((1, ROUTER_PAD), F32)],
        compiler_params=_cparams(("arbitrary",)),
        name="out_proj",
    )(x, ylru, ydiff, ymla, mlag, wo, g2, wrh, wrl, rb)


TOK_TILE = 256
RUN_CHUNK = 16
TAB_ROWS = 8


def _rank_kernel(eid_ref, start_ref, loc_ref, tab_ref, carry):
    tm = eid_ref.shape[0]

    @pl.when(pl.program_id(0) == 0)
    def _():
        carry[...] = start_ref[...]

    lane = lax.broadcasted_iota(jnp.int32, (tm, LANES), 1)
    r = lax.broadcasted_iota(jnp.int32, (tm, tm), 0)
    c = lax.broadcasted_iota(jnp.int32, (tm, tm), 1)
    lower = (c < r).astype(BF16)
    eid = eid_ref[...]
    onehot = [lane == eid[:, k:k + 1] for k in range(2)]
    cnt = [jnp.sum(oh.astype(F32), axis=0, keepdims=True) for oh in onehot]
    cnt_tile = cnt[0] + cnt[1]

    chunks = jnp.floor((cnt_tile + (RUN_CHUNK - 1)) * (1.0 / RUN_CHUNK))
    lane8 = lax.broadcasted_iota(jnp.int32, (TAB_ROWS, LANES), 1)
    incl = jnp.broadcast_to(chunks, (TAB_ROWS, LANES))
    shift = 1
    while shift < LANES:
        incl = incl + jnp.where(lane8 >= shift, pltpu.roll(incl, shift, 1), 0.0)
        shift *= 2
    offs = (incl[0:1] - chunks) * RUN_CHUNK

    locs = []
    for k in range(2):
        before = _dot(lower, onehot[k].astype(BF16))
        first = offs if k == 0 else offs + cnt[0]
        locs.append(jnp.sum(jnp.where(onehot[k], before + first, 0.0), axis=-1, keepdims=True))
    two = lax.broadcasted_iota(jnp.int32, loc_ref.shape, 1)
    loc_ref[...] = jnp.where(two == 0, locs[0], locs[1]).astype(jnp.int32)

    row8 = lax.broadcasted_iota(jnp.int32, (TAB_ROWS, LANES), 0)
    base = carry[...]
    tab = jnp.where(row8 == 0, cnt_tile, jnp.where(row8 == 1, offs, jnp.where(row8 == 2, base, 0.0)))
    tab_ref[...] = tab.astype(jnp.int32)
    carry[...] = base + cnt_tile


def _rank(eid, starts):
    N = eid.shape[0]
    n_tiles = N // TOK_TILE
    return pl.pallas_call(
        _rank_kernel,
        grid=(n_tiles,),
        in_specs=[pl.BlockSpec((TOK_TILE, 2), lambda i: (i, 0)), pl.BlockSpec((1, LANES), lambda i: (0, 0))],
        out_specs=[pl.BlockSpec((TOK_TILE, 2), lambda i: (i, 0)), pl.BlockSpec((TAB_ROWS, LANES), lambda i: (i, 0))],
        out_shape=[jax.ShapeDtypeStruct((N, 2), jnp.int32),
                   jax.ShapeDtypeStruct((n_tiles * TAB_ROWS, LANES), jnp.int32)],
        scratch_shapes=[pltpu.VMEM((1, LANES), F32)],
        compiler_params=_cparams(("arbitrary",)),
        name="rank",
    )(eid, starts)


SORT_ROWS = 2 * TOK_TILE + N_EXPERTS * RUN_CHUNK


def _run_chunks(tab_ref, tile, e):
    cnt = tab_ref[(tile * 3 + 0) * N_EXPERTS + e]
    offs = tab_ref[(tile * 3 + 1) * N_EXPERTS + e]
    base = tab_ref[(tile * 3 + 2) * N_EXPERTS + e]
    return lax.shift_right_logical(cnt + (RUN_CHUNK - 1), RUN_CHUNK.bit_length() - 1), offs, base


def _rows(ref, row, n):
    return ref.at[pl.ds(pl.multiple_of(row * ROW_TILES, ROW_TILES), n * ROW_TILES)]


def _for_each_chunk(tab_ref, tile, fn):
    for e in range(N_EXPERTS):
        chunks, offs, base = _run_chunks(tab_ref, tile, e)

        def one(k, carry):
            fn(offs + k * RUN_CHUNK, base + k * RUN_CHUNK)
            return carry

        lax.fori_loop(0, chunks, one, 0)


def _drain_chunks(tab_ref, tile, wait_one):
    total = 0
    for e in range(N_EXPERTS):
        total = total + _run_chunks(tab_ref, tile, e)[0]

    def one(k, carry):
        wait_one()
        return carry

    lax.fori_loop(0, total, one, 0)


def _dispatch_kernel(tab_ref, ztab_ref, h_ref, loc_ref, rows_hbm, sbuf, zbuf, sems, zsem):
    i = pl.program_id(0)
    n_tiles = pl.num_programs(0)
    slot = lax.rem(i, 2)
    T = TOK_TILE

    def run_copy(s, offs_row, base_row):
        return pltpu.make_async_copy(_rows(sbuf, s * SORT_ROWS + offs_row, RUN_CHUNK),
                                     _rows(rows_hbm, base_row, RUN_CHUNK), sems.at[s])

    @pl.when(i == 0)
    def _():
        zbuf[...] = jnp.zeros(zbuf.shape, F32)

        def zero_chunk(first_row):
            return pltpu.make_async_copy(zbuf, _rows(rows_hbm, first_row, RUN_CHUNK), zsem)

        def done(k, carry):
            zero_chunk(0).wait()
            return carry

        for e in range(N_EXPERTS):
            zero_chunk(ztab_ref[e]).start()
        lax.fori_loop(0, N_EXPERTS, done, 0)
        total = 0
        for e in range(N_EXPERTS):
            start, chunks = ztab_ref[N_EXPERTS + e], ztab_ref[2 * N_EXPERTS + e]
            total = total + chunks

            def fill(k, carry, start=start):
                zero_chunk(start + k * RUN_CHUNK).start()
                return carry

            lax.fori_loop(0, chunks, fill, 0)
        lax.fori_loop(0, total, done, 0)

    @pl.when(i > 0)
    def _():
        _drain_chunks(tab_ref, i - 1, lambda: run_copy(1 - slot, 0, 0).wait())

    locf = loc_ref[...].astype(F32)
    loc_lanes = [jnp.broadcast_to(locf[:, k:k + 1], (T, LANES)).T[0:1, :] for k in range(2)]
    pos = lax.broadcasted_iota(jnp.int32, (SORT_ROWS, T), 0).astype(F32)
    perm = jnp.where((pos == loc_lanes[0]) | (pos == loc_lanes[1]), 1.0, 0.0).astype(BF16)
    sorted_rows = _dot(perm, h_ref[...])
    for c in range(ROW_TILES):
        sbuf[pl.ds(slot * (SORT_ROWS * ROW_TILES) + c, SORT_ROWS, stride=ROW_TILES), :] = (
            sorted_rows[:, c * LANES:(c + 1) * LANES])

    _for_each_chunk(tab_ref, i, lambda offs_row, base_row: run_copy(slot, offs_row, base_row).start())

    @pl.when(i == n_tiles - 1)
    def _():
        _drain_chunks(tab_ref, i, lambda: run_copy(slot, 0, 0).wait())


def _dispatch(tabs, ztab, h2, loc, n_rows):
    N = h2.shape[0]
    return pl.pallas_call(
        _dispatch_kernel,
        grid_spec=pltpu.PrefetchScalarGridSpec(
            num_scalar_prefetch=2,
            grid=(N // TOK_TILE,),
            in_specs=[pl.BlockSpec((TOK_TILE, D_MODEL), lambda i, t, z: (i, 0)),
                      pl.BlockSpec((TOK_TILE, 2), lambda i, t, z: (i, 0))],
            out_specs=pl.BlockSpec(memory_space=pl.ANY),
            scratch_shapes=[pltpu.VMEM((2 * SORT_ROWS * ROW_TILES, LANES), F32),
                            pltpu.VMEM((RUN_CHUNK * ROW_TILES, LANES), F32),
                            pltpu.SemaphoreType.DMA((2,)), pltpu.SemaphoreType.DMA(())],
        ),
        out_shape=jax.ShapeDtypeStruct((n_rows * ROW_TILES, LANES), F32),
        compiler_params=_cparams(("arbitrary",)),
        name="dispatch",
    )(tabs, ztab, h2, loc)


def _expert_kernel(be_ref, nused_ref, rows_ref, w1_ref, w3_ref, w2_ref, y_ref):
    del be_ref

    @pl.when(pl.program_id(0) < nused_ref[0])
    def _():
        h = jnp.concatenate([rows_ref[pl.ds(c, ROW_BLOCK, stride=ROW_TILES), :] for c in range(ROW_TILES)],
                            axis=1).astype(BF16)
        a = _dot(h, w1_ref[0].astype(BF16))
        b = _dot(h, w3_ref[0].astype(BF16))
        z = (a * jax.nn.sigmoid(a) * b).astype(BF16)
        y = _dot(z, w2_ref[0].astype(BF16))
        for c in range(ROW_TILES):
            y_ref[pl.ds(c, ROW_BLOCK, stride=ROW_TILES), :] = y[:, c * LANES:(c + 1) * LANES]

    @pl.when(pl.program_id(0) >= nused_ref[0])
    def _():
        y_ref[...] = jnp.zeros(y_ref.shape, y_ref.dtype)


def _experts(blk_expert, n_used, rows, w1, w3, w2):
    n_blk = blk_expert.shape[0]
    row_blk = lambda i, be, nu: (jnp.minimum(i, nu[0] - 1), 0)
    wspec = lambda shape: pl.BlockSpec((1,) + shape, lambda i, be, nu: (be[i], 0, 0))
    return pl.pallas_call(
        _expert_kernel,
        grid_spec=pltpu.PrefetchScalarGridSpec(
            num_scalar_prefetch=2,
            grid=(n_blk,),
            in_specs=[pl.BlockSpec((ROW_BLOCK * ROW_TILES, LANES), row_blk),
                      wspec((D_MODEL, D_EXPERT)), wspec((D_MODEL, D_EXPERT)), wspec((D_EXPERT, D_MODEL))],
            out_specs=pl.BlockSpec((ROW_BLOCK * ROW_TILES, LANES), lambda i, be, nu: (i, 0)),
        ),
        out_shape=jax.ShapeDtypeStruct((n_blk * ROW_BLOCK * ROW_TILES, LANES), F32),
        compiler_params=_cparams(("arbitrary",)),
        name="experts",
    )(blk_expert, n_used, rows, w1, w3, w2)


def _combine_kernel(tab_ref, x1_ref, gate_ref, loc_ref, y_hbm, o_ref, ybuf, sems):
    i = pl.program_id(0)
    n_tiles = pl.num_programs(0)
    slot = lax.rem(i, 2)
    T = TOK_TILE

    def run_copy(s, offs_row, base_row):
        return pltpu.make_async_copy(_rows(y_hbm, base_row, RUN_CHUNK),
                                     _rows(ybuf, s * SORT_ROWS + offs_row, RUN_CHUNK), sems.at[s])

    def fetch(tile, s):
        _for_each_chunk(tab_ref, tile, lambda offs_row, base_row: run_copy(s, offs_row, base_row).start())

    @pl.when(i == 0)
    def _():
        ybuf[...] = jnp.zeros(ybuf.shape, F32)
        fetch(0, 0)

    @pl.when(i + 1 < n_tiles)
    def _():
        fetch(i + 1, 1 - slot)

    _drain_chunks(tab_ref, i, lambda: run_copy(slot, 0, 0).wait())

    y_sorted = jnp.concatenate([ybuf[pl.ds(slot * (SORT_ROWS * ROW_TILES) + c, SORT_ROWS, stride=ROW_TILES), :]
                                for c in range(ROW_TILES)], axis=1).astype(BF16)
    gate = gate_ref[...]
    loc = loc_ref[...]
    pos = lax.broadcasted_iota(jnp.int32, (T, SORT_ROWS), 1)
    g = (jnp.where(pos == loc[:, 0:1], gate[:, 0:1], 0.0) + jnp.where(pos == loc[:, 1:2], gate[:, 1:2], 0.0))
    g_hi, g_lo = _split_bf16(g)
    o_ref[...] = x1_ref[...] + _dot(g_hi, y_sorted) + _dot(g_lo, y_sorted)


def _combine(tabs, x1, gate, loc, y_rows):
    N = gate.shape[0]
    tok = lambda width: pl.BlockSpec((TOK_TILE, width), lambda i, t: (i, 0))
    return pl.pallas_call(
        _combine_kernel,
        grid_spec=pltpu.PrefetchScalarGridSpec(
            num_scalar_prefetch=1,
            grid=(N // TOK_TILE,),
            in_specs=[tok(D_MODEL), tok(2), tok(2), pl.BlockSpec(memory_space=pl.ANY)],
            out_specs=tok(D_MODEL),
            scratch_shapes=[pltpu.VMEM((2 * SORT_ROWS * ROW_TILES, LANES), F32), pltpu.SemaphoreType.DMA((2,))],
        ),
        out_shape=jax.ShapeDtypeStruct((N, D_MODEL), F32),
        compiler_params=_cparams(("arbitrary",)),
        name="combine",
    )(tabs, x1, gate, loc, y_rows)


def _block_diag(w):
    n, a, b = w.shape
    eye = np.eye(n, dtype=np.float32)
    return (w[:, :, None, :] * eye[:, None, :, None]).reshape(n * a, n * b)


def _pad_heads(v, used):
    lead = v.shape[:-1]
    v = v.reshape(lead + (MLA_HEADS, used))
    v = jnp.pad(v, [(0, 0)] * len(lead) + [(0, 0), (0, MLA_HEAD_PAD - used)])
    return v.reshape(lead + (MLA_QK_PAD,))


def _rotary_lane_tables(seq):
    half = MLA_ROPE_DIM // 2
    inv_freq = ROPE_THETA ** (-jnp.arange(half, dtype=F32) / half)
    ang = jnp.arange(seq, dtype=F32)[:, None] * inv_freq[None, :]
    cos, sin = jnp.cos(ang), jnp.sin(ang)
    ones = jnp.ones((seq, MLA_NOPE_DIM), F32)
    tail = MLA_HEAD_PAD - MLA_QK_DIM
    cos_t = jnp.concatenate([ones, cos, cos, jnp.ones((seq, tail), F32)], axis=1)
    sin_t = jnp.concatenate([0.0 * ones, -sin, sin, jnp.zeros((seq, tail), F32)], axis=1)
    return cos_t, sin_t


def _pick_tile(n, pref):
    t = min(n, pref)
    while n % t:
        t //= 2
    return t


def kernel(x, norm1_g, w_in, lru_conv_w, lru_conv_b, lru_wa, lru_ba, lru_wx, lru_bx, lru_lambda, lru_out_g,
           diff_q_g, diff_k_g, diff_lq1, diff_lk1, diff_lq2, diff_lk2, diff_sub_g,
           mla_cq_g, mla_ckv_g, mla_w_uq, mla_w_ukv, mla_q_g, mla_k_g, mla_out_g, w_out, norm2_g,
           router_g_w, router_g_b, router_e_w, router_e_b, exp_w1, exp_w3, exp_w2):
    B, S, D = x.shape
    N = B * S
    depth = w_in.shape[0]
    assert S % KEY_TILE == 0 and D == D_MODEL
    ts = _pick_tile(S, 512)
    tc = _pick_tile(S, 512)
    assert N % TOK_TILE == 0
    tm = _pick_tile(N, 512)

    cos_t, sin_t = _rotary_lane_tables(S)
    n_dgroups = DIFF_QK_WIDTH // DIFF_HEAD_DIM
    gsum = jnp.asarray(np.kron(np.eye(n_dgroups), np.ones((DIFF_HEAD_DIM, DIFF_HEAD_DIM))), BF16)
    slopes = np.asarray([2.0 ** (-ALIBI_MAX_BIAS * (h + 1) / DIFF_HEADS) * LOG2E for h in range(DIFF_HEADS)], np.float32)
    slopes = jnp.asarray(np.pad(slopes.reshape(DIFF_HEADS // 2, 1, 2), ((0, 0), (0, 0), (0, LANES - 2))))
    place = np.zeros((MLA_ROPE_DIM, MLA_HEADS, MLA_HEAD_PAD), np.float32)
    place[np.arange(MLA_ROPE_DIM), :, MLA_NOPE_DIM + np.arange(MLA_ROPE_DIM)] = 1.0
    place = jnp.asarray(place.reshape(MLA_ROPE_DIM, MLA_QK_PAD))
    row = lambda v: v.reshape(1, -1).astype(F32)

    n_blk = -(-(2 * N + N_EXPERTS * (RUN_CHUNK + ROW_BLOCK)) // ROW_BLOCK) + 1
    n_rows = n_blk * ROW_BLOCK

    for l in range(depth):
        dqg = row(jnp.tile(diff_q_g[l], n_dgroups) * (DIFF_HEAD_DIM ** -0.5 * LOG2E))
        dkg = row(jnp.tile(diff_k_g[l], n_dgroups))
        wuq = _pad_heads(mla_w_uq[l], MLA_QK_DIM).astype(BF16)
        mqg = row(_pad_heads(jnp.tile(mla_q_g[l], MLA_HEADS) * (MLA_QK_DIM ** -0.5 * LOG2E), MLA_QK_DIM))
        mkg = row(_pad_heads(jnp.tile(mla_k_g[l], MLA_HEADS), MLA_QK_DIM))
        wukv = mla_w_ukv[l].reshape(MLA_KV_RANK, MLA_HEADS, MLA_NOPE_DIM + MLA_V_DIM)
        wk_nope = _pad_heads(wukv[:, :, :MLA_NOPE_DIM].reshape(MLA_KV_RANK, -1), MLA_NOPE_DIM)
        wk = jnp.concatenate([wk_nope, place], axis=0).astype(BF16)
        wv = wukv[:, :, MLA_NOPE_DIM:].reshape(MLA_KV_RANK, MLA_WIDTH).astype(BF16)

        lru_in, dq, dk, dv, mq, mk, mv = _in_proj(
            x, row(norm1_g[l]), w_in[l], gsum, dqg, dkg, row(mla_cq_g[l]), wuq, mqg, row(mla_ckv_g[l]), wk, mkg, wv,
            cos_t, sin_t, ts)

        y_lru = _lru(lru_in, lru_conv_w[l], row(lru_conv_b[l]), _block_diag(lru_wa[l]).astype(BF16),
                     row(lru_ba[l]), _block_diag(lru_wx[l]).astype(BF16), row(lru_bx[l]), row(lru_lambda[l]),
                     row(lru_out_g[l]), tc)

        lambda_init = 0.8 - 0.6 * math.exp(-0.3 * l)
        y_diff = _diff_attn(dq, dk, dv, slopes, row(diff_lq1[l]), row(diff_lk1[l]), row(diff_lq2[l]),
                            row(diff_lk2[l]), row(jnp.tile(diff_sub_g[l], 2)), lambda_init)
        y_mla = _mla_attn(mq, mk, mv)

        wr = jnp.concatenate([router_g_w[l], router_e_w[l],
                              jnp.zeros((D, ROUTER_PAD - N_GROUPS - N_EXPERTS), F32)], axis=1)
        wrh = wr.astype(BF16)
        wrl = (wr - wrh.astype(F32)).astype(BF16)
        rb = jnp.pad(jnp.concatenate([router_g_b[l], router_e_b[l]]), (0, ROUTER_PAD - N_GROUPS - N_EXPERTS))
        x1, h2, eid, gate, cnt = _out_proj(
            x.reshape(N, D), y_lru.reshape(N, LRU_WIDTH), y_diff.reshape(N, DIFF_WIDTH), y_mla.reshape(N, MLA_WIDTH),
            row(mla_out_g[l]), w_out[l].astype(BF16), row(norm2_g[l]), wrh, wrl, row(rb), tm)

        counts = cnt[0, N_GROUPS:N_GROUPS + N_EXPERTS].astype(jnp.int32)
        padded = (counts + RUN_CHUNK + ROW_BLOCK - 1) // ROW_BLOCK * ROW_BLOCK
        pad_ends = jnp.cumsum(padded)
        pad_starts = pad_ends - padded
        blk_row0 = jnp.arange(n_blk, dtype=jnp.int32) * ROW_BLOCK
        blk_expert = jnp.minimum(jnp.sum((pad_ends[None, :] <= blk_row0[:, None]).astype(jnp.int32), axis=1),
                                 N_EXPERTS - 1)
        n_used = (pad_ends[-1:] // ROW_BLOCK).astype(jnp.int32)
        starts = jnp.pad(pad_starts.astype(F32), (0, LANES - N_EXPERTS)).reshape(1, LANES)
        tail_start = pad_starts + counts
        last = jnp.arange(N_EXPERTS) == N_EXPERTS - 1
        tail_ends = jnp.where(last, n_rows, pad_ends)
        aligned_start = (tail_start + RUN_CHUNK - 1) // RUN_CHUNK * RUN_CHUNK
        ztab = jnp.concatenate([tail_start, aligned_start, (tail_ends - aligned_start) // RUN_CHUNK]).astype(jnp.int32)

        loc, tab = _rank(eid, starts)
        tabs = tab.reshape(N // TOK_TILE, TAB_ROWS, LANES)[:, :3, :N_EXPERTS].reshape(-1)
        rows = _dispatch(tabs, ztab, h2, loc, n_rows)
        y_rows = _experts(blk_expert, n_used, rows, exp_w1[l], exp_w3[l], exp_w2[l])
        x = _combine(tabs, x1, gate, loc, y_rows).reshape(B, S, D)
    return x
```

```python
import functools
import math

import jax
import jax.numpy as jnp
import numpy as np
from jax import lax
from jax.experimental import pallas as pl
from jax.experimental.pallas import tpu as pltpu

F32 = jnp.float32
BF16 = jnp.bfloat16

D_MODEL = 1024
CHUNK = 64
LRU_WIDTH = 256
CONV_WIDTH = 4
RG_C = 8.0
DIFF_HEADS = 6
DIFF_HEAD_DIM = 32
DIFF_V_DIM = 64
DIFF_QK_WIDTH = DIFF_HEADS * 2 * DIFF_HEAD_DIM
DIFF_WIDTH = DIFF_HEADS * DIFF_V_DIM
ALIBI_MAX_BIAS = 8.0
MLA_HEADS = 6
MLA_Q_RANK = 192
MLA_KV_RANK = 128
MLA_NOPE_DIM = 64
MLA_ROPE_DIM = 32
MLA_V_DIM = 64
MLA_QK_DIM = MLA_NOPE_DIM + MLA_ROPE_DIM
MLA_WIDTH = MLA_HEADS * MLA_V_DIM
ROPE_THETA = 10000.0
N_GROUPS = 4
EXPERTS_PER_GROUP = 8
N_EXPERTS = N_GROUPS * EXPERTS_PER_GROUP
D_EXPERT = 256
ROW_BLOCK = 256
EPS = 1e-6

LANES = 128
ROW_TILES = D_MODEL // LANES
MLA_HEAD_PAD = LANES
MLA_QK_PAD = MLA_HEADS * MLA_HEAD_PAD
VMEM_LIMIT = 56 * 1024 * 1024
NEG_BIG = -1e30
KEY_TILE = 256
TILES_PER_GROUP = 4
LOG2E = math.log2(math.e)

C_LRU = 0
C_DQ = 512
C_DK = 896
C_DV = 1280
C_CQ = 1664
C_CKV = 1856
C_KR = 1984
D_IN = 2016
D_IN_PAD = 2048


def _cparams(sem):
    return pltpu.CompilerParams(dimension_semantics=sem, vmem_limit_bytes=VMEM_LIMIT)


def _rms(v, width):
    return v * lax.rsqrt(jnp.sum(v * v, axis=-1, keepdims=True) * (1.0 / width) + EPS)


def _dot(a, b):
    return jnp.dot(a, b, preferred_element_type=F32)


def _split_bf16(v):
    hi = v.astype(BF16)
    lo = (v - hi.astype(F32)).astype(BF16)
    return hi, lo


def _in_proj_kernel(x_ref, g1_ref, w_ref, gsum_ref, dqg_ref, dkg_ref, cqg_ref, wuq_ref, mqg_ref, ckvg_ref,
                    wk_ref, mkg_ref, wv_ref, cos_ref, sin_ref,
                    lru_ref, dqt_ref, dk_ref, dvt_ref, mqt_ref, mk_ref, mvt_ref, w_bf):
    @pl.when((pl.program_id(0) == 0) & (pl.program_id(1) == 0))
    def _():
        w_bf[:, 0:D_IN] = w_ref[0].astype(BF16)
        w_bf[:, D_IN:] = jnp.zeros((D_MODEL, D_IN_PAD - D_IN), BF16)

    x = x_ref[0]
    hn = (_rms(x, D_MODEL) * g1_ref[...]).astype(BF16)
    p = _dot(hn, w_bf[...])
    lru_ref[0] = p[:, C_LRU:C_LRU + 2 * LRU_WIDTH]

    gsum = gsum_ref[...]

    def group_norm32(v, gain):
        hi, lo = _split_bf16(v * v)
        ss = _dot(hi, gsum) + _dot(lo, gsum)
        return v * lax.rsqrt(ss * (1.0 / DIFF_HEAD_DIM) + EPS) * gain

    def store_tiles(ref, vt):
        for c in range(vt.shape[1] // KEY_TILE):
            ref[0, c] = vt[:, c * KEY_TILE:(c + 1) * KEY_TILE].astype(BF16)

    dqt_ref[0] = group_norm32(p[:, C_DQ:C_DQ + DIFF_QK_WIDTH], dqg_ref[...]).T.astype(BF16)
    dk_ref[0] = group_norm32(p[:, C_DK:C_DK + DIFF_QK_WIDTH], dkg_ref[...]).astype(BF16)
    store_tiles(dvt_ref, p[:, C_DV:C_DV + DIFF_WIDTH].T)

    cos = cos_ref[...]
    sin = sin_ref[...]
    lane = lax.broadcasted_iota(jnp.int32, cos.shape, 1)
    first_half = lane < MLA_NOPE_DIM + MLA_ROPE_DIM // 2

    def head_norm_rotary(v, gain):
        outs = []
        for h in range(MLA_HEADS):
            c = v[:, h * MLA_HEAD_PAD:(h + 1) * MLA_HEAD_PAD]
            c = _rms(c, MLA_QK_DIM) * gain[:, h * MLA_HEAD_PAD:(h + 1) * MLA_HEAD_PAD]
            swapped = jnp.where(first_half, pltpu.roll(c, MLA_HEAD_PAD - MLA_ROPE_DIM // 2, 1),
                                pltpu.roll(c, MLA_ROPE_DIM // 2, 1))
            outs.append(c * cos + swapped * sin)
        return jnp.concatenate(outs, axis=1)

    cq = (_rms(p[:, C_CQ:C_CQ + MLA_Q_RANK], MLA_Q_RANK) * cqg_ref[...]).astype(BF16)
    mqt_ref[0] = head_norm_rotary(_dot(cq, wuq_ref[...]), mqg_ref[...]).T.astype(BF16)

    ckv = (_rms(p[:, C_CKV:C_CKV + MLA_KV_RANK], MLA_KV_RANK) * ckvg_ref[...]).astype(BF16)
    kcat = jnp.concatenate([ckv, p[:, C_KR:C_KR + MLA_ROPE_DIM].astype(BF16)], axis=1)
    mk_ref[0] = head_norm_rotary(_dot(kcat, wk_ref[...]), mkg_ref[...]).astype(BF16)
    store_tiles(mvt_ref, _dot(ckv, wv_ref[...]).T)


def _in_proj(x, g1, w_all, layer, gsum, dqg, dkg, cqg, wuq, mqg, ckvg, wk, mkg, wv, cos_t, sin_t, ts):
    B, S, _ = x.shape
    full = lambda a: pl.BlockSpec(a.shape, lambda b, i: (0,) * a.ndim)
    w_spec = pl.BlockSpec((1,) + w_all.shape[1:], lambda b, i: (layer, 0, 0))
    tok = lambda width: pl.BlockSpec((1, ts, width), lambda b, i: (b, i, 0))
    tok_t = lambda width: pl.BlockSpec((1, width, ts), lambda b, i: (b, 0, i))
    nkt = ts // KEY_TILE
    tiles_t = lambda width: pl.BlockSpec((1, nkt, width, KEY_TILE), lambda b, i: (b, i, 0, 0))
    rot = pl.BlockSpec((ts, LANES), lambda b, i: (i, 0))
    shp = lambda shape, dt: jax.ShapeDtypeStruct(shape, dt)
    return pl.pallas_call(
        _in_proj_kernel,
        grid=(B, S // ts),
        in_specs=[tok(D_MODEL), full(g1), w_spec, full(gsum), full(dqg), full(dkg), full(cqg), full(wuq),
                  full(mqg), full(ckvg), full(wk), full(mkg), full(wv), rot, rot],
        out_specs=[tok(2 * LRU_WIDTH), tok_t(DIFF_QK_WIDTH), tok(DIFF_QK_WIDTH), tiles_t(DIFF_WIDTH),
                   tok_t(MLA_QK_PAD), tok(MLA_QK_PAD), tiles_t(MLA_WIDTH)],
        out_shape=[shp((B, S, 2 * LRU_WIDTH), F32), shp((B, DIFF_QK_WIDTH, S), BF16), shp((B, S, DIFF_QK_WIDTH), BF16),
                   shp((B, S // KEY_TILE, DIFF_WIDTH, KEY_TILE), BF16), shp((B, MLA_QK_PAD, S), BF16),
                   shp((B, S, MLA_QK_PAD), BF16), shp((B, S // KEY_TILE, MLA_WIDTH, KEY_TILE), BF16)],
        scratch_shapes=[pltpu.VMEM((D_MODEL, D_IN_PAD), BF16)],
        compiler_params=_cparams(("arbitrary", "arbitrary")),
        name="in_proj",
    )(x, g1, w_all, gsum, dqg, dkg, cqg, wuq, mqg, ckvg, wk, mkg, wv, cos_t, sin_t)


LRU_HALO = 8


def _lru_kernel(blk_ref, cw_ref, cb_ref, wa_ref, ba_ref, wx_ref, bx_ref, lam_ref, og_ref, y_ref,
                ubuf, a_scr, b_scr, h_scr):
    B, tc, _ = blk_ref.shape
    W = LRU_WIDTH

    @pl.when(pl.program_id(0) == 0)
    def _():
        ubuf[:, 0:LRU_HALO, :] = jnp.zeros((B, LRU_HALO, W), F32)
        h_scr[...] = jnp.zeros(h_scr.shape, F32)

    ubuf[:, LRU_HALO:, :] = blk_ref[:, :, 0:W]
    xc = cb_ref[...][None]
    for j in range(CONV_WIDTH):
        off = LRU_HALO - (CONV_WIDTH - 1) + j
        xc = xc + cw_ref[j:j + 1, :][None] * ubuf[:, off:off + tc, :]
    ubuf[:, 0:LRU_HALO, :] = ubuf[:, tc:tc + LRU_HALO, :]

    x2 = xc.reshape(B * tc, W)
    xb = x2.astype(BF16)
    r = jax.nn.sigmoid(_dot(xb, wa_ref[...]) + ba_ref[...])
    gi = jax.nn.sigmoid(_dot(xb, wx_ref[...]) + bx_ref[...])
    nl = -lam_ref[...]
    softplus = jnp.maximum(nl, 0.0) + jnp.log(1.0 + jnp.exp(-jnp.abs(nl)))
    a = jnp.exp(-RG_C * r * softplus)
    bb = jnp.sqrt(1.0 - a * a) * gi * x2
    a_scr[...] = a.reshape(B, tc, W)
    b_scr[...] = bb.reshape(B, tc, W)

    def step(t, h):
        h = a_scr[:, pl.ds(t, 1), :] * h + b_scr[:, pl.ds(t, 1), :]
        b_scr[:, pl.ds(t, 1), :] = h
        return h

    h_scr[...] = lax.fori_loop(0, tc, step, h_scr[...], unroll=8)

    gate = blk_ref[:, :, W:2 * W]
    gelu = 0.5 * gate * (1.0 + jnp.tanh(math.sqrt(2.0 / math.pi) * (gate + 0.044715 * gate * gate * gate)))
    y = b_scr[...] * gelu
    y_ref[...] = (_rms(y, W) * og_ref[...][None]).astype(y_ref.dtype)


def _lru(lru_in, cw, cb, wa, ba, wx, bx, lam, og, tc):
    B, S, _ = lru_in.shape
    W = LRU_WIDTH
    full = lambda a: pl.BlockSpec(a.shape, lambda i: (0,) * a.ndim)
    return pl.pallas_call(
        _lru_kernel,
        grid=(S // tc,),
        in_specs=[pl.BlockSpec((B, tc, 2 * W), lambda i: (0, i, 0)), full(cw), full(cb), full(wa), full(ba),
                  full(wx), full(bx), full(lam), full(og)],
        out_specs=pl.BlockSpec((B, tc, W), lambda i: (0, i, 0)),
        out_shape=jax.ShapeDtypeStruct((B, S, W), BF16),
        scratch_shapes=[pltpu.VMEM((B, tc + LRU_HALO, W), F32), pltpu.VMEM((B, tc, W), F32),
                        pltpu.VMEM((B, tc, W), F32), pltpu.VMEM((B, 1, W), F32)],
        compiler_params=_cparams(("arbitrary",)),
        name="lru",
    )(lru_in, cw, cb, wa, ba, wx, bx, lam, og)


def _flash_maps(qt_list, k_ref, vt_ref, k_lanes, bias_of, tile_shift, m_scr, l_scr, acc_scr, tiles_per_step):
    qi = pl.program_id(2)
    n_maps = len(qt_list)
    tq = qt_list[0].shape[1]
    maps = range(n_maps)

    def biased_scores(j):
        jc = jnp.minimum(j, qi)
        start = pl.multiple_of(jc * KEY_TILE, KEY_TILE)
        is_own = (j == qi).astype(jnp.int32)
        return tuple(_dot(k_ref[0, pl.ds(start, KEY_TILE), k_lanes[i]], qt_list[i]) + bias_of(i, is_own)
                     for i in maps)

    for i in maps:
        m_scr[i] = jnp.full((1, tq), NEG_BIG, F32)
        l_scr[i] = jnp.zeros((1, tq), F32)
        acc_scr[i] = jnp.zeros(acc_scr.shape[1:], F32)

    def update(j, s_tile):
        vt = vt_ref[0, jnp.minimum(j, qi)]
        for i in maps:
            s = s_tile[i]
            c = tile_shift(i, qi - j)
            m_old = m_scr[i]
            m_new = jnp.maximum(m_old, jnp.max(s, axis=0, keepdims=True) + c)
            alpha = jnp.exp2(m_old - m_new)
            p = jnp.exp2(s - (m_new - c))
            m_scr[i] = m_new
            l_scr[i] = alpha * l_scr[i] + jnp.sum(p, axis=0, keepdims=True)
            acc_scr[i] = alpha * acc_scr[i] + _dot(vt, p.astype(BF16))

    def one_tile(j, s_tile):
        s_ahead = biased_scores(j + 1)
        update(j, s_tile)
        return s_ahead

    def group(t, s_tile):
        for u in range(tiles_per_step):
            s_tile = one_tile(tiles_per_step * t + u, s_tile)
        return s_tile

    n_groups = (qi + 1) // tiles_per_step
    s_rest = lax.fori_loop(0, n_groups, group, biased_scores(0))
    lax.fori_loop(n_groups * tiles_per_step, qi + 1, one_tile, s_rest)


def _chunk_allowed_t(tq):
    krow = lax.broadcasted_iota(jnp.int32, (KEY_TILE, tq), 0)
    qcol = lax.broadcasted_iota(jnp.int32, (KEY_TILE, tq), 1)
    return krow, qcol, (krow // CHUNK) <= (qcol // CHUNK)


def _diff_attn_kernel(qt_ref, k_ref, vt_ref, slope_ref, lq1_ref, lk1_ref, lq2_ref, lk2_ref, subg_ref, o_ref,
                      m_scr, l_scr, acc_scr, bias_scr, *, lambda_init, tq):
    qt = qt_ref[0]
    feat = lax.broadcasted_iota(jnp.int32, qt.shape, 0)
    zero = jnp.zeros_like(qt)
    qt_list = [jnp.where(feat // DIFF_HEAD_DIM == i, qt, zero) for i in range(4)]
    slope = [slope_ref[0, 0:1, hh:hh + 1] for hh in range(2)]

    @pl.when(pl.program_id(2) == 0)
    def _():
        krow, qcol, allowed = _chunk_allowed_t(tq)
        kf, qf = krow.astype(F32), qcol.astype(F32)
        self_dist = qf - jnp.abs(qf - kf)
        for hh in range(2):
            bias_scr[hh, 0] = slope[hh] * kf
            bias_scr[hh, 1] = jnp.where(allowed, slope[hh] * self_dist, NEG_BIG)

    def tile_shift(i, n_tiles):
        return -slope[i // 2] * (n_tiles * KEY_TILE).astype(F32)

    _flash_maps(qt_list, k_ref, vt_ref, [slice(None)] * 4, lambda i, is_own: bias_scr[i // 2, is_own],
                tile_shift, m_scr, l_scr, acc_scr, TILES_PER_GROUP)

    lam = (jnp.exp(jnp.sum(lq1_ref[...] * lk1_ref[...], axis=-1, keepdims=True))
           - jnp.exp(jnp.sum(lq2_ref[...] * lk2_ref[...], axis=-1, keepdims=True)) + lambda_init)
    orow = lax.broadcasted_iota(jnp.int32, (LANES, tq), 0)
    out_t = jnp.zeros((LANES, tq), F32)
    for hh in range(2):
        o = acc_scr[2 * hh] / l_scr[2 * hh] - lam * (acc_scr[2 * hh + 1] / l_scr[2 * hh + 1])
        mine = (orow // DIFF_V_DIM) == hh
        ms = jnp.sum(jnp.where(mine, o * o, 0.0), axis=0, keepdims=True) * (1.0 / DIFF_V_DIM)
        out_t = jnp.where(mine, o * lax.rsqrt(ms + EPS), out_t)
    o_ref[0] = (out_t.T * subg_ref[...] * (1.0 - lambda_init)).astype(o_ref.dtype)


def _diff_attn(dqt, dk, dvt, slopes, lq1, lk1, lq2, lk2, subg, lambda_init):
    B, S, _ = dk.shape
    tq = KEY_TILE
    n_pairs = DIFF_HEADS // 2
    full = lambda a: pl.BlockSpec(a.shape, lambda b, p, i: (0,) * a.ndim)
    kern = functools.partial(_diff_attn_kernel, lambda_init=lambda_init, tq=tq)
    return pl.pallas_call(
        kern,
        grid=(B, n_pairs, S // tq),
        in_specs=[pl.BlockSpec((1, LANES, tq), lambda b, p, i: (b, p, i)),
                  pl.BlockSpec((1, S, LANES), lambda b, p, i: (b, 0, p)),
                  pl.BlockSpec((1, S // KEY_TILE, LANES, KEY_TILE), lambda b, p, i: (b, 0, p, 0)),
                  pl.BlockSpec((1, 1, LANES), lambda b, p, i: (p, 0, 0)),
                  full(lq1), full(lk1), full(lq2), full(lk2), full(subg)],
        out_specs=pl.BlockSpec((1, tq, LANES), lambda b, p, i: (b, i, p)),
        out_shape=jax.ShapeDtypeStruct((B, S, DIFF_WIDTH), BF16),
        scratch_shapes=[pltpu.VMEM((4, 1, tq), F32), pltpu.VMEM((4, 1, tq), F32), pltpu.VMEM((4, LANES, tq), F32),
                        pltpu.VMEM((2, 2, KEY_TILE, tq), F32)],
        compiler_params=_cparams(("parallel", "parallel", "arbitrary")),
        name="diff_attn",
    )(dqt, dk, dvt, slopes, lq1, lk1, lq2, lk2, subg)


def _mla_attn_kernel(qt_ref, k_ref, vt_ref, o_ref, m_scr, l_scr, acc_scr, bias_scr, *, tq):
    qt = qt_ref[0]
    qt_list = [qt[hh * MLA_HEAD_PAD:(hh + 1) * MLA_HEAD_PAD, :] for hh in range(2)]
    k_lanes = [slice(hh * MLA_HEAD_PAD, (hh + 1) * MLA_HEAD_PAD) for hh in range(2)]

    @pl.when(pl.program_id(2) == 0)
    def _():
        _, _, allowed = _chunk_allowed_t(tq)
        bias_scr[0] = jnp.zeros((KEY_TILE, tq), F32)
        bias_scr[1] = jnp.where(allowed, 0.0, NEG_BIG)

    _flash_maps(qt_list, k_ref, vt_ref, k_lanes, lambda i, is_own: bias_scr[is_own], lambda i, n: 0.0,
                m_scr, l_scr, acc_scr, TILES_PER_GROUP)
    orow = lax.broadcasted_iota(jnp.int32, (LANES, tq), 0)
    out_t = jnp.where(orow < MLA_V_DIM, acc_scr[0] / l_scr[0], acc_scr[1] / l_scr[1])
    o_ref[0] = out_t.T.astype(o_ref.dtype)


def _mla_attn(mqt, mk, mvt):
    B, S, _ = mk.shape
    tq = KEY_TILE
    n_pairs = MLA_HEADS // 2
    kern = functools.partial(_mla_attn_kernel, tq=tq)
    return pl.pallas_call(
        kern,
        grid=(B, n_pairs, S // tq),
        in_specs=[pl.BlockSpec((1, 2 * MLA_HEAD_PAD, tq), lambda b, p, i: (b, p, i)),
                  pl.BlockSpec((1, S, 2 * MLA_HEAD_PAD), lambda b, p, i: (b, 0, p)),
                  pl.BlockSpec((1, S // KEY_TILE, LANES, KEY_TILE), lambda b, p, i: (b, 0, p, 0))],
        out_specs=pl.BlockSpec((1, tq, LANES), lambda b, p, i: (b, i, p)),
        out_shape=jax.ShapeDtypeStruct((B, S, MLA_WIDTH), F32),
        scratch_shapes=[pltpu.VMEM((2, 1, tq), F32), pltpu.VMEM((2, 1, tq), F32), pltpu.VMEM((2, LANES, tq), F32),
                        pltpu.VMEM((2, KEY_TILE, tq), F32)],
        compiler_params=_cparams(("parallel", "parallel", "arbitrary")),
        name="mla_attn",
    )(mqt, mk, mvt)


ROUTER_PAD = LANES


def _out_proj_kernel(x_ref, ylru_ref, ydiff_ref, ymla_ref, mlag_ref, wo_ref, g2_ref, wrh_ref, wrl_ref, rb_ref,
                     x1_ref, h2_ref, eid_ref, gate_ref, cnt_ref):
    @pl.when(pl.program_id(0) == 0)
    def _():
        cnt_ref[...] = jnp.zeros(cnt_ref.shape, F32)

    ymla = (_rms(ymla_ref[...], MLA_WIDTH) * mlag_ref[...]).astype(BF16)
    x1 = (x_ref[...]
          + _dot(ylru_ref[...], wo_ref[0:LRU_WIDTH, :])
          + _dot(ydiff_ref[...], wo_ref[LRU_WIDTH:LRU_WIDTH + DIFF_WIDTH, :])
          + _dot(ymla, wo_ref[LRU_WIDTH + DIFF_WIDTH:, :]))
    x1_ref[...] = x1

    h2 = _rms(x1, D_MODEL) * g2_ref[...]
    hi, lo = _split_bf16(h2)
    h2_ref[...] = hi
    wrh = wrh_ref[...]
    logits = _dot(hi, wrh) + _dot(hi, wrl_ref[...]) + _dot(lo, wrh) + rb_ref[...]

    lane = lax.broadcasted_iota(jnp.int32, logits.shape, 1)
    big = jnp.int32(1 << 20)
    is_group = lane < N_GROUPS
    gl = jnp.where(is_group, logits, NEG_BIG)
    gmax = jnp.max(gl, axis=-1, keepdims=True)
    g_idx = jnp.min(jnp.where(gl == gmax, lane, big), axis=-1, keepdims=True)
    pg_top = 1.0 / jnp.sum(jnp.where(is_group, jnp.exp(gl - gmax), 0.0), axis=-1, keepdims=True)

    e_lane = lane - N_GROUPS
    in_group = (e_lane >= 0) & (e_lane < N_EXPERTS) & ((e_lane // EXPERTS_PER_GROUP) == g_idx)
    el = jnp.where(in_group, logits, NEG_BIG)
    m1 = jnp.max(el, axis=-1, keepdims=True)
    i1 = jnp.min(jnp.where(el == m1, lane, big), axis=-1, keepdims=True)
    el2 = jnp.where(lane == i1, NEG_BIG, el)
    m2 = jnp.max(el2, axis=-1, keepdims=True)
    i2 = jnp.min(jnp.where(el2 == m2, lane, big), axis=-1, keepdims=True)
    e2 = jnp.exp(m2 - m1)
    g1 = pg_top / (1.0 + e2)
    g2 = pg_top * e2 / (1.0 + e2)

    two = lax.broadcasted_iota(jnp.int32, eid_ref.shape, 1)
    eid_ref[...] = jnp.where(two == 0, i1 - N_GROUPS, i2 - N_GROUPS)
    gate_ref[...] = jnp.where(two == 0, g1, g2)
    onehot = ((e_lane == i1 - N_GROUPS) | (e_lane == i2 - N_GROUPS)).astype(F32)
    cnt_ref[...] += jnp.sum(onehot, axis=0, keepdims=True)


def _out_proj(x, ylru, ydiff, ymla, mlag, wo, g2, wrh, wrl, rb, tm):
    N = x.shape[0]
    full = lambda a: pl.BlockSpec(a.shape, lambda i: (0,) * a.ndim)
    tok = lambda width: pl.BlockSpec((tm, width), lambda i: (i, 0))
    return pl.pallas_call(
        _out_proj_kernel,
        grid=(N // tm,),
        in_specs=[tok(D_MODEL), tok(LRU_WIDTH), tok(DIFF_WIDTH), tok(MLA_WIDTH), full(mlag), full(wo), full(g2),
                  full(wrh), full(wrl), full(rb)],
        out_specs=[tok(D_MODEL), tok(D_MODEL), tok(2), tok(2), pl.BlockSpec((1, ROUTER_PAD), lambda i: (0, 0))],
        out_shape=[jax.ShapeDtypeStruct((N, D_MODEL), F32), jax.ShapeDtypeStruct((N, D_MODEL), BF16),
                   jax.ShapeDtypeStruct((N, 2), jnp.int32),
                   jax.ShapeDtypeStruct((N, 2), F32), jax.ShapeDtypeStruct((1, ROUTER_PAD), F32)],
        compiler_params=_cparams(("arbitrary",)),
        name="out_proj",
    )(x, ylru, ydiff, ymla, mlag, wo, g2, wrh, wrl, rb)


TOK_TILE = 256
RUN_CHUNK = 8
TAB_ROWS = 8


def _rank_kernel(eid_ref, start_ref, loc_ref, tab_ref, carry):
    tm = eid_ref.shape[0]

    @pl.when(pl.program_id(0) == 0)
    def _():
        carry[...] = start_ref[...]

    lane = lax.broadcasted_iota(jnp.int32, (tm, LANES), 1)
    r = lax.broadcasted_iota(jnp.int32, (tm, tm), 0)
    c = lax.broadcasted_iota(jnp.int32, (tm, tm), 1)
    lower = (c < r).astype(BF16)
    eid = eid_ref[...]
    onehot = [lane == eid[:, k:k + 1] for k in range(2)]
    cnt = [jnp.sum(oh.astype(F32), axis=0, keepdims=True) for oh in onehot]
    cnt_tile = cnt[0] + cnt[1]

    chunks = jnp.floor((cnt_tile + (RUN_CHUNK - 1)) * (1.0 / RUN_CHUNK))
    lane8 = lax.broadcasted_iota(jnp.int32, (TAB_ROWS, LANES), 1)
    incl = jnp.broadcast_to(chunks, (TAB_ROWS, LANES))
    shift = 1
    while shift < LANES:
        incl = incl + jnp.where(lane8 >= shift, pltpu.roll(incl, shift, 1), 0.0)
        shift *= 2
    offs = (incl[0:1] - chunks) * RUN_CHUNK

    locs = []
    for k in range(2):
        before = _dot(lower, onehot[k].astype(BF16))
        first = offs if k == 0 else offs + cnt[0]
        locs.append(jnp.sum(jnp.where(onehot[k], before + first, 0.0), axis=-1, keepdims=True))
    two = lax.broadcasted_iota(jnp.int32, loc_ref.shape, 1)
    loc_ref[...] = jnp.where(two == 0, locs[0], locs[1]).astype(jnp.int32)

    row8 = lax.broadcasted_iota(jnp.int32, (TAB_ROWS, LANES), 0)
    base = carry[...]
    tab = jnp.where(row8 == 0, cnt_tile, jnp.where(row8 == 1, offs, jnp.where(row8 == 2, base, 0.0)))
    tab_ref[...] = tab.astype(jnp.int32)
    carry[...] = base + cnt_tile


def _rank(eid, starts):
    N = eid.shape[0]
    n_tiles = N // TOK_TILE
    return pl.pallas_call(
        _rank_kernel,
        grid=(n_tiles,),
        in_specs=[pl.BlockSpec((TOK_TILE, 2), lambda i: (i, 0)), pl.BlockSpec((1, LANES), lambda i: (0, 0))],
        out_specs=[pl.BlockSpec((TOK_TILE, 2), lambda i: (i, 0)), pl.BlockSpec((TAB_ROWS, LANES), lambda i: (i, 0))],
        out_shape=[jax.ShapeDtypeStruct((N, 2), jnp.int32),
                   jax.ShapeDtypeStruct((n_tiles * TAB_ROWS, LANES), jnp.int32)],
        scratch_shapes=[pltpu.VMEM((1, LANES), F32)],
        compiler_params=_cparams(("arbitrary",)),
        name="rank",
    )(eid, starts)


SORT_ROWS = 2 * TOK_TILE + N_EXPERTS * RUN_CHUNK


def _run_chunks(tab_ref, tile, e):
    cnt = tab_ref[(tile * 3 + 0) * N_EXPERTS + e]
    offs = tab_ref[(tile * 3 + 1) * N_EXPERTS + e]
    base = tab_ref[(tile * 3 + 2) * N_EXPERTS + e]
    return lax.shift_right_logical(cnt + (RUN_CHUNK - 1), RUN_CHUNK.bit_length() - 1), offs, base


def _rows(ref, row, n):
    return ref.at[pl.ds(pl.multiple_of(row * ROW_TILES, ROW_TILES), n * ROW_TILES)]


def _for_each_chunk(tab_ref, tile, fn):
    for e in range(N_EXPERTS):
        chunks, offs, base = _run_chunks(tab_ref, tile, e)

        def one(k, carry):
            fn(offs + k * RUN_CHUNK, base + k * RUN_CHUNK)
            return carry

        lax.fori_loop(0, chunks, one, 0)


def _drain_chunks(tab_ref, tile, wait_one):
    total = 0
    for e in range(N_EXPERTS):
        total = total + _run_chunks(tab_ref, tile, e)[0]

    def one(k, carry):
        wait_one()
        return carry

    lax.fori_loop(0, total, one, 0)


def _dispatch_kernel(tab_ref, ztab_ref, h_ref, loc_ref, rows_hbm, sbuf, zbuf, sems, zsem):
    i = pl.program_id(0)
    n_tiles = pl.num_programs(0)
    slot = lax.rem(i, 2)
    T = TOK_TILE

    def run_copy(s, offs_row, base_row):
        return pltpu.make_async_copy(_rows(sbuf, s * SORT_ROWS + offs_row, RUN_CHUNK),
                                     _rows(rows_hbm, base_row, RUN_CHUNK), sems.at[s])

    @pl.when(i == 0)
    def _():
        zbuf[...] = jnp.zeros(zbuf.shape, F32)

        def zero_chunk(first_row):
            return pltpu.make_async_copy(zbuf, _rows(rows_hbm, first_row, RUN_CHUNK), zsem)

        def done(k, carry):
            zero_chunk(0).wait()
            return carry

        for e in range(N_EXPERTS):
            zero_chunk(ztab_ref[e]).start()
        lax.fori_loop(0, N_EXPERTS, done, 0)
        total = 0
        for e in range(N_EXPERTS):
            start, chunks = ztab_ref[N_EXPERTS + e], ztab_ref[2 * N_EXPERTS + e]
            total = total + chunks

            def fill(k, carry, start=start):
                zero_chunk(start + k * RUN_CHUNK).start()
                return carry

            lax.fori_loop(0, chunks, fill, 0)
        lax.fori_loop(0, total, done, 0)

    @pl.when(i > 0)
    def _():
        _drain_chunks(tab_ref, i - 1, lambda: run_copy(1 - slot, 0, 0).wait())

    locf = loc_ref[...].astype(F32)
    loc_lanes = [jnp.broadcast_to(locf[:, k:k + 1], (T, LANES)).T[0:1, :] for k in range(2)]
    pos = lax.broadcasted_iota(jnp.int32, (SORT_ROWS, T), 0).astype(F32)
    perm = jnp.where((pos == loc_lanes[0]) | (pos == loc_lanes[1]), 1.0, 0.0).astype(BF16)
    sorted_rows = _dot(perm, h_ref[...])
    for c in range(ROW_TILES):
        sbuf[pl.ds(slot * (SORT_ROWS * ROW_TILES) + c, SORT_ROWS, stride=ROW_TILES), :] = (
            sorted_rows[:, c * LANES:(c + 1) * LANES])

    _for_each_chunk(tab_ref, i, lambda offs_row, base_row: run_copy(slot, offs_row, base_row).start())

    @pl.when(i == n_tiles - 1)
    def _():
        _drain_chunks(tab_ref, i, lambda: run_copy(slot, 0, 0).wait())


def _dispatch(tabs, ztab, h2, loc, n_rows):
    N = h2.shape[0]
    return pl.pallas_call(
        _dispatch_kernel,
        grid_spec=pltpu.PrefetchScalarGridSpec(
            num_scalar_prefetch=2,
            grid=(N // TOK_TILE,),
            in_specs=[pl.BlockSpec((TOK_TILE, D_MODEL), lambda i, t, z: (i, 0)),
                      pl.BlockSpec((TOK_TILE, 2), lambda i, t, z: (i, 0))],
            out_specs=pl.BlockSpec(memory_space=pl.ANY),
            scratch_shapes=[pltpu.VMEM((2 * SORT_ROWS * ROW_TILES, LANES), F32),
                            pltpu.VMEM((RUN_CHUNK * ROW_TILES, LANES), F32),
                            pltpu.SemaphoreType.DMA((2,)), pltpu.SemaphoreType.DMA(())],
        ),
        out_shape=jax.ShapeDtypeStruct((n_rows * ROW_TILES, LANES), F32),
        compiler_params=_cparams(("arbitrary",)),
        name="dispatch",
    )(tabs, ztab, h2, loc)


def _expert_kernel(be_ref, nused_ref, rows_ref, w1_ref, w3_ref, w2_ref, y_ref):
    del be_ref

    @pl.when(pl.program_id(0) < nused_ref[0])
    def _():
        h = jnp.concatenate([rows_ref[pl.ds(c, ROW_BLOCK, stride=ROW_TILES), :] for c in range(ROW_TILES)],
                            axis=1).astype(BF16)
        a = _dot(h, w1_ref[0].astype(BF16))
        b = _dot(h, w3_ref[0].astype(BF16))
        z = (a * jax.nn.sigmoid(a) * b).astype(BF16)
        y = _dot(z, w2_ref[0].astype(BF16))
        for c in range(ROW_TILES):
            y_ref[pl.ds(c, ROW_BLOCK, stride=ROW_TILES), :] = y[:, c * LANES:(c + 1) * LANES]

    @pl.when(pl.program_id(0) >= nused_ref[0])
    def _():
        y_ref[...] = jnp.zeros(y_ref.shape, y_ref.dtype)


def _experts(blk_expert, n_used, rows, w1, w3, w2):
    n_blk = blk_expert.shape[0]
    row_blk = lambda i, be, nu: (jnp.minimum(i, nu[0] - 1), 0)
    wspec = lambda shape: pl.BlockSpec((1,) + shape, lambda i, be, nu: (be[i], 0, 0))
    return pl.pallas_call(
        _expert_kernel,
        grid_spec=pltpu.PrefetchScalarGridSpec(
            num_scalar_prefetch=2,
            grid=(n_blk,),
            in_specs=[pl.BlockSpec((ROW_BLOCK * ROW_TILES, LANES), row_blk),
                      wspec((D_MODEL, D_EXPERT)), wspec((D_MODEL, D_EXPERT)), wspec((D_EXPERT, D_MODEL))],
            out_specs=pl.BlockSpec((ROW_BLOCK * ROW_TILES, LANES), lambda i, be, nu: (i, 0)),
        ),
        out_shape=jax.ShapeDtypeStruct((n_blk * ROW_BLOCK * ROW_TILES, LANES), F32),
        compiler_params=_cparams(("arbitrary",)),
        name="experts",
    )(blk_expert, n_used, rows, w1, w3, w2)


def _combine_kernel(tab_ref, x1_ref, gate_ref, loc_ref, y_hbm, o_ref, ybuf, sems):
    i = pl.program_id(0)
    n_tiles = pl.num_programs(0)
    slot = lax.rem(i, 2)
    T = TOK_TILE

    def run_copy(s, offs_row, base_row):
        return pltpu.make_async_copy(_rows(y_hbm, base_row, RUN_CHUNK),
                                     _rows(ybuf, s * SORT_ROWS + offs_row, RUN_CHUNK), sems.at[s])

    def fetch(tile, s):
        _for_each_chunk(tab_ref, tile, lambda offs_row, base_row: run_copy(s, offs_row, base_row).start())

    @pl.when(i == 0)
    def _():
        ybuf[...] = jnp.zeros(ybuf.shape, F32)
        fetch(0, 0)

    @pl.when(i + 1 < n_tiles)
    def _():
        fetch(i + 1, 1 - slot)

    _drain_chunks(tab_ref, i, lambda: run_copy(slot, 0, 0).wait())

    y_sorted = jnp.concatenate([ybuf[pl.ds(slot * (SORT_ROWS * ROW_TILES) + c, SORT_ROWS, stride=ROW_TILES), :]
                                for c in range(ROW_TILES)], axis=1).astype(BF16)
    gate = gate_ref[...]
    loc = loc_ref[...]
    pos = lax.broadcasted_iota(jnp.int32, (T, SORT_ROWS), 1)
    g = (jnp.where(pos == loc[:, 0:1], gate[:, 0:1], 0.0) + jnp.where(pos == loc[:, 1:2], gate[:, 1:2], 0.0))
    g_hi, g_lo = _split_bf16(g)
    o_ref[...] = x1_ref[...] + _dot(g_hi, y_sorted) + _dot(g_lo, y_sorted)


def _combine(tabs, x1, gate, loc, y_rows):
    N = gate.shape[0]
    tok = lambda width: pl.BlockSpec((TOK_TILE, width), lambda i, t: (i, 0))
    return pl.pallas_call(
        _combine_kernel,
        grid_spec=pltpu.PrefetchScalarGridSpec(
            num_scalar_prefetch=1,
            grid=(N // TOK_TILE,),
            in_specs=[tok(D_MODEL), tok(2), tok(2), pl.BlockSpec(memory_space=pl.ANY)],
            out_specs=tok(D_MODEL),
            scratch_shapes=[pltpu.VMEM((2 * SORT_ROWS * ROW_TILES, LANES), F32), pltpu.SemaphoreType.DMA((2,))],
        ),
        out_shape=jax.ShapeDtypeStruct((N, D_MODEL), F32),
        compiler_params=_cparams(("arbitrary",)),
        name="combine",
    )(tabs, x1, gate, loc, y_rows)


def _block_diag(w):
    n, a, b = w.shape
    eye = np.eye(n, dtype=np.float32)
    return (w[:, :, None, :] * eye[:, None, :, None]).reshape(n * a, n * b)


def _pad_heads(v, used):
    lead = v.shape[:-1]
    v = v.reshape(lead + (MLA_HEADS, used))
    v = jnp.pad(v, [(0, 0)] * len(lead) + [(0, 0), (0, MLA_HEAD_PAD - used)])
    return v.reshape(lead + (MLA_QK_PAD,))


def _rotary_lane_tables(seq):
    half = MLA_ROPE_DIM // 2
    inv_freq = ROPE_THETA ** (-jnp.arange(half, dtype=F32) / half)
    ang = jnp.arange(seq, dtype=F32)[:, None] * inv_freq[None, :]
    cos, sin = jnp.cos(ang), jnp.sin(ang)
    ones = jnp.ones((seq, MLA_NOPE_DIM), F32)
    tail = MLA_HEAD_PAD - MLA_QK_DIM
    cos_t = jnp.concatenate([ones, cos, cos, jnp.ones((seq, tail), F32)], axis=1)
    sin_t = jnp.concatenate([0.0 * ones, -sin, sin, jnp.zeros((seq, tail), F32)], axis=1)
    return cos_t, sin_t


def _pick_tile(n, pref):
    t = min(n, pref)
    while n % t:
        t //= 2
    return t


def kernel(x, norm1_g, w_in, lru_conv_w, lru_conv_b, lru_wa, lru_ba, lru_wx, lru_bx, lru_lambda, lru_out_g,
           diff_q_g, diff_k_g, diff_lq1, diff_lk1, diff_lq2, diff_lk2, diff_sub_g,
           mla_cq_g, mla_ckv_g, mla_w_uq, mla_w_ukv, mla_q_g, mla_k_g, mla_out_g, w_out, norm2_g,
           router_g_w, router_g_b, router_e_w, router_e_b, exp_w1, exp_w3, exp_w2):
    B, S, D = x.shape
    N = B * S
    depth = w_in.shape[0]
    assert S % KEY_TILE == 0 and D == D_MODEL
    ts = _pick_tile(S, 512)
    tc = _pick_tile(S, 512)
    assert N % TOK_TILE == 0
    tm = _pick_tile(N, 512)

    cos_t, sin_t = _rotary_lane_tables(S)
    n_dgroups = DIFF_QK_WIDTH // DIFF_HEAD_DIM
    gsum = jnp.asarray(np.kron(np.eye(n_dgroups), np.ones((DIFF_HEAD_DIM, DIFF_HEAD_DIM))), BF16)
    slopes = np.asarray([2.0 ** (-ALIBI_MAX_BIAS * (h + 1) / DIFF_HEADS) * LOG2E for h in range(DIFF_HEADS)], np.float32)
    slopes = jnp.asarray(np.pad(slopes.reshape(DIFF_HEADS // 2, 1, 2), ((0, 0), (0, 0), (0, LANES - 2))))
    place = np.zeros((MLA_ROPE_DIM, MLA_HEADS, MLA_HEAD_PAD), np.float32)
    place[np.arange(MLA_ROPE_DIM), :, MLA_NOPE_DIM + np.arange(MLA_ROPE_DIM)] = 1.0
    place = jnp.asarray(place.reshape(MLA_ROPE_DIM, MLA_QK_PAD))
    row = lambda v: v.reshape(1, -1).astype(F32)

    n_blk = -(-(2 * N + N_EXPERTS * (RUN_CHUNK + ROW_BLOCK)) // ROW_BLOCK) + 1
    n_rows = n_blk * ROW_BLOCK

    for l in range(depth):
        dqg = row(jnp.tile(diff_q_g[l], n_dgroups) * (DIFF_HEAD_DIM ** -0.5 * LOG2E))
        dkg = row(jnp.tile(diff_k_g[l], n_dgroups))
        wuq = _pad_heads(mla_w_uq[l], MLA_QK_DIM).astype(BF16)
        mqg = row(_pad_heads(jnp.tile(mla_q_g[l], MLA_HEADS) * (MLA_QK_DIM ** -0.5 * LOG2E), MLA_QK_DIM))
        mkg = row(_pad_heads(jnp.tile(mla_k_g[l], MLA_HEADS), MLA_QK_DIM))
        wukv = mla_w_ukv[l].reshape(MLA_KV_RANK, MLA_HEADS, MLA_NOPE_DIM + MLA_V_DIM)
        wk_nope = _pad_heads(wukv[:, :, :MLA_NOPE_DIM].reshape(MLA_KV_RANK, -1), MLA_NOPE_DIM)
        wk = jnp.concatenate([wk_nope, place], axis=0).astype(BF16)
        wv = wukv[:, :, MLA_NOPE_DIM:].reshape(MLA_KV_RANK, MLA_WIDTH).astype(BF16)

        lru_in, dq, dk, dv, mq, mk, mv = _in_proj(
            x, row(norm1_g[l]), w_in, l, gsum, dqg, dkg, row(mla_cq_g[l]), wuq, mqg, row(mla_ckv_g[l]), wk, mkg, wv,
            cos_t, sin_t, ts)

        y_lru = _lru(lru_in, lru_conv_w[l], row(lru_conv_b[l]), _block_diag(lru_wa[l]).astype(BF16),
                     row(lru_ba[l]), _block_diag(lru_wx[l]).astype(BF16), row(lru_bx[l]), row(lru_lambda[l]),
                     row(lru_out_g[l]), tc)

        lambda_init = 0.8 - 0.6 * math.exp(-0.3 * l)
        y_diff = _diff_attn(dq, dk, dv, slopes, row(diff_lq1[l]), row(diff_lk1[l]), row(diff_lq2[l]),
                            row(diff_lk2[l]), row(jnp.tile(diff_sub_g[l], 2)), lambda_init)
        y_mla = _mla_attn(mq, mk, mv)

        wr = jnp.concatenate([router_g_w[l], router_e_w[l],
                              jnp.zeros((D, ROUTER_PAD - N_GROUPS - N_EXPERTS), F32)], axis=1)
        wrh = wr.astype(BF16)
        wrl = (wr - wrh.astype(F32)).astype(BF16)
        rb = jnp.pad(jnp.concatenate([router_g_b[l], router_e_b[l]]), (0, ROUTER_PAD - N_GROUPS - N_EXPERTS))
        x1, h2, eid, gate, cnt = _out_proj(
            x.reshape(N, D), y_lru.reshape(N, LRU_WIDTH), y_diff.reshape(N, DIFF_WIDTH), y_mla.reshape(N, MLA_WIDTH),
            row(mla_out_g[l]), w_out[l].astype(BF16), row(norm2_g[l]), wrh, wrl, row(rb), tm)

        counts = cnt[0, N_GROUPS:N_GROUPS + N_EXPERTS].astype(jnp.int32)
        padded = (counts + RUN_CHUNK + ROW_BLOCK - 1) // ROW_BLOCK * ROW_BLOCK
        pad_ends = jnp.cumsum(padded)
        pad_starts = pad_ends - padded
        blk_row0 = jnp.arange(n_blk, dtype=jnp.int32) * ROW_BLOCK
        blk_expert = jnp.minimum(jnp.sum((pad_ends[None, :] <= blk_row0[:, None]).astype(jnp.int32), axis=1),
                                 N_EXPERTS - 1)
        n_used = (pad_ends[-1:] // ROW_BLOCK).astype(jnp.int32)
        starts = jnp.pad(pad_starts.astype(F32), (0, LANES - N_EXPERTS)).reshape(1, LANES)
        tail_start = pad_starts + counts
        last = jnp.arange(N_EXPERTS) == N_EXPERTS - 1
        tail_ends = jnp.where(last, n_rows, pad_ends)
        aligned_start = (tail_start + RUN_CHUNK - 1) // RUN_CHUNK * RUN_CHUNK
        ztab = jnp.concatenate([tail_start, aligned_start, (tail_ends - aligned_start) // RUN_CHUNK]).astype(jnp.int32)

        loc, tab = _rank(eid, starts)
        tabs = tab.reshape(N // TOK_TILE, TAB_ROWS, LANES)[:, :3, :N_EXPERTS].reshape(-1)
        rows = _dispatch(tabs, ztab, h2, loc, n_rows)
        y_rows = _experts(blk_expert + l * N_EXPERTS, n_used, rows, exp_w1.reshape(-1, D, D_EXPERT),
                          exp_w3.reshape(-1, D, D_EXPERT), exp_w2.reshape(-1, D_EXPERT, D))
        x = _combine(tabs, x1, gate, loc, y_rows).reshape(B, S, D)
    return x
```

```python
import functools
import math

import jax
import jax.numpy as jnp
import numpy as np
from jax import lax
from jax.experimental import pallas as pl
from jax.experimental.pallas import tpu as pltpu

F32 = jnp.float32
BF16 = jnp.bfloat16

D_MODEL = 1024
CHUNK = 64
LRU_WIDTH = 256
CONV_WIDTH = 4
RG_C = 8.0
DIFF_HEADS = 6
DIFF_HEAD_DIM = 32
DIFF_V_DIM = 64
DIFF_QK_WIDTH = DIFF_HEADS * 2 * DIFF_HEAD_DIM
DIFF_WIDTH = DIFF_HEADS * DIFF_V_DIM
ALIBI_MAX_BIAS = 8.0
MLA_HEADS = 6
MLA_Q_RANK = 192
MLA_KV_RANK = 128
MLA_NOPE_DIM = 64
MLA_ROPE_DIM = 32
MLA_V_DIM = 64
MLA_QK_DIM = MLA_NOPE_DIM + MLA_ROPE_DIM
MLA_WIDTH = MLA_HEADS * MLA_V_DIM
ROPE_THETA = 10000.0
N_GROUPS = 4
EXPERTS_PER_GROUP = 8
N_EXPERTS = N_GROUPS * EXPERTS_PER_GROUP
D_EXPERT = 256
ROW_BLOCK = 256
EPS = 1e-6

LANES = 128
ROW_TILES = D_MODEL // LANES
MLA_HEAD_PAD = LANES
MLA_QK_PAD = MLA_HEADS * MLA_HEAD_PAD
VMEM_LIMIT = 56 * 1024 * 1024
NEG_BIG = -1e30
KEY_TILE = 256
TILES_PER_GROUP = 4
LOG2E = math.log2(math.e)

C_LRU = 0
C_DQ = 512
C_DK = 896
C_DV = 1280
C_CQ = 1664
C_CKV = 1856
C_KR = 1984
D_IN = 2016
D_IN_PAD = 2048


def _cparams(sem):
    return pltpu.CompilerParams(dimension_semantics=sem, vmem_limit_bytes=VMEM_LIMIT)


def _rms(v, width):
    return v * lax.rsqrt(jnp.sum(v * v, axis=-1, keepdims=True) * (1.0 / width) + EPS)


def _dot(a, b):
    return jnp.dot(a, b, preferred_element_type=F32)


def _split_bf16(v):
    hi = v.astype(BF16)
    lo = (v - hi.astype(F32)).astype(BF16)
    return hi, lo


def _in_proj_kernel(x_ref, g1_ref, w_ref, gsum_ref, dqg_ref, dkg_ref, cqg_ref, wuq_ref, mqg_ref, ckvg_ref,
                    wk_ref, mkg_ref, wv_ref, cos_ref, sin_ref, cosq_ref, sinq_ref,
                    lru_ref, dqt_ref, dk_ref, dvt_ref, mqt_ref, mk_ref, mvt_ref, w_bf):
    @pl.when((pl.program_id(0) == 0) & (pl.program_id(1) == 0))
    def _():
        w_bf[:, 0:D_IN] = w_ref[0].astype(BF16)
        w_bf[:, D_IN:] = jnp.zeros((D_MODEL, D_IN_PAD - D_IN), BF16)

    x = x_ref[0]
    hn = (_rms(x, D_MODEL) * g1_ref[...]).astype(BF16)
    p = _dot(hn, w_bf[...])
    lru_ref[0] = p[:, C_LRU:C_LRU + 2 * LRU_WIDTH]

    gsum = gsum_ref[...]

    def group_norm32(v, gain):
        hi, lo = _split_bf16(v * v)
        ss = _dot(hi, gsum) + _dot(lo, gsum)
        return v * lax.rsqrt(ss * (1.0 / DIFF_HEAD_DIM) + EPS) * gain

    def store_tiles(ref, vt):
        for c in range(vt.shape[1] // KEY_TILE):
            ref[0, c] = vt[:, c * KEY_TILE:(c + 1) * KEY_TILE].astype(BF16)

    t = x.shape[0]

    def along_lanes(per_feature):
        return jnp.concatenate([per_feature] * (t // LANES), axis=1)

    dq_t = p[:, C_DQ:C_DQ + DIFF_QK_WIDTH].T.reshape(DIFF_QK_WIDTH // DIFF_HEAD_DIM, DIFF_HEAD_DIM, t)
    dq_ss = jnp.sum(dq_t * dq_t, axis=1, keepdims=True) * (1.0 / DIFF_HEAD_DIM)
    dq_n = (dq_t * lax.rsqrt(dq_ss + EPS)).reshape(DIFF_QK_WIDTH, t) * along_lanes(dqg_ref[...])
    dqt_ref[0] = dq_n.astype(BF16)
    dk_ref[0] = group_norm32(p[:, C_DK:C_DK + DIFF_QK_WIDTH], dkg_ref[...]).astype(BF16)
    store_tiles(dvt_ref, p[:, C_DV:C_DV + DIFF_WIDTH].T)

    cos = cos_ref[...]
    sin = sin_ref[...]
    lane = lax.broadcasted_iota(jnp.int32, cos.shape, 1)
    first_half = lane < MLA_NOPE_DIM + MLA_ROPE_DIM // 2

    def head_norm_rotary(v, gain):
        outs = []
        for h in range(MLA_HEADS):
            c = v[:, h * MLA_HEAD_PAD:(h + 1) * MLA_HEAD_PAD]
            c = _rms(c, MLA_QK_DIM) * gain[:, h * MLA_HEAD_PAD:(h + 1) * MLA_HEAD_PAD]
            swapped = jnp.where(first_half, pltpu.roll(c, MLA_HEAD_PAD - MLA_ROPE_DIM // 2, 1),
                                pltpu.roll(c, MLA_ROPE_DIM // 2, 1))
            outs.append(c * cos + swapped * sin)
        return jnp.concatenate(outs, axis=1)

    cq = (_rms(p[:, C_CQ:C_CQ + MLA_Q_RANK], MLA_Q_RANK) * cqg_ref[...]).astype(BF16)
    q_t = _dot(cq, wuq_ref[...]).T
    cos_q, sin_q = cosq_ref[...], sinq_ref[...]
    gain_q = along_lanes(mqg_ref[...])
    half = MLA_ROPE_DIM // 2
    heads_t = []
    for h in range(MLA_HEADS):
        rows = slice(h * MLA_HEAD_PAD, (h + 1) * MLA_HEAD_PAD)
        c = q_t[rows]
        ss = jnp.sum(c * c, axis=0, keepdims=True) * (1.0 / MLA_QK_DIM)
        c = c * lax.rsqrt(ss + EPS) * gain_q[rows]
        swapped = jnp.concatenate([c[:MLA_NOPE_DIM], c[MLA_NOPE_DIM + half:MLA_QK_DIM],
                                   c[MLA_NOPE_DIM:MLA_NOPE_DIM + half], c[MLA_QK_DIM:]], axis=0)
        heads_t.append((c * cos_q + swapped * sin_q).astype(BF16))
    mqt_ref[0] = jnp.concatenate(heads_t, axis=0)

    ckv = (_rms(p[:, C_CKV:C_CKV + MLA_KV_RANK], MLA_KV_RANK) * ckvg_ref[...]).astype(BF16)
    kcat = jnp.concatenate([ckv, p[:, C_KR:C_KR + MLA_ROPE_DIM].astype(BF16)], axis=1)
    mk_ref[0] = head_norm_rotary(_dot(kcat, wk_ref[...]), mkg_ref[...]).astype(BF16)
    store_tiles(mvt_ref, _dot(ckv, wv_ref[...]).T)


def _in_proj(x, g1, w_all, layer, gsum, dqg, dkg, cqg, wuq, mqg, ckvg, wk, mkg, wv, cos_t, sin_t, cos_q, sin_q, ts):
    B, S, _ = x.shape
    full = lambda a: pl.BlockSpec(a.shape, lambda b, i: (0,) * a.ndim)
    w_spec = pl.BlockSpec((1,) + w_all.shape[1:], lambda b, i: (layer, 0, 0))
    tok = lambda width: pl.BlockSpec((1, ts, width), lambda b, i: (b, i, 0))
    tok_t = lambda width: pl.BlockSpec((1, width, ts), lambda b, i: (b, 0, i))
    nkt = ts // KEY_TILE
    tiles_t = lambda width: pl.BlockSpec((1, nkt, width, KEY_TILE), lambda b, i: (b, i, 0, 0))
    rot = pl.BlockSpec((ts, LANES), lambda b, i: (i, 0))
    rot_t = pl.BlockSpec((LANES, ts), lambda b, i: (0, i))
    shp = lambda shape, dt: jax.ShapeDtypeStruct(shape, dt)
    return pl.pallas_call(
        _in_proj_kernel,
        grid=(B, S // ts),
        in_specs=[tok(D_MODEL), full(g1), w_spec, full(gsum), full(dqg), full(dkg), full(cqg), full(wuq),
                  full(mqg), full(ckvg), full(wk), full(mkg), full(wv), rot, rot, rot_t, rot_t],
        out_specs=[tok(2 * LRU_WIDTH), tok_t(DIFF_QK_WIDTH), tok(DIFF_QK_WIDTH), tiles_t(DIFF_WIDTH),
                   tok_t(MLA_QK_PAD), tok(MLA_QK_PAD), tiles_t(MLA_WIDTH)],
        out_shape=[shp((B, S, 2 * LRU_WIDTH), F32), shp((B, DIFF_QK_WIDTH, S), BF16), shp((B, S, DIFF_QK_WIDTH), BF16),
                   shp((B, S // KEY_TILE, DIFF_WIDTH, KEY_TILE), BF16), shp((B, MLA_QK_PAD, S), BF16),
                   shp((B, S, MLA_QK_PAD), BF16), shp((B, S // KEY_TILE, MLA_WIDTH, KEY_TILE), BF16)],
        scratch_shapes=[pltpu.VMEM((D_MODEL, D_IN_PAD), BF16)],
        compiler_params=_cparams(("arbitrary", "arbitrary")),
        name="in_proj",
    )(x, g1, w_all, gsum, dqg, dkg, cqg, wuq, mqg, ckvg, wk, mkg, wv, cos_t, sin_t, cos_q, sin_q)


LRU_HALO = 8


def _lru_kernel(blk_ref, cw_ref, cb_ref, wa_ref, ba_ref, wx_ref, bx_ref, lam_ref, og_ref, y_ref,
                ubuf, a_scr, b_scr, h_scr):
    B, tc, _ = blk_ref.shape
    W = LRU_WIDTH

    @pl.when(pl.program_id(0) == 0)
    def _():
        ubuf[:, 0:LRU_HALO, :] = jnp.zeros((B, LRU_HALO, W), F32)
        h_scr[...] = jnp.zeros(h_scr.shape, F32)

    ubuf[:, LRU_HALO:, :] = blk_ref[:, :, 0:W]
    xc = cb_ref[...][None]
    for j in range(CONV_WIDTH):
        off = LRU_HALO - (CONV_WIDTH - 1) + j
        xc = xc + cw_ref[j:j + 1, :][None] * ubuf[:, off:off + tc, :]
    ubuf[:, 0:LRU_HALO, :] = ubuf[:, tc:tc + LRU_HALO, :]

    x2 = xc.reshape(B * tc, W)
    xb = x2.astype(BF16)
    r = jax.nn.sigmoid(_dot(xb, wa_ref[...]) + ba_ref[...])
    gi = jax.nn.sigmoid(_dot(xb, wx_ref[...]) + bx_ref[...])
    nl = -lam_ref[...]
    softplus = jnp.maximum(nl, 0.0) + jnp.log(1.0 + jnp.exp(-jnp.abs(nl)))
    a = jnp.exp(-RG_C * r * softplus)
    bb = jnp.sqrt(1.0 - a * a) * gi * x2
    a_scr[...] = a.reshape(B, tc, W)
    b_scr[...] = bb.reshape(B, tc, W)

    def step(t, h):
        h = a_scr[:, pl.ds(t, 1), :] * h + b_scr[:, pl.ds(t, 1), :]
        b_scr[:, pl.ds(t, 1), :] = h
        return h

    h_scr[...] = lax.fori_loop(0, tc, step, h_scr[...], unroll=8)

    gate = blk_ref[:, :, W:2 * W]
    gelu = 0.5 * gate * (1.0 + jnp.tanh(math.sqrt(2.0 / math.pi) * (gate + 0.044715 * gate * gate * gate)))
    y = b_scr[...] * gelu
    y_ref[...] = (_rms(y, W) * og_ref[...][None]).astype(y_ref.dtype)


def _lru(lru_in, cw, cb, wa, ba, wx, bx, lam, og, tc):
    B, S, _ = lru_in.shape
    W = LRU_WIDTH
    full = lambda a: pl.BlockSpec(a.shape, lambda i: (0,) * a.ndim)
    return pl.pallas_call(
        _lru_kernel,
        grid=(S // tc,),
        in_specs=[pl.BlockSpec((B, tc, 2 * W), lambda i: (0, i, 0)), full(cw), full(cb), full(wa), full(ba),
                  full(wx), full(bx), full(lam), full(og)],
        out_specs=pl.BlockSpec((B, tc, W), lambda i: (0, i, 0)),
        out_shape=jax.ShapeDtypeStruct((B, S, W), BF16),
        scratch_shapes=[pltpu.VMEM((B, tc + LRU_HALO, W), F32), pltpu.VMEM((B, tc, W), F32),
                        pltpu.VMEM((B, tc, W), F32), pltpu.VMEM((B, 1, W), F32)],
        compiler_params=_cparams(("arbitrary",)),
        name="lru",
    )(lru_in, cw, cb, wa, ba, wx, bx, lam, og)


def _flash_maps(qt_list, k_ref, vt_ref, k_lanes, bias_of, tile_shift, m_scr, l_scr, acc_scr, tiles_per_step):
    qi = pl.program_id(2)
    n_maps = len(qt_list)
    tq = qt_list[0].shape[1]
    maps = range(n_maps)

    def biased_scores(j):
        jc = jnp.minimum(j, qi)
        start = pl.multiple_of(jc * KEY_TILE, KEY_TILE)
        is_own = (j == qi).astype(jnp.int32)
        return tuple(_dot(k_ref[0, pl.ds(start, KEY_TILE), k_lanes[i]], qt_list[i]) + bias_of(i, is_own)
                     for i in maps)

    for i in maps:
        m_scr[i] = jnp.full((1, tq), NEG_BIG, F32)
        l_scr[i] = jnp.zeros((1, tq), F32)
        acc_scr[i] = jnp.zeros(acc_scr.shape[1:], F32)

    def update(j, s_tile):
        vt = vt_ref[0, jnp.minimum(j, qi)]
        for i in maps:
            s = s_tile[i]
            c = tile_shift(i, qi - j)
            m_old = m_scr[i]
            m_new = jnp.maximum(m_old, jnp.max(s, axis=0, keepdims=True) + c)
            alpha = jnp.exp2(m_old - m_new)
            p = jnp.exp2(s - (m_new - c))
            m_scr[i] = m_new
            l_scr[i] = alpha * l_scr[i] + jnp.sum(p, axis=0, keepdims=True)
            acc_scr[i] = alpha * acc_scr[i] + _dot(vt, p.astype(BF16))

    def one_tile(j, s_tile):
        s_ahead = biased_scores(j + 1)
        update(j, s_tile)
        return s_ahead

    def group(t, s_tile):
        for u in range(tiles_per_step):
            s_tile = one_tile(tiles_per_step * t + u, s_tile)
        return s_tile

    n_groups = (qi + 1) // tiles_per_step
    s_rest = lax.fori_loop(0, n_groups, group, biased_scores(0))
    lax.fori_loop(n_groups * tiles_per_step, qi + 1, one_tile, s_rest)


def _chunk_allowed_t(tq):
    krow = lax.broadcasted_iota(jnp.int32, (KEY_TILE, tq), 0)
    qcol = lax.broadcasted_iota(jnp.int32, (KEY_TILE, tq), 1)
    return krow, qcol, (krow // CHUNK) <= (qcol // CHUNK)


def _diff_attn_kernel(qt_ref, k_ref, vt_ref, slope_ref, lq1_ref, lk1_ref, lq2_ref, lk2_ref, subg_ref, o_ref,
                      m_scr, l_scr, acc_scr, bias_scr, *, lambda_init, tq):
    qt = qt_ref[0]
    feat = lax.broadcasted_iota(jnp.int32, qt.shape, 0)
    zero = jnp.zeros_like(qt)
    qt_list = [jnp.where(feat // DIFF_HEAD_DIM == i, qt, zero) for i in range(4)]
    slope = [slope_ref[0, 0:1, hh:hh + 1] for hh in range(2)]

    @pl.when(pl.program_id(2) == 0)
    def _():
        krow, qcol, allowed = _chunk_allowed_t(tq)
        kf, qf = krow.astype(F32), qcol.astype(F32)
        self_dist = qf - jnp.abs(qf - kf)
        for hh in range(2):
            bias_scr[hh, 0] = slope[hh] * kf
            bias_scr[hh, 1] = jnp.where(allowed, slope[hh] * self_dist, NEG_BIG)

    def tile_shift(i, n_tiles):
        return -slope[i // 2] * (n_tiles * KEY_TILE).astype(F32)

    _flash_maps(qt_list, k_ref, vt_ref, [slice(None)] * 4, lambda i, is_own: bias_scr[i // 2, is_own],
                tile_shift, m_scr, l_scr, acc_scr, TILES_PER_GROUP)

    lam = (jnp.exp(jnp.sum(lq1_ref[...] * lk1_ref[...], axis=-1, keepdims=True))
           - jnp.exp(jnp.sum(lq2_ref[...] * lk2_ref[...], axis=-1, keepdims=True)) + lambda_init)
    orow = lax.broadcasted_iota(jnp.int32, (LANES, tq), 0)
    out_t = jnp.zeros((LANES, tq), F32)
    for hh in range(2):
        o = acc_scr[2 * hh] / l_scr[2 * hh] - lam * (acc_scr[2 * hh + 1] / l_scr[2 * hh + 1])
        mine = (orow // DIFF_V_DIM) == hh
        ms = jnp.sum(jnp.where(mine, o * o, 0.0), axis=0, keepdims=True) * (1.0 / DIFF_V_DIM)
        out_t = jnp.where(mine, o * lax.rsqrt(ms + EPS), out_t)
    o_ref[0] = (out_t.T * subg_ref[...] * (1.0 - lambda_init)).astype(o_ref.dtype)


def _diff_attn(dqt, dk, dvt, slopes, lq1, lk1, lq2, lk2, subg, lambda_init):
    B, S, _ = dk.shape
    tq = KEY_TILE
    n_pairs = DIFF_HEADS // 2
    full = lambda a: pl.BlockSpec(a.shape, lambda b, p, i: (0,) * a.ndim)
    kern = functools.partial(_diff_attn_kernel, lambda_init=lambda_init, tq=tq)
    return pl.pallas_call(
        kern,
        grid=(B, n_pairs, S // tq),
        in_specs=[pl.BlockSpec((1, LANES, tq), lambda b, p, i: (b, p, i)),
                  pl.BlockSpec((1, S, LANES), lambda b, p, i: (b, 0, p)),
                  pl.BlockSpec((1, S // KEY_TILE, LANES, KEY_TILE), lambda b, p, i: (b, 0, p, 0)),
                  pl.BlockSpec((1, 1, LANES), lambda b, p, i: (p, 0, 0)),
                  full(lq1), full(lk1), full(lq2), full(lk2), full(subg)],
        out_specs=pl.BlockSpec((1, tq, LANES), lambda b, p, i: (b, i, p)),
        out_shape=jax.ShapeDtypeStruct((B, S, DIFF_WIDTH), BF16),
        scratch_shapes=[pltpu.VMEM((4, 1, tq), F32), pltpu.VMEM((4, 1, tq), F32), pltpu.VMEM((4, LANES, tq), F32),
                        pltpu.VMEM((2, 2, KEY_TILE, tq), F32)],
        compiler_params=_cparams(("parallel", "parallel", "arbitrary")),
        name="diff_attn",
    )(dqt, dk, dvt, slopes, lq1, lk1, lq2, lk2, subg)


def _mla_attn_kernel(qt_ref, k_ref, vt_ref, o_ref, m_scr, l_scr, acc_scr, bias_scr, *, tq):
    qt = qt_ref[0]
    qt_list = [qt[hh * MLA_HEAD_PAD:(hh + 1) * MLA_HEAD_PAD, :] for hh in range(2)]
    k_lanes = [slice(hh * MLA_HEAD_PAD, (hh + 1) * MLA_HEAD_PAD) for hh in range(2)]

    @pl.when(pl.program_id(2) == 0)
    def _():
        _, _, allowed = _chunk_allowed_t(tq)
        bias_scr[0] = jnp.zeros((KEY_TILE, tq), F32)
        bias_scr[1] = jnp.where(allowed, 0.0, NEG_BIG)

    _flash_maps(qt_list, k_ref, vt_ref, k_lanes, lambda i, is_own: bias_scr[is_own], lambda i, n: 0.0,
                m_scr, l_scr, acc_scr, TILES_PER_GROUP)
    orow = lax.broadcasted_iota(jnp.int32, (LANES, tq), 0)
    out_t = jnp.where(orow < MLA_V_DIM, acc_scr[0] / l_scr[0], acc_scr[1] / l_scr[1])
    o_ref[0] = out_t.T.astype(o_ref.dtype)


def _mla_attn(mqt, mk, mvt):
    B, S, _ = mk.shape
    tq = KEY_TILE
    n_pairs = MLA_HEADS // 2
    kern = functools.partial(_mla_attn_kernel, tq=tq)
    return pl.pallas_call(
        kern,
        grid=(B, n_pairs, S // tq),
        in_specs=[pl.BlockSpec((1, 2 * MLA_HEAD_PAD, tq), lambda b, p, i: (b, p, i)),
                  pl.BlockSpec((1, S, 2 * MLA_HEAD_PAD), lambda b, p, i: (b, 0, p)),
                  pl.BlockSpec((1, S // KEY_TILE, LANES, KEY_TILE), lambda b, p, i: (b, 0, p, 0))],
        out_specs=pl.BlockSpec((1, tq, LANES), lambda b, p, i: (b, i, p)),
        out_shape=jax.ShapeDtypeStruct((B, S, MLA_WIDTH), F32),
        scratch_shapes=[pltpu.VMEM((2, 1, tq), F32), pltpu.VMEM((2, 1, tq), F32), pltpu.VMEM((2, LANES, tq), F32),
                        pltpu.VMEM((2, KEY_TILE, tq), F32)],
        compiler_params=_cparams(("parallel", "parallel", "arbitrary")),
        name="mla_attn",
    )(mqt, mk, mvt)


ROUTER_PAD = LANES


def _out_proj_kernel(x_ref, ylru_ref, ydiff_ref, ymla_ref, mlag_ref, wo_ref, g2_ref, wrh_ref, wrl_ref, rb_ref,
                     x1_ref, h2_ref, eid_ref, gate_ref, cnt_ref):
    @pl.when(pl.program_id(0) == 0)
    def _():
        cnt_ref[...] = jnp.zeros(cnt_ref.shape, F32)

    ymla = (_rms(ymla_ref[...], MLA_WIDTH) * mlag_ref[...]).astype(BF16)
    x1 = (x_ref[...]
          + _dot(ylru_ref[...], wo_ref[0:LRU_WIDTH, :])
          + _dot(ydiff_ref[...], wo_ref[LRU_WIDTH:LRU_WIDTH + DIFF_WIDTH, :])
          + _dot(ymla, wo_ref[LRU_WIDTH + DIFF_WIDTH:, :]))
    x1_ref[...] = x1

    h2 = _rms(x1, D_MODEL) * g2_ref[...]
    hi, lo = _split_bf16(h2)
    h2_ref[...] = hi
    wrh = wrh_ref[...]
    logits = _dot(hi, wrh) + _dot(hi, wrl_ref[...]) + _dot(lo, wrh) + rb_ref[...]

    lane = lax.broadcasted_iota(jnp.int32, logits.shape, 1)
    big = jnp.int32(1 << 20)
    is_group = lane < N_GROUPS
    gl = jnp.where(is_group, logits, NEG_BIG)
    gmax = jnp.max(gl, axis=-1, keepdims=True)
    g_idx = jnp.min(jnp.where(gl == gmax, lane, big), axis=-1, keepdims=True)
    pg_top = 1.0 / jnp.sum(jnp.where(is_group, jnp.exp(gl - gmax), 0.0), axis=-1, keepdims=True)

    e_lane = lane - N_GROUPS
    in_group = (e_lane >= 0) & (e_lane < N_EXPERTS) & ((e_lane // EXPERTS_PER_GROUP) == g_idx)
    el = jnp.where(in_group, logits, NEG_BIG)
    m1 = jnp.max(el, axis=-1, keepdims=True)
    i1 = jnp.min(jnp.where(el == m1, lane, big), axis=-1, keepdims=True)
    el2 = jnp.where(lane == i1, NEG_BIG, el)
    m2 = jnp.max(el2, axis=-1, keepdims=True)
    i2 = jnp.min(jnp.where(el2 == m2, lane, big), axis=-1, keepdims=True)
    e2 = jnp.exp(m2 - m1)
    g1 = pg_top / (1.0 + e2)
    g2 = pg_top * e2 / (1.0 + e2)

    two = lax.broadcasted_iota(jnp.int32, eid_ref.shape, 1)
    eid_ref[...] = jnp.where(two == 0, i1 - N_GROUPS, i2 - N_GROUPS)
    gate_ref[...] = jnp.where(two == 0, g1, g2)
    onehot = ((e_lane == i1 - N_GROUPS) | (e_lane == i2 - N_GROUPS)).astype(F32)
    cnt_ref[...] += jnp.sum(onehot, axis=0, keepdims=True)


def _out_proj(x, ylru, ydiff, ymla, mlag, wo, g2, wrh, wrl, rb, tm):
    N = x.shape[0]
    full = lambda a: pl.BlockSpec(a.shape, lambda i: (0,) * a.ndim)
    tok = lambda width: pl.BlockSpec((tm, width), lambda i: (i, 0))
    return pl.pallas_call(
        _out_proj_kernel,
        grid=(N // tm,),
        in_specs=[tok(D_MODEL), tok(LRU_WIDTH), tok(DIFF_WIDTH), tok(MLA_WIDTH), full(mlag), full(wo), full(g2),
                  full(wrh), full(wrl), full(rb)],
        out_specs=[tok(D_MODEL), tok(D_MODEL), tok(2), tok(2), pl.BlockSpec((1, ROUTER_PAD), lambda i: (0, 0))],
        out_shape=[jax.ShapeDtypeStruct((N, D_MODEL), F32), jax.ShapeDtypeStruct((N, D_MODEL), BF16),
                   jax.ShapeDtypeStruct((N, 2), jnp.int32),
                   jax.ShapeDtypeStruct((N, 2), F32), jax.ShapeDtypeStruct((1, ROUTER_PAD), F32)],
        compiler_params=_cparams(("arbitrary",)),
        name="out_proj",
    )(x, ylru, ydiff, ymla, mlag, wo, g2, wrh, wrl, rb)


TOK_TILE = 256
RUN_CHUNK = 8
TAB_ROWS = 8


def _rank_kernel(eid_ref, start_ref, loc_ref, tab_ref, carry):
    tm = eid_ref.shape[0]

    @pl.when(pl.program_id(0) == 0)
    def _():
        carry[...] = start_ref[...]

    lane = lax.broadcasted_iota(jnp.int32, (tm, LANES), 1)
    r = lax.broadcasted_iota(jnp.int32, (tm, tm), 0)
    c = lax.broadcasted_iota(jnp.int32, (tm, tm), 1)
    lower = (c < r).astype(BF16)
    eid = eid_ref[...]
    onehot = [lane == eid[:, k:k + 1] for k in range(2)]
    cnt = [jnp.sum(oh.astype(F32), axis=0, keepdims=True) for oh in onehot]
    cnt_tile = cnt[0] + cnt[1]

    chunks = jnp.floor((cnt_tile + (RUN_CHUNK - 1)) * (1.0 / RUN_CHUNK))
    lane8 = lax.broadcasted_iota(jnp.int32, (TAB_ROWS, LANES), 1)
    incl = jnp.broadcast_to(chunks, (TAB_ROWS, LANES))
    shift = 1
    while shift < LANES:
        incl = incl + jnp.where(lane8 >= shift, pltpu.roll(incl, shift, 1), 0.0)
        shift *= 2
    offs = (incl[0:1] - chunks) * RUN_CHUNK

    locs = []
    for k in range(2):
        before = _dot(lower, onehot[k].astype(BF16))
        first = offs if k == 0 else offs + cnt[0]
        locs.append(jnp.sum(jnp.where(onehot[k], before + first, 0.0), axis=-1, keepdims=True))
    two = lax.broadcasted_iota(jnp.int32, loc_ref.shape, 1)
    loc_ref[...] = jnp.where(two == 0, locs[0], locs[1]).astype(jnp.int32)

    row8 = lax.broadcasted_iota(jnp.int32, (TAB_ROWS, LANES), 0)
    base = carry[...]
    tab = jnp.where(row8 == 0, cnt_tile, jnp.where(row8 == 1, offs, jnp.where(row8 == 2, base, 0.0)))
    tab_ref[...] = tab.astype(jnp.int32)
    carry[...] = base + cnt_tile


def _rank(eid, starts):
    N = eid.shape[0]
    n_tiles = N // TOK_TILE
    return pl.pallas_call(
        _rank_kernel,
        grid=(n_tiles,),
        in_specs=[pl.BlockSpec((TOK_TILE, 2), lambda i: (i, 0)), pl.BlockSpec((1, LANES), lambda i: (0, 0))],
        out_specs=[pl.BlockSpec((TOK_TILE, 2), lambda i: (i, 0)), pl.BlockSpec((TAB_ROWS, LANES), lambda i: (i, 0))],
        out_shape=[jax.ShapeDtypeStruct((N, 2), jnp.int32),
                   jax.ShapeDtypeStruct((n_tiles * TAB_ROWS, LANES), jnp.int32)],
        scratch_shapes=[pltpu.VMEM((1, LANES), F32)],
        compiler_params=_cparams(("arbitrary",)),
        name="rank",
    )(eid, starts)


SORT_ROWS = 2 * TOK_TILE + N_EXPERTS * RUN_CHUNK


def _run_chunks(tab_ref, tile, e):
    cnt = tab_ref[(tile * 3 + 0) * N_EXPERTS + e]
    offs = tab_ref[(tile * 3 + 1) * N_EXPERTS + e]
    base = tab_ref[(tile * 3 + 2) * N_EXPERTS + e]
    return lax.shift_right_logical(cnt + (RUN_CHUNK - 1), RUN_CHUNK.bit_length() - 1), offs, base


def _rows(ref, row, n):
    return ref.at[pl.ds(pl.multiple_of(row * ROW_TILES, ROW_TILES), n * ROW_TILES)]


def _for_each_chunk(tab_ref, tile, fn):
    for e in range(N_EXPERTS):
        chunks, offs, base = _run_chunks(tab_ref, tile, e)

        def one(k, carry):
            fn(offs + k * RUN_CHUNK, base + k * RUN_CHUNK)
            return carry

        lax.fori_loop(0, chunks, one, 0)


def _drain_chunks(tab_ref, tile, wait_one):
    total = 0
    for e in range(N_EXPERTS):
        total = total + _run_chunks(tab_ref, tile, e)[0]

    def one(k, carry):
        wait_one()
        return carry

    lax.fori_loop(0, total, one, 0)


def _dispatch_kernel(tab_ref, ztab_ref, h_ref, loc_ref, rows_hbm, sbuf, zbuf, sems, zsem):
    i = pl.program_id(0)
    n_tiles = pl.num_programs(0)
    slot = lax.rem(i, 2)
    T = TOK_TILE

    def run_copy(s, offs_row, base_row):
        return pltpu.make_async_copy(_rows(sbuf, s * SORT_ROWS + offs_row, RUN_CHUNK),
                                     _rows(rows_hbm, base_row, RUN_CHUNK), sems.at[s])

    @pl.when(i == 0)
    def _():
        zbuf[...] = jnp.zeros(zbuf.shape, F32)

        def zero_chunk(first_row):
            return pltpu.make_async_copy(zbuf, _rows(rows_hbm, first_row, RUN_CHUNK), zsem)

        def done(k, carry):
            zero_chunk(0).wait()
            return carry

        for e in range(N_EXPERTS):
            zero_chunk(ztab_ref[e]).start()
        lax.fori_loop(0, N_EXPERTS, done, 0)
        total = 0
        for e in range(N_EXPERTS):
            start, chunks = ztab_ref[N_EXPERTS + e], ztab_ref[2 * N_EXPERTS + e]
            total = total + chunks

            def fill(k, carry, start=start):
                zero_chunk(start + k * RUN_CHUNK).start()
                return carry

            lax.fori_loop(0, chunks, fill, 0)
        lax.fori_loop(0, total, done, 0)

    @pl.when(i > 0)
    def _():
        _drain_chunks(tab_ref, i - 1, lambda: run_copy(1 - slot, 0, 0).wait())

    locf = loc_ref[...].astype(F32)
    loc_lanes = [jnp.broadcast_to(locf[:, k:k + 1], (T, LANES)).T[0:1, :] for k in range(2)]
    pos = lax.broadcasted_iota(jnp.int32, (SORT_ROWS, T), 0).astype(F32)
    perm = jnp.where((pos == loc_lanes[0]) | (pos == loc_lanes[1]), 1.0, 0.0).astype(BF16)
    sorted_rows = _dot(perm, h_ref[...])
    for c in range(ROW_TILES):
        sbuf[pl.ds(slot * (SORT_ROWS * ROW_TILES) + c, SORT_ROWS, stride=ROW_TILES), :] = (
            sorted_rows[:, c * LANES:(c + 1) * LANES])

    _for_each_chunk(tab_ref, i, lambda offs_row, base_row: run_copy(slot, offs_row, base_row).start())

    @pl.when(i == n_tiles - 1)
    def _():
        _drain_chunks(tab_ref, i, lambda: run_copy(slot, 0, 0).wait())


def _dispatch(tabs, ztab, h2, loc, n_rows):
    N = h2.shape[0]
    return pl.pallas_call(
        _dispatch_kernel,
        grid_spec=pltpu.PrefetchScalarGridSpec(
            num_scalar_prefetch=2,
            grid=(N // TOK_TILE,),
            in_specs=[pl.BlockSpec((TOK_TILE, D_MODEL), lambda i, t, z: (i, 0)),
                      pl.BlockSpec((TOK_TILE, 2), lambda i, t, z: (i, 0))],
            out_specs=pl.BlockSpec(memory_space=pl.ANY),
            scratch_shapes=[pltpu.VMEM((2 * SORT_ROWS * ROW_TILES, LANES), F32),
                            pltpu.VMEM((RUN_CHUNK * ROW_TILES, LANES), F32),
                            pltpu.SemaphoreType.DMA((2,)), pltpu.SemaphoreType.DMA(())],
        ),
        out_shape=jax.ShapeDtypeStruct((n_rows * ROW_TILES, LANES), F32),
        compiler_params=_cparams(("arbitrary",)),
        name="dispatch",
    )(tabs, ztab, h2, loc)


def _expert_kernel(be_ref, nused_ref, rows_ref, w1_ref, w3_ref, w2_ref, y_ref):
    del be_ref

    @pl.when(pl.program_id(0) < nused_ref[0])
    def _():
        h = jnp.concatenate([rows_ref[pl.ds(c, ROW_BLOCK, stride=ROW_TILES), :] for c in range(ROW_TILES)],
                            axis=1).astype(BF16)
        a = _dot(h, w1_ref[0].astype(BF16))
        b = _dot(h, w3_ref[0].astype(BF16))
        z = (a * jax.nn.sigmoid(a) * b).astype(BF16)
        y = _dot(z, w2_ref[0].astype(BF16))
        for c in range(ROW_TILES):
            y_ref[pl.ds(c, ROW_BLOCK, stride=ROW_TILES), :] = y[:, c * LANES:(c + 1) * LANES]

    @pl.when(pl.program_id(0) >= nused_ref[0])
    def _():
        y_ref[...] = jnp.zeros(y_ref.shape, y_ref.dtype)


def _experts(blk_expert, n_used, rows, w1, w3, w2):
    n_blk = blk_expert.shape[0]
    row_blk = lambda i, be, nu: (jnp.minimum(i, nu[0] - 1), 0)
    wspec = lambda shape: pl.BlockSpec((1,) + shape, lambda i, be, nu: (be[i], 0, 0))
    return pl.pallas_call(
        _expert_kernel,
        grid_spec=pltpu.PrefetchScalarGridSpec(
            num_scalar_prefetch=2,
            grid=(n_blk,),
            in_specs=[pl.BlockSpec((ROW_BLOCK * ROW_TILES, LANES), row_blk),
                      wspec((D_MODEL, D_EXPERT)), wspec((D_MODEL, D_EXPERT)), wspec((D_EXPERT, D_MODEL))],
            out_specs=pl.BlockSpec((ROW_BLOCK * ROW_TILES, LANES), lambda i, be, nu: (i, 0)),
        ),
        out_shape=jax.ShapeDtypeStruct((n_blk * ROW_BLOCK * ROW_TILES, LANES), F32),
        compiler_params=_cparams(("arbitrary",)),
        name="experts",
    )(blk_expert, n_used, rows, w1, w3, w2)


def _combine_kernel(tab_ref, x1_ref, gate_ref, loc_ref, y_hbm, o_ref, ybuf, sems):
    i = pl.program_id(0)
    n_tiles = pl.num_programs(0)
    slot = lax.rem(i, 2)
    T = TOK_TILE

    def run_copy(s, offs_row, base_row):
        return pltpu.make_async_copy(_rows(y_hbm, base_row, RUN_CHUNK),
                                     _rows(ybuf, s * SORT_ROWS + offs_row, RUN_CHUNK), sems.at[s])

    def fetch(tile, s):
        _for_each_chunk(tab_ref, tile, lambda offs_row, base_row: run_copy(s, offs_row, base_row).start())

    @pl.when(i == 0)
    def _():
        ybuf[...] = jnp.zeros(ybuf.shape, F32)
        fetch(0, 0)

    @pl.when(i + 1 < n_tiles)
    def _():
        fetch(i + 1, 1 - slot)

    _drain_chunks(tab_ref, i, lambda: run_copy(slot, 0, 0).wait())

    y_sorted = jnp.concatenate([ybuf[pl.ds(slot * (SORT_ROWS * ROW_TILES) + c, SORT_ROWS, stride=ROW_TILES), :]
                                for c in range(ROW_TILES)], axis=1).astype(BF16)
    gate = gate_ref[...]
    loc = loc_ref[...]
    pos = lax.broadcasted_iota(jnp.int32, (T, SORT_ROWS), 1)
    g = (jnp.where(pos == loc[:, 0:1], gate[:, 0:1], 0.0) + jnp.where(pos == loc[:, 1:2], gate[:, 1:2], 0.0))
    g_hi, g_lo = _split_bf16(g)
    o_ref[...] = x1_ref[...] + _dot(g_hi, y_sorted) + _dot(g_lo, y_sorted)


def _combine(tabs, x1, gate, loc, y_rows):
    N = gate.shape[0]
    tok = lambda width: pl.BlockSpec((TOK_TILE, width), lambda i, t: (i, 0))
    return pl.pallas_call(
        _combine_kernel,
        grid_spec=pltpu.PrefetchScalarGridSpec(
            num_scalar_prefetch=1,
            grid=(N // TOK_TILE,),
            in_specs=[tok(D_MODEL), tok(2), tok(2), pl.BlockSpec(memory_space=pl.ANY)],
            out_specs=tok(D_MODEL),
            scratch_shapes=[pltpu.VMEM((2 * SORT_ROWS * ROW_TILES, LANES), F32), pltpu.SemaphoreType.DMA((2,))],
        ),
        out_shape=jax.ShapeDtypeStruct((N, D_MODEL), F32),
        compiler_params=_cparams(("arbitrary",)),
        name="combine",
    )(tabs, x1, gate, loc, y_rows)


def _block_diag(w):
    n, a, b = w.shape
    eye = np.eye(n, dtype=np.float32)
    return (w[:, :, None, :] * eye[:, None, :, None]).reshape(n * a, n * b)


def _pad_heads(v, used):
    lead = v.shape[:-1]
    v = v.reshape(lead + (MLA_HEADS, used))
    v = jnp.pad(v, [(0, 0)] * len(lead) + [(0, 0), (0, MLA_HEAD_PAD - used)])
    return v.reshape(lead + (MLA_QK_PAD,))


def _rotary_lane_tables(seq):
    half = MLA_ROPE_DIM // 2
    inv_freq = ROPE_THETA ** (-jnp.arange(half, dtype=F32) / half)
    ang = jnp.arange(seq, dtype=F32)[:, None] * inv_freq[None, :]
    cos, sin = jnp.cos(ang), jnp.sin(ang)
    ones = jnp.ones((seq, MLA_NOPE_DIM), F32)
    tail = MLA_HEAD_PAD - MLA_QK_DIM
    cos_t = jnp.concatenate([ones, cos, cos, jnp.ones((seq, tail), F32)], axis=1)
    sin_t = jnp.concatenate([0.0 * ones, -sin, sin, jnp.zeros((seq, tail), F32)], axis=1)
    return cos_t, sin_t


def _pick_tile(n, pref):
    t = min(n, pref)
    while n % t:
        t //= 2
    return t


def kernel(x, norm1_g, w_in, lru_conv_w, lru_conv_b, lru_wa, lru_ba, lru_wx, lru_bx, lru_lambda, lru_out_g,
           diff_q_g, diff_k_g, diff_lq1, diff_lk1, diff_lq2, diff_lk2, diff_sub_g,
           mla_cq_g, mla_ckv_g, mla_w_uq, mla_w_ukv, mla_q_g, mla_k_g, mla_out_g, w_out, norm2_g,
           router_g_w, router_g_b, router_e_w, router_e_b, exp_w1, exp_w3, exp_w2):
    B, S, D = x.shape
    N = B * S
    depth = w_in.shape[0]
    assert S % KEY_TILE == 0 and D == D_MODEL
    ts = _pick_tile(S, 512)
    tc = _pick_tile(S, 512)
    assert N % TOK_TILE == 0
    tm = _pick_tile(N, 512)

    cos_t, sin_t = _rotary_lane_tables(S)
    cos_q, sin_q = cos_t.T, sin_t.T
    per_row = lambda v: jnp.broadcast_to(v.astype(F32)[:, None], (v.shape[0], LANES))
    n_dgroups = DIFF_QK_WIDTH // DIFF_HEAD_DIM
    gsum = jnp.asarray(np.kron(np.eye(n_dgroups), np.ones((DIFF_HEAD_DIM, DIFF_HEAD_DIM))), BF16)
    slopes = np.asarray([2.0 ** (-ALIBI_MAX_BIAS * (h + 1) / DIFF_HEADS) * LOG2E for h in range(DIFF_HEADS)], np.float32)
    slopes = jnp.asarray(np.pad(slopes.reshape(DIFF_HEADS // 2, 1, 2), ((0, 0), (0, 0), (0, LANES - 2))))
    place = np.zeros((MLA_ROPE_DIM, MLA_HEADS, MLA_HEAD_PAD), np.float32)
    place[np.arange(MLA_ROPE_DIM), :, MLA_NOPE_DIM + np.arange(MLA_ROPE_DIM)] = 1.0
    place = jnp.asarray(place.reshape(MLA_ROPE_DIM, MLA_QK_PAD))
    row = lambda v: v.reshape(1, -1).astype(F32)

    n_blk = -(-(2 * N + N_EXPERTS * (RUN_CHUNK + ROW_BLOCK)) // ROW_BLOCK) + 1
    n_rows = n_blk * ROW_BLOCK

    for l in range(depth):
        dqg = per_row(jnp.tile(diff_q_g[l], n_dgroups) * (DIFF_HEAD_DIM ** -0.5 * LOG2E))
        dkg = row(jnp.tile(diff_k_g[l], n_dgroups))
        wuq = _pad_heads(mla_w_uq[l], MLA_QK_DIM).astype(BF16)
        mqg = per_row(_pad_heads(jnp.tile(mla_q_g[l], MLA_HEADS) * (MLA_QK_DIM ** -0.5 * LOG2E), MLA_QK_DIM))
        mkg = row(_pad_heads(jnp.tile(mla_k_g[l], MLA_HEADS), MLA_QK_DIM))
        wukv = mla_w_ukv[l].reshape(MLA_KV_RANK, MLA_HEADS, MLA_NOPE_DIM + MLA_V_DIM)
        wk_nope = _pad_heads(wukv[:, :, :MLA_NOPE_DIM].reshape(MLA_KV_RANK, -1), MLA_NOPE_DIM)
        wk = jnp.concatenate([wk_nope, place], axis=0).astype(BF16)
        wv = wukv[:, :, MLA_NOPE_DIM:].reshape(MLA_KV_RANK, MLA_WIDTH).astype(BF16)

        lru_in, dq, dk, dv, mq, mk, mv = _in_proj(
            x, row(norm1_g[l]), w_in, l, gsum, dqg, dkg, row(mla_cq_g[l]), wuq, mqg, row(mla_ckv_g[l]), wk, mkg, wv,
            cos_t, sin_t, cos_q, sin_q, ts)

        y_lru = _lru(lru_in, lru_conv_w[l], row(lru_conv_b[l]), _block_diag(lru_wa[l]).astype(BF16),
                     row(lru_ba[l]), _block_diag(lru_wx[l]).astype(BF16), row(lru_bx[l]), row(lru_lambda[l]),
                     row(lru_out_g[l]), tc)

        lambda_init = 0.8 - 0.6 * math.exp(-0.3 * l)
        y_diff = _diff_attn(dq, dk, dv, slopes, row(diff_lq1[l]), row(diff_lk1[l]), row(diff_lq2[l]),
                            row(diff_lk2[l]), row(jnp.tile(diff_sub_g[l], 2)), lambda_init)
        y_mla = _mla_attn(mq, mk, mv)

        wr = jnp.concatenate([router_g_w[l], router_e_w[l],
                              jnp.zeros((D, ROUTER_PAD - N_GROUPS - N_EXPERTS), F32)], axis=1)
        wrh = wr.astype(BF16)
        wrl = (wr - wrh.astype(F32)).astype(BF16)
        rb = jnp.pad(jnp.concatenate([router_g_b[l], router_e_b[l]]), (0, ROUTER_PAD - N_GROUPS - N_EXPERTS))
        x1, h2, eid, gate, cnt = _out_proj(
            x.reshape(N, D), y_lru.reshape(N, LRU_WIDTH), y_diff.reshape(N, DIFF_WIDTH), y_mla.reshape(N, MLA_WIDTH),
            row(mla_out_g[l]), w_out[l].astype(BF16), row(norm2_g[l]), wrh, wrl, row(rb), tm)

        counts = cnt[0, N_GROUPS:N_GROUPS + N_EXPERTS].astype(jnp.int32)
        padded = (counts + RUN_CHUNK + ROW_BLOCK - 1) // ROW_BLOCK * ROW_BLOCK
        pad_ends = jnp.cumsum(padded)
        pad_starts = pad_ends - padded
        blk_row0 = jnp.arange(n_blk, dtype=jnp.int32) * ROW_BLOCK
        blk_expert = jnp.minimum(jnp.sum((pad_ends[None, :] <= blk_row0[:, None]).astype(jnp.int32), axis=1),
                                 N_EXPERTS - 1)
        n_used = (pad_ends[-1:] // ROW_BLOCK).astype(jnp.int32)
        starts = jnp.pad(pad_starts.astype(F32), (0, LANES - N_EXPERTS)).reshape(1, LANES)
        tail_start = pad_starts + counts
        last = jnp.arange(N_EXPERTS) == N_EXPERTS - 1
        tail_ends = jnp.where(last, n_rows, pad_ends)
        aligned_start = (tail_start + RUN_CHUNK - 1) // RUN_CHUNK * RUN_CHUNK
        ztab = jnp.concatenate([tail_start, aligned_start, (tail_ends - aligned_start) // RUN_CHUNK]).astype(jnp.int32)

        loc, tab = _rank(eid, starts)
        tabs = tab.reshape(N // TOK_TILE, TAB_ROWS, LANES)[:, :3, :N_EXPERTS].reshape(-1)
        rows = _dispatch(tabs, ztab, h2, loc, n_rows)
        y_rows = _experts(blk_expert + l * N_EXPERTS, n_used, rows, exp_w1.reshape(-1, D, D_EXPERT),
                          exp_w3.reshape(-1, D, D_EXPERT), exp_w2.reshape(-1, D_EXPERT, D))
        x = _combine(tabs, x1, gate, loc, y_rows).reshape(B, S, D)
    return x
```

```python
import functools
import math

import jax
import jax.numpy as jnp
import numpy as np
from jax import lax
from jax.experimental import pallas as pl
from jax.experimental.pallas import tpu as pltpu

F32 = jnp.float32
BF16 = jnp.bfloat16

D_MODEL = 1024
CHUNK = 64
LRU_WIDTH = 256
CONV_WIDTH = 4
RG_C = 8.0
DIFF_HEADS = 6
DIFF_HEAD_DIM = 32
DIFF_V_DIM = 64
DIFF_QK_WIDTH = DIFF_HEADS * 2 * DIFF_HEAD_DIM
DIFF_WIDTH = DIFF_HEADS * DIFF_V_DIM
ALIBI_MAX_BIAS = 8.0
MLA_HEADS = 6
MLA_Q_RANK = 192
MLA_KV_RANK = 128
MLA_NOPE_DIM = 64
MLA_ROPE_DIM = 32
MLA_V_DIM = 64
MLA_QK_DIM = MLA_NOPE_DIM + MLA_ROPE_DIM
MLA_WIDTH = MLA_HEADS * MLA_V_DIM
ROPE_THETA = 10000.0
N_GROUPS = 4
EXPERTS_PER_GROUP = 8
N_EXPERTS = N_GROUPS * EXPERTS_PER_GROUP
D_EXPERT = 256
ROW_BLOCK = 256
EPS = 1e-6

LANES = 128
ROW_TILES = D_MODEL // LANES
MLA_HEAD_PAD = LANES
MLA_QK_PAD = MLA_HEADS * MLA_HEAD_PAD
VMEM_LIMIT = 56 * 1024 * 1024
NEG_BIG = -1e30
KEY_TILE = 256
TILES_PER_GROUP = 4
LOG2E = math.log2(math.e)

C_LRU = 0
C_DQ = 512
C_DK = 896
C_DV = 1280
C_CQ = 1664
C_CKV = 1856
C_KR = 1984
D_IN = 2016
D_IN_PAD = 2048


def _cparams(sem):
    return pltpu.CompilerParams(dimension_semantics=sem, vmem_limit_bytes=VMEM_LIMIT)


def _rms(v, width):
    return v * lax.rsqrt(jnp.sum(v * v, axis=-1, keepdims=True) * (1.0 / width) + EPS)


def _dot(a, b):
    return jnp.dot(a, b, preferred_element_type=F32)


def _split_bf16(v):
    hi = v.astype(BF16)
    lo = (v - hi.astype(F32)).astype(BF16)
    return hi, lo


def _in_proj_kernel(x_ref, g1_ref, w_ref, gsum_ref, dqg_ref, dkg_ref, cqg_ref, wuq_ref, mqg_ref, ckvg_ref,
                    wk_ref, mkg_ref, wv_ref, cos_ref, sin_ref, cosq_ref, sinq_ref,
                    lru_ref, dqt_ref, dk_ref, dvt_ref, mqt_ref, mk_ref, mvt_ref, w_bf):
    @pl.when((pl.program_id(0) == 0) & (pl.program_id(1) == 0))
    def _():
        w_bf[:, 0:D_IN] = w_ref[0].astype(BF16)
        w_bf[:, D_IN:] = jnp.zeros((D_MODEL, D_IN_PAD - D_IN), BF16)

    x = x_ref[0]
    hn = (_rms(x, D_MODEL) * g1_ref[...]).astype(BF16)
    p = _dot(hn, w_bf[...])
    lru_ref[0] = p[:, C_LRU:C_LRU + 2 * LRU_WIDTH]

    gsum = gsum_ref[...]

    def group_norm32(v, gain):
        hi, lo = _split_bf16(v * v)
        ss = _dot(hi, gsum) + _dot(lo, gsum)
        return v * lax.rsqrt(ss * (1.0 / DIFF_HEAD_DIM) + EPS) * gain

    def store_tiles(ref, vt):
        for c in range(vt.shape[1] // KEY_TILE):
            ref[0, c] = vt[:, c * KEY_TILE:(c + 1) * KEY_TILE].astype(BF16)

    t = x.shape[0]

    def along_lanes(per_feature):
        return jnp.concatenate([per_feature] * (t // LANES), axis=1)

    dq_t = p[:, C_DQ:C_DQ + DIFF_QK_WIDTH].T.reshape(DIFF_QK_WIDTH // DIFF_HEAD_DIM, DIFF_HEAD_DIM, t)
    dq_ss = jnp.sum(dq_t * dq_t, axis=1, keepdims=True) * (1.0 / DIFF_HEAD_DIM)
    dq_n = (dq_t * lax.rsqrt(dq_ss + EPS)).reshape(DIFF_QK_WIDTH, t) * along_lanes(dqg_ref[...])
    dqt_ref[0] = dq_n.astype(BF16)
    dk_ref[0] = group_norm32(p[:, C_DK:C_DK + DIFF_QK_WIDTH], dkg_ref[...]).astype(BF16)
    store_tiles(dvt_ref, p[:, C_DV:C_DV + DIFF_WIDTH].T)

    cos = cos_ref[...]
    sin = sin_ref[...]
    lane = lax.broadcasted_iota(jnp.int32, cos.shape, 1)
    first_half = lane < MLA_NOPE_DIM + MLA_ROPE_DIM // 2

    def head_norm_rotary(v, gain):
        outs = []
        for h in range(MLA_HEADS):
            c = v[:, h * MLA_HEAD_PAD:(h + 1) * MLA_HEAD_PAD]
            c = _rms(c, MLA_QK_DIM) * gain[:, h * MLA_HEAD_PAD:(h + 1) * MLA_HEAD_PAD]
            swapped = jnp.where(first_half, pltpu.roll(c, MLA_HEAD_PAD - MLA_ROPE_DIM // 2, 1),
                                pltpu.roll(c, MLA_ROPE_DIM // 2, 1))
            outs.append(c * cos + swapped * sin)
        return jnp.concatenate(outs, axis=1)

    cq = (_rms(p[:, C_CQ:C_CQ + MLA_Q_RANK], MLA_Q_RANK) * cqg_ref[...]).astype(BF16)
    q_t = _dot(cq, wuq_ref[...]).T
    cos_q, sin_q = cosq_ref[...], sinq_ref[...]
    gain_q = along_lanes(mqg_ref[...])
    half = MLA_ROPE_DIM // 2
    heads_t = []
    for h in range(MLA_HEADS):
        rows = slice(h * MLA_HEAD_PAD, (h + 1) * MLA_HEAD_PAD)
        c = q_t[rows]
        ss = jnp.sum(c * c, axis=0, keepdims=True) * (1.0 / MLA_QK_DIM)
        c = c * lax.rsqrt(ss + EPS) * gain_q[rows]
        swapped = jnp.concatenate([c[:MLA_NOPE_DIM], c[MLA_NOPE_DIM + half:MLA_QK_DIM],
                                   c[MLA_NOPE_DIM:MLA_NOPE_DIM + half], c[MLA_QK_DIM:]], axis=0)
        heads_t.append((c * cos_q + swapped * sin_q).astype(BF16))
    mqt_ref[0] = jnp.concatenate(heads_t, axis=0)

    ckv = (_rms(p[:, C_CKV:C_CKV + MLA_KV_RANK], MLA_KV_RANK) * ckvg_ref[...]).astype(BF16)
    kcat = jnp.concatenate([ckv, p[:, C_KR:C_KR + MLA_ROPE_DIM].astype(BF16)], axis=1)
    mk_ref[0] = head_norm_rotary(_dot(kcat, wk_ref[...]), mkg_ref[...]).astype(BF16)
    store_tiles(mvt_ref, _dot(ckv, wv_ref[...]).T)


def _in_proj(x, g1, w_all, layer, gsum, dqg, dkg, cqg, wuq, mqg, ckvg, wk, mkg, wv, cos_t, sin_t, cos_q, sin_q, ts):
    B, S, _ = x.shape
    full = lambda a: pl.BlockSpec(a.shape, lambda b, i: (0,) * a.ndim)
    w_spec = pl.BlockSpec((1,) + w_all.shape[1:], lambda b, i: (layer, 0, 0))
    tok = lambda width: pl.BlockSpec((1, ts, width), lambda b, i: (b, i, 0))
    tok_t = lambda width: pl.BlockSpec((1, width, ts), lambda b, i: (b, 0, i))
    nkt = ts // KEY_TILE
    tiles_t = lambda width: pl.BlockSpec((1, nkt, width, KEY_TILE), lambda b, i: (b, i, 0, 0))
    rot = pl.BlockSpec((ts, LANES), lambda b, i: (i, 0))
    rot_t = pl.BlockSpec((LANES, ts), lambda b, i: (0, i))
    shp = lambda shape, dt: jax.ShapeDtypeStruct(shape, dt)
    return pl.pallas_call(
        _in_proj_kernel,
        grid=(B, S // ts),
        in_specs=[tok(D_MODEL), full(g1), w_spec, full(gsum), full(dqg), full(dkg), full(cqg), full(wuq),
                  full(mqg), full(ckvg), full(wk), full(mkg), full(wv), rot, rot, rot_t, rot_t],
        out_specs=[tok(2 * LRU_WIDTH), tok_t(DIFF_QK_WIDTH), tok(DIFF_QK_WIDTH), tiles_t(DIFF_WIDTH),
                   tok_t(MLA_QK_PAD), tok(MLA_QK_PAD), tiles_t(MLA_WIDTH)],
        out_shape=[shp((B, S, 2 * LRU_WIDTH), F32), shp((B, DIFF_QK_WIDTH, S), BF16), shp((B, S, DIFF_QK_WIDTH), BF16),
                   shp((B, S // KEY_TILE, DIFF_WIDTH, KEY_TILE), BF16), shp((B, MLA_QK_PAD, S), BF16),
                   shp((B, S, MLA_QK_PAD), BF16), shp((B, S // KEY_TILE, MLA_WIDTH, KEY_TILE), BF16)],
        scratch_shapes=[pltpu.VMEM((D_MODEL, D_IN_PAD), BF16)],
        compiler_params=_cparams(("arbitrary", "arbitrary")),
        name="in_proj",
    )(x, g1, w_all, gsum, dqg, dkg, cqg, wuq, mqg, ckvg, wk, mkg, wv, cos_t, sin_t, cos_q, sin_q)


LRU_HALO = 8


def _lru_kernel(blk_ref, cw_ref, cb_ref, wa_ref, ba_ref, wx_ref, bx_ref, lam_ref, og_ref, y_ref,
                ubuf, a_scr, b_scr, h_scr):
    B, tc, _ = blk_ref.shape
    W = LRU_WIDTH

    @pl.when(pl.program_id(0) == 0)
    def _():
        ubuf[:, 0:LRU_HALO, :] = jnp.zeros((B, LRU_HALO, W), F32)
        h_scr[...] = jnp.zeros(h_scr.shape, F32)

    ubuf[:, LRU_HALO:, :] = blk_ref[:, :, 0:W]
    xc = cb_ref[...][None]
    for j in range(CONV_WIDTH):
        off = LRU_HALO - (CONV_WIDTH - 1) + j
        xc = xc + cw_ref[j:j + 1, :][None] * ubuf[:, off:off + tc, :]
    ubuf[:, 0:LRU_HALO, :] = ubuf[:, tc:tc + LRU_HALO, :]

    x2 = xc.reshape(B * tc, W)
    xb = x2.astype(BF16)
    r = jax.nn.sigmoid(_dot(xb, wa_ref[...]) + ba_ref[...])
    gi = jax.nn.sigmoid(_dot(xb, wx_ref[...]) + bx_ref[...])
    nl = -lam_ref[...]
    softplus = jnp.maximum(nl, 0.0) + jnp.log(1.0 + jnp.exp(-jnp.abs(nl)))
    a = jnp.exp(-RG_C * r * softplus)
    bb = jnp.sqrt(1.0 - a * a) * gi * x2
    a_scr[...] = a.reshape(B, tc, W)
    b_scr[...] = bb.reshape(B, tc, W)

    def step(t, h):
        h = a_scr[:, pl.ds(t, 1), :] * h + b_scr[:, pl.ds(t, 1), :]
        b_scr[:, pl.ds(t, 1), :] = h
        return h

    h_scr[...] = lax.fori_loop(0, tc, step, h_scr[...], unroll=8)

    gate = blk_ref[:, :, W:2 * W]
    gelu = 0.5 * gate * (1.0 + jnp.tanh(math.sqrt(2.0 / math.pi) * (gate + 0.044715 * gate * gate * gate)))
    y = b_scr[...] * gelu
    y_ref[...] = (_rms(y, W) * og_ref[...][None]).astype(y_ref.dtype)


def _lru(lru_in, cw, cb, wa, ba, wx, bx, lam, og, tc):
    B, S, _ = lru_in.shape
    W = LRU_WIDTH
    full = lambda a: pl.BlockSpec(a.shape, lambda i: (0,) * a.ndim)
    return pl.pallas_call(
        _lru_kernel,
        grid=(S // tc,),
        in_specs=[pl.BlockSpec((B, tc, 2 * W), lambda i: (0, i, 0)), full(cw), full(cb), full(wa), full(ba),
                  full(wx), full(bx), full(lam), full(og)],
        out_specs=pl.BlockSpec((B, tc, W), lambda i: (0, i, 0)),
        out_shape=jax.ShapeDtypeStruct((B, S, W), BF16),
        scratch_shapes=[pltpu.VMEM((B, tc + LRU_HALO, W), F32), pltpu.VMEM((B, tc, W), F32),
                        pltpu.VMEM((B, tc, W), F32), pltpu.VMEM((B, 1, W), F32)],
        compiler_params=_cparams(("arbitrary",)),
        name="lru",
    )(lru_in, cw, cb, wa, ba, wx, bx, lam, og)


def _flash_maps(qt_list, k_ref, vt_ref, k_lanes, bias_of, tile_shift, m_scr, l_scr, acc_scr, tiles_per_step):
    qi = pl.program_id(2)
    n_maps = len(qt_list)
    tq = qt_list[0].shape[1]
    maps = range(n_maps)

    def biased_scores(j):
        jc = jnp.minimum(j, qi)
        start = pl.multiple_of(jc * KEY_TILE, KEY_TILE)
        is_own = (j == qi).astype(jnp.int32)
        return tuple(_dot(k_ref[0, pl.ds(start, KEY_TILE), k_lanes[i]], qt_list[i]) + bias_of(i, is_own)
                     for i in maps)

    for i in maps:
        m_scr[i] = jnp.full((1, tq), NEG_BIG, F32)
        l_scr[i] = jnp.zeros((1, tq), F32)
        acc_scr[i] = jnp.zeros(acc_scr.shape[1:], F32)

    def update(j, s_tile):
        vt = vt_ref[0, jnp.minimum(j, qi)]
        for i in maps:
            s = s_tile[i]
            c = tile_shift(i, qi - j)
            m_old = m_scr[i]
            m_new = jnp.maximum(m_old, jnp.max(s, axis=0, keepdims=True) + c)
            alpha = jnp.exp2(m_old - m_new)
            p = jnp.exp2(s - (m_new - c))
            m_scr[i] = m_new
            l_scr[i] = alpha * l_scr[i] + jnp.sum(p, axis=0, keepdims=True)
            acc_scr[i] = alpha * acc_scr[i] + _dot(vt, p.astype(BF16))

    def one_tile(j, s_tile):
        s_ahead = biased_scores(j + 1)
        update(j, s_tile)
        return s_ahead

    def group(t, s_tile):
        for u in range(tiles_per_step):
            s_tile = one_tile(tiles_per_step * t + u, s_tile)
        return s_tile

    n_groups = (qi + 1) // tiles_per_step
    s_rest = lax.fori_loop(0, n_groups, group, biased_scores(0))
    lax.fori_loop(n_groups * tiles_per_step, qi + 1, one_tile, s_rest)


def _chunk_allowed_t(tq):
    krow = lax.broadcasted_iota(jnp.int32, (KEY_TILE, tq), 0)
    qcol = lax.broadcasted_iota(jnp.int32, (KEY_TILE, tq), 1)
    return krow, qcol, (krow // CHUNK) <= (qcol // CHUNK)


def _diff_attn_kernel(qt_ref, k_ref, vt_ref, slope_ref, lq1_ref, lk1_ref, lq2_ref, lk2_ref, subg_ref, o_ref,
                      m_scr, l_scr, acc_scr, bias_scr, *, lambda_init, tq):
    qt = qt_ref[0]
    feat = lax.broadcasted_iota(jnp.int32, qt.shape, 0)
    zero = jnp.zeros_like(qt)
    qt_list = [jnp.where(feat // DIFF_HEAD_DIM == i, qt, zero) for i in range(4)]
    slope = [slope_ref[0, 0:1, hh:hh + 1] for hh in range(2)]

    @pl.when(pl.program_id(2) == 0)
    def _():
        krow, qcol, allowed = _chunk_allowed_t(tq)
        kf, qf = krow.astype(F32), qcol.astype(F32)
        self_dist = qf - jnp.abs(qf - kf)
        for hh in range(2):
            bias_scr[hh, 0] = slope[hh] * kf
            bias_scr[hh, 1] = jnp.where(allowed, slope[hh] * self_dist, NEG_BIG)

    def tile_shift(i, n_tiles):
        return -slope[i // 2] * (n_tiles * KEY_TILE).astype(F32)

    _flash_maps(qt_list, k_ref, vt_ref, [slice(None)] * 4, lambda i, is_own: bias_scr[i // 2, is_own],
                tile_shift, m_scr, l_scr, acc_scr, TILES_PER_GROUP)

    lam = (jnp.exp(jnp.sum(lq1_ref[...] * lk1_ref[...], axis=-1, keepdims=True))
           - jnp.exp(jnp.sum(lq2_ref[...] * lk2_ref[...], axis=-1, keepdims=True)) + lambda_init)
    orow = lax.broadcasted_iota(jnp.int32, (LANES, tq), 0)
    out_t = jnp.zeros((LANES, tq), F32)
    for hh in range(2):
        o = (acc_scr[2 * hh] * (1.0 / l_scr[2 * hh])
             - acc_scr[2 * hh + 1] * (lam / l_scr[2 * hh + 1]))
        mine = (orow // DIFF_V_DIM) == hh
        ms = jnp.sum(jnp.where(mine, o * o, 0.0), axis=0, keepdims=True) * (1.0 / DIFF_V_DIM)
        out_t = jnp.where(mine, o * lax.rsqrt(ms + EPS), out_t)
    o_ref[0] = (out_t.T * subg_ref[...] * (1.0 - lambda_init)).astype(o_ref.dtype)


def _diff_attn(dqt, dk, dvt, slopes, lq1, lk1, lq2, lk2, subg, lambda_init):
    B, S, _ = dk.shape
    tq = KEY_TILE
    n_pairs = DIFF_HEADS // 2
    full = lambda a: pl.BlockSpec(a.shape, lambda b, p, i: (0,) * a.ndim)
    kern = functools.partial(_diff_attn_kernel, lambda_init=lambda_init, tq=tq)
    return pl.pallas_call(
        kern,
        grid=(B, n_pairs, S // tq),
        in_specs=[pl.BlockSpec((1, LANES, tq), lambda b, p, i: (b, p, i)),
                  pl.BlockSpec((1, S, LANES), lambda b, p, i: (b, 0, p)),
                  pl.BlockSpec((1, S // KEY_TILE, LANES, KEY_TILE), lambda b, p, i: (b, 0, p, 0)),
                  pl.BlockSpec((1, 1, LANES), lambda b, p, i: (p, 0, 0)),
                  full(lq1), full(lk1), full(lq2), full(lk2), full(subg)],
        out_specs=pl.BlockSpec((1, tq, LANES), lambda b, p, i: (b, i, p)),
        out_shape=jax.ShapeDtypeStruct((B, S, DIFF_WIDTH), BF16),
        scratch_shapes=[pltpu.VMEM((4, 1, tq), F32), pltpu.VMEM((4, 1, tq), F32), pltpu.VMEM((4, LANES, tq), F32),
                        pltpu.VMEM((2, 2, KEY_TILE, tq), F32)],
        compiler_params=_cparams(("parallel", "parallel", "arbitrary")),
        name="diff_attn",
    )(dqt, dk, dvt, slopes, lq1, lk1, lq2, lk2, subg)


def _mla_attn_kernel(qt_ref, k_ref, vt_ref, o_ref, m_scr, l_scr, acc_scr, bias_scr, *, tq):
    qt = qt_ref[0]
    qt_list = [qt[hh * MLA_HEAD_PAD:(hh + 1) * MLA_HEAD_PAD, :] for hh in range(2)]
    k_lanes = [slice(hh * MLA_HEAD_PAD, (hh + 1) * MLA_HEAD_PAD) for hh in range(2)]

    @pl.when(pl.program_id(2) == 0)
    def _():
        _, _, allowed = _chunk_allowed_t(tq)
        bias_scr[0] = jnp.zeros((KEY_TILE, tq), F32)
        bias_scr[1] = jnp.where(allowed, 0.0, NEG_BIG)

    _flash_maps(qt_list, k_ref, vt_ref, k_lanes, lambda i, is_own: bias_scr[is_own], lambda i, n: 0.0,
                m_scr, l_scr, acc_scr, TILES_PER_GROUP)
    orow = lax.broadcasted_iota(jnp.int32, (LANES, tq), 0)
    out_t = jnp.where(orow < MLA_V_DIM, acc_scr[0] * (1.0 / l_scr[0]), acc_scr[1] * (1.0 / l_scr[1]))
    o_ref[0] = out_t.T.astype(o_ref.dtype)


def _mla_attn(mqt, mk, mvt):
    B, S, _ = mk.shape
    tq = KEY_TILE
    n_pairs = MLA_HEADS // 2
    kern = functools.partial(_mla_attn_kernel, tq=tq)
    return pl.pallas_call(
        kern,
        grid=(B, n_pairs, S // tq),
        in_specs=[pl.BlockSpec((1, 2 * MLA_HEAD_PAD, tq), lambda b, p, i: (b, p, i)),
                  pl.BlockSpec((1, S, 2 * MLA_HEAD_PAD), lambda b, p, i: (b, 0, p)),
                  pl.BlockSpec((1, S // KEY_TILE, LANES, KEY_TILE), lambda b, p, i: (b, 0, p, 0))],
        out_specs=pl.BlockSpec((1, tq, LANES), lambda b, p, i: (b, i, p)),
        out_shape=jax.ShapeDtypeStruct((B, S, MLA_WIDTH), F32),
        scratch_shapes=[pltpu.VMEM((2, 1, tq), F32), pltpu.VMEM((2, 1, tq), F32), pltpu.VMEM((2, LANES, tq), F32),
                        pltpu.VMEM((2, KEY_TILE, tq), F32)],
        compiler_params=_cparams(("parallel", "parallel", "arbitrary")),
        name="mla_attn",
    )(mqt, mk, mvt)


ROUTER_PAD = LANES


def _out_proj_kernel(x_ref, ylru_ref, ydiff_ref, ymla_ref, mlag_ref, wo_ref, g2_ref, wrh_ref, wrl_ref, rb_ref,
                     x1_ref, h2_ref, eid_ref, gate_ref, cnt_ref):
    @pl.when(pl.program_id(0) == 0)
    def _():
        cnt_ref[...] = jnp.zeros(cnt_ref.shape, F32)

    ymla = (_rms(ymla_ref[...], MLA_WIDTH) * mlag_ref[...]).astype(BF16)
    x1 = (x_ref[...]
          + _dot(ylru_ref[...], wo_ref[0:LRU_WIDTH, :])
          + _dot(ydiff_ref[...], wo_ref[LRU_WIDTH:LRU_WIDTH + DIFF_WIDTH, :])
          + _dot(ymla, wo_ref[LRU_WIDTH + DIFF_WIDTH:, :]))
    x1_ref[...] = x1

    h2 = _rms(x1, D_MODEL) * g2_ref[...]
    hi, lo = _split_bf16(h2)
    h2_ref[...] = hi
    wrh = wrh_ref[...]
    logits = _dot(hi, wrh) + _dot(hi, wrl_ref[...]) + _dot(lo, wrh) + rb_ref[...]

    lane = lax.broadcasted_iota(jnp.int32, logits.shape, 1)
    big = jnp.int32(1 << 20)
    is_group = lane < N_GROUPS
    gl = jnp.where(is_group, logits, NEG_BIG)
    gmax = jnp.max(gl, axis=-1, keepdims=True)
    g_idx = jnp.min(jnp.where(gl == gmax, lane, big), axis=-1, keepdims=True)
    pg_top = 1.0 / jnp.sum(jnp.where(is_group, jnp.exp(gl - gmax), 0.0), axis=-1, keepdims=True)

    e_lane = lane - N_GROUPS
    in_group = (e_lane >= 0) & (e_lane < N_EXPERTS) & ((e_lane // EXPERTS_PER_GROUP) == g_idx)
    el = jnp.where(in_group, logits, NEG_BIG)
    m1 = jnp.max(el, axis=-1, keepdims=True)
    i1 = jnp.min(jnp.where(el == m1, lane, big), axis=-1, keepdims=True)
    el2 = jnp.where(lane == i1, NEG_BIG, el)
    m2 = jnp.max(el2, axis=-1, keepdims=True)
    i2 = jnp.min(jnp.where(el2 == m2, lane, big), axis=-1, keepdims=True)
    e2 = jnp.exp(m2 - m1)
    g1 = pg_top / (1.0 + e2)
    g2 = pg_top * e2 / (1.0 + e2)

    two = lax.broadcasted_iota(jnp.int32, eid_ref.shape, 1)
    eid_ref[...] = jnp.where(two == 0, i1 - N_GROUPS, i2 - N_GROUPS)
    gate_ref[...] = jnp.where(two == 0, g1, g2)
    onehot = ((e_lane == i1 - N_GROUPS) | (e_lane == i2 - N_GROUPS)).astype(F32)
    cnt_ref[...] += jnp.sum(onehot, axis=0, keepdims=True)


def _out_proj(x, ylru, ydiff, ymla, mlag, wo, g2, wrh, wrl, rb, tm):
    N = x.shape[0]
    full = lambda a: pl.BlockSpec(a.shape, lambda i: (0,) * a.ndim)
    tok = lambda width: pl.BlockSpec((tm, width), lambda i: (i, 0))
    return pl.pallas_call(
        _out_proj_kernel,
        grid=(N // tm,),
        in_specs=[tok(D_MODEL), tok(LRU_WIDTH), tok(DIFF_WIDTH), tok(MLA_WIDTH), full(mlag), full(wo), full(g2),
                  full(wrh), full(wrl), full(rb)],
        out_specs=[tok(D_MODEL), tok(D_MODEL), tok(2), tok(2), pl.BlockSpec((1, ROUTER_PAD), lambda i: (0, 0))],
        out_shape=[jax.ShapeDtypeStruct((N, D_MODEL), F32), jax.ShapeDtypeStruct((N, D_MODEL), BF16),
                   jax.ShapeDtypeStruct((N, 2), jnp.int32),
                   jax.ShapeDtypeStruct((N, 2), F32), jax.ShapeDtypeStruct((1, ROUTER_PAD), F32)],
        compiler_params=_cparams(("arbitrary",)),
        name="out_proj",
    )(x, ylru, ydiff, ymla, mlag, wo, g2, wrh, wrl, rb)


TOK_TILE = 256
RUN_CHUNK = 8
TAB_ROWS = 8


def _rank_kernel(eid_ref, start_ref, loc_ref, tab_ref, carry):
    tm = eid_ref.shape[0]

    @pl.when(pl.program_id(0) == 0)
    def _():
        carry[...] = start_ref[...]

    lane = lax.broadcasted_iota(jnp.int32, (tm, LANES), 1)
    r = lax.broadcasted_iota(jnp.int32, (tm, tm), 0)
    c = lax.broadcasted_iota(jnp.int32, (tm, tm), 1)
    lower = (c < r).astype(BF16)
    eid = eid_ref[...]
    onehot = [lane == eid[:, k:k + 1] for k in range(2)]
    cnt = [jnp.sum(oh.astype(F32), axis=0, keepdims=True) for oh in onehot]
    cnt_tile = cnt[0] + cnt[1]

    chunks = jnp.floor((cnt_tile + (RUN_CHUNK - 1)) * (1.0 / RUN_CHUNK))
    lane8 = lax.broadcasted_iota(jnp.int32, (TAB_ROWS, LANES), 1)
    incl = jnp.broadcast_to(chunks, (TAB_ROWS, LANES))
    shift = 1
    while shift < LANES:
        incl = incl + jnp.where(lane8 >= shift, pltpu.roll(incl, shift, 1), 0.0)
        shift *= 2
    offs = (incl[0:1] - chunks) * RUN_CHUNK

    locs = []
    for k in range(2):
        before = _dot(lower, onehot[k].astype(BF16))
        first = offs if k == 0 else offs + cnt[0]
        locs.append(jnp.sum(jnp.where(onehot[k], before + first, 0.0), axis=-1, keepdims=True))
    two = lax.broadcasted_iota(jnp.int32, loc_ref.shape, 1)
    loc_ref[...] = jnp.where(two == 0, locs[0], locs[1]).astype(jnp.int32)

    row8 = lax.broadcasted_iota(jnp.int32, (TAB_ROWS, LANES), 0)
    base = carry[...]
    tab = jnp.where(row8 == 0, cnt_tile, jnp.where(row8 == 1, offs, jnp.where(row8 == 2, base, 0.0)))
    tab_ref[...] = tab.astype(jnp.int32)
    carry[...] = base + cnt_tile


def _rank(eid, starts):
    N = eid.shape[0]
    n_tiles = N // TOK_TILE
    return pl.pallas_call(
        _rank_kernel,
        grid=(n_tiles,),
        in_specs=[pl.BlockSpec((TOK_TILE, 2), lambda i: (i, 0)), pl.BlockSpec((1, LANES), lambda i: (0, 0))],
        out_specs=[pl.BlockSpec((TOK_TILE, 2), lambda i: (i, 0)), pl.BlockSpec((TAB_ROWS, LANES), lambda i: (i, 0))],
        out_shape=[jax.ShapeDtypeStruct((N, 2), jnp.int32),
                   jax.ShapeDtypeStruct((n_tiles * TAB_ROWS, LANES), jnp.int32)],
        scratch_shapes=[pltpu.VMEM((1, LANES), F32)],
        compiler_params=_cparams(("arbitrary",)),
        name="rank",
    )(eid, starts)


SORT_ROWS = 2 * TOK_TILE + N_EXPERTS * RUN_CHUNK


def _run_chunks(tab_ref, tile, e):
    cnt = tab_ref[(tile * 3 + 0) * N_EXPERTS + e]
    offs = tab_ref[(tile * 3 + 1) * N_EXPERTS + e]
    base = tab_ref[(tile * 3 + 2) * N_EXPERTS + e]
    return lax.shift_right_logical(cnt + (RUN_CHUNK - 1), RUN_CHUNK.bit_length() - 1), offs, base


def _rows(ref, row, n):
    return ref.at[pl.ds(pl.multiple_of(row * ROW_TILES, ROW_TILES), n * ROW_TILES)]


def _for_each_chunk(tab_ref, tile, fn):
    for e in range(N_EXPERTS):
        chunks, offs, base = _run_chunks(tab_ref, tile, e)

        def one(k, carry):
            fn(offs + k * RUN_CHUNK, base + k * RUN_CHUNK)
            return carry

        lax.fori_loop(0, chunks, one, 0)


def _drain_chunks(tab_ref, tile, wait_one):
    total = 0
    for e in range(N_EXPERTS):
        total = total + _run_chunks(tab_ref, tile, e)[0]

    def one(k, carry):
        wait_one()
        return carry

    lax.fori_loop(0, total, one, 0)


def _dispatch_kernel(tab_ref, ztab_ref, h_ref, loc_ref, rows_hbm, sbuf, zbuf, sems, zsem):
    i = pl.program_id(0)
    n_tiles = pl.num_programs(0)
    slot = lax.rem(i, 2)
    T = TOK_TILE

    def run_copy(s, offs_row, base_row):
        return pltpu.make_async_copy(_rows(sbuf, s * SORT_ROWS + offs_row, RUN_CHUNK),
                                     _rows(rows_hbm, base_row, RUN_CHUNK), sems.at[s])

    @pl.when(i == 0)
    def _():
        zbuf[...] = jnp.zeros(zbuf.shape, F32)

        def zero_chunk(first_row):
            return pltpu.make_async_copy(zbuf, _rows(rows_hbm, first_row, RUN_CHUNK), zsem)

        def done(k, carry):
            zero_chunk(0).wait()
            return carry

        for e in range(N_EXPERTS):
            zero_chunk(ztab_ref[e]).start()
        lax.fori_loop(0, N_EXPERTS, done, 0)
        total = 0
        for e in range(N_EXPERTS):
            start, chunks = ztab_ref[N_EXPERTS + e], ztab_ref[2 * N_EXPERTS + e]
            total = total + chunks

            def fill(k, carry, start=start):
                zero_chunk(start + k * RUN_CHUNK).start()
                return carry

            lax.fori_loop(0, chunks, fill, 0)
        lax.fori_loop(0, total, done, 0)

    locf = loc_ref[...].astype(F32)
    loc_lanes = [jnp.broadcast_to(locf[:, k:k + 1], (T, LANES)).T[0:1, :] for k in range(2)]
    pos = lax.broadcasted_iota(jnp.int32, (SORT_ROWS, T), 0).astype(F32)
    perm = jnp.where((pos == loc_lanes[0]) | (pos == loc_lanes[1]), 1.0, 0.0).astype(BF16)
    sorted_rows = _dot(perm, h_ref[...])
    for c in range(ROW_TILES):
        sbuf[pl.ds(slot * (SORT_ROWS * ROW_TILES) + c, SORT_ROWS, stride=ROW_TILES), :] = (
            sorted_rows[:, c * LANES:(c + 1) * LANES])

    @pl.when(i > 0)
    def _():
        _drain_chunks(tab_ref, i - 1, lambda: run_copy(1 - slot, 0, 0).wait())

    _for_each_chunk(tab_ref, i, lambda offs_row, base_row: run_copy(slot, offs_row, base_row).start())

    @pl.when(i == n_tiles - 1)
    def _():
        _drain_chunks(tab_ref, i, lambda: run_copy(slot, 0, 0).wait())


def _dispatch(tabs, ztab, h2, loc, n_rows):
    N = h2.shape[0]
    return pl.pallas_call(
        _dispatch_kernel,
        grid_spec=pltpu.PrefetchScalarGridSpec(
            num_scalar_prefetch=2,
            grid=(N // TOK_TILE,),
            in_specs=[pl.BlockSpec((TOK_TILE, D_MODEL), lambda i, t, z: (i, 0)),
                      pl.BlockSpec((TOK_TILE, 2), lambda i, t, z: (i, 0))],
            out_specs=pl.BlockSpec(memory_space=pl.ANY),
            scratch_shapes=[pltpu.VMEM((2 * SORT_ROWS * ROW_TILES, LANES), F32),
                            pltpu.VMEM((RUN_CHUNK * ROW_TILES, LANES), F32),
                            pltpu.SemaphoreType.DMA((2,)), pltpu.SemaphoreType.DMA(())],
        ),
        out_shape=jax.ShapeDtypeStruct((n_rows * ROW_TILES, LANES), F32),
        compiler_params=_cparams(("arbitrary",)),
        name="dispatch",
    )(tabs, ztab, h2, loc)


def _expert_kernel(be_ref, nused_ref, rows_ref, w1_ref, w3_ref, w2_ref, y_ref):
    del be_ref
    used = pl.program_id(0) < nused_ref[0]

    @pl.when(used)
    def _():
        h = jnp.concatenate([rows_ref[pl.ds(c, ROW_BLOCK, stride=ROW_TILES), :] for c in range(ROW_TILES)],
                            axis=1).astype(BF16)
        a = _dot(h, w1_ref[0].astype(BF16))
        b = _dot(h, w3_ref[0].astype(BF16))
        z = (a * jax.nn.sigmoid(a) * b).astype(BF16)
        y = _dot(z, w2_ref[0].astype(BF16))
        for c in range(ROW_TILES):
            y_ref[pl.ds(c, ROW_BLOCK, stride=ROW_TILES), :] = y[:, c * LANES:(c + 1) * LANES]

    @pl.when(jnp.logical_not(used))
    def _():
        y_ref[...] = jnp.zeros(y_ref.shape, y_ref.dtype)


def _experts(blk_expert, n_used, rows, w1, w3, w2):
    n_blk = blk_expert.shape[0]
    row_blk = lambda i, be, nu: (jnp.minimum(i, nu[0] - 1), 0)
    wspec = lambda shape: pl.BlockSpec((1,) + shape, lambda i, be, nu: (be[i], 0, 0))
    return pl.pallas_call(
        _expert_kernel,
        grid_spec=pltpu.PrefetchScalarGridSpec(
            num_scalar_prefetch=2,
            grid=(n_blk,),
            in_specs=[pl.BlockSpec((ROW_BLOCK * ROW_TILES, LANES), row_blk),
                      wspec((D_MODEL, D_EXPERT)), wspec((D_MODEL, D_EXPERT)), wspec((D_EXPERT, D_MODEL))],
            out_specs=pl.BlockSpec((ROW_BLOCK * ROW_TILES, LANES), lambda i, be, nu: (i, 0)),
        ),
        out_shape=jax.ShapeDtypeStruct((n_blk * ROW_BLOCK * ROW_TILES, LANES), F32),
        compiler_params=_cparams(("arbitrary",)),
        name="experts",
    )(blk_expert, n_used, rows, w1, w3, w2)


def _combine_kernel(tab_ref, x1_ref, gate_ref, loc_ref, y_hbm, o_ref, ybuf, sems):
    i = pl.program_id(0)
    n_tiles = pl.num_programs(0)
    slot = lax.rem(i, 2)
    T = TOK_TILE

    def run_copy(s, offs_row, base_row):
        return pltpu.make_async_copy(_rows(y_hbm, base_row, RUN_CHUNK),
                                     _rows(ybuf, s * SORT_ROWS + offs_row, RUN_CHUNK), sems.at[s])

    def fetch(tile, s):
        _for_each_chunk(tab_ref, tile, lambda offs_row, base_row: run_copy(s, offs_row, base_row).start())

    @pl.when(i == 0)
    def _():
        ybuf[...] = jnp.zeros(ybuf.shape, F32)
        fetch(0, 0)

    @pl.when(i + 1 < n_tiles)
    def _():
        fetch(i + 1, 1 - slot)

    _drain_chunks(tab_ref, i, lambda: run_copy(slot, 0, 0).wait())

    y_sorted = jnp.concatenate([ybuf[pl.ds(slot * (SORT_ROWS * ROW_TILES) + c, SORT_ROWS, stride=ROW_TILES), :]
                                for c in range(ROW_TILES)], axis=1).astype(BF16)
    gate = gate_ref[...]
    loc = loc_ref[...]
    pos = lax.broadcasted_iota(jnp.int32, (T, SORT_ROWS), 1)
    g = (jnp.where(pos == loc[:, 0:1], gate[:, 0:1], 0.0) + jnp.where(pos == loc[:, 1:2], gate[:, 1:2], 0.0))
    g_hi, g_lo = _split_bf16(g)
    o_ref[...] = x1_ref[...] + _dot(g_hi, y_sorted) + _dot(g_lo, y_sorted)


def _combine(tabs, x1, gate, loc, y_rows):
    N = gate.shape[0]
    tok = lambda width: pl.BlockSpec((TOK_TILE, width), lambda i, t: (i, 0))
    return pl.pallas_call(
        _combine_kernel,
        grid_spec=pltpu.PrefetchScalarGridSpec(
            num_scalar_prefetch=1,
            grid=(N // TOK_TILE,),
            in_specs=[tok(D_MODEL), tok(2), tok(2), pl.BlockSpec(memory_space=pl.ANY)],
            out_specs=tok(D_MODEL),
            scratch_shapes=[pltpu.VMEM((2 * SORT_ROWS * ROW_TILES, LANES), F32), pltpu.SemaphoreType.DMA((2,))],
        ),
        out_shape=jax.ShapeDtypeStruct((N, D_MODEL), F32),
        compiler_params=_cparams(("arbitrary",)),
        name="combine",
    )(tabs, x1, gate, loc, y_rows)


def _block_diag(w):
    n, a, b = w.shape
    eye = np.eye(n, dtype=np.float32)
    return (w[:, :, None, :] * eye[:, None, :, None]).reshape(n * a, n * b)


def _pad_heads(v, used):
    lead = v.shape[:-1]
    v = v.reshape(lead + (MLA_HEADS, used))
    v = jnp.pad(v, [(0, 0)] * len(lead) + [(0, 0), (0, MLA_HEAD_PAD - used)])
    return v.reshape(lead + (MLA_QK_PAD,))


def _rotary_lane_tables(seq):
    half = MLA_ROPE_DIM // 2
    inv_freq = ROPE_THETA ** (-jnp.arange(half, dtype=F32) / half)
    ang = jnp.arange(seq, dtype=F32)[:, None] * inv_freq[None, :]
    cos, sin = jnp.cos(ang), jnp.sin(ang)
    ones = jnp.ones((seq, MLA_NOPE_DIM), F32)
    tail = MLA_HEAD_PAD - MLA_QK_DIM
    cos_t = jnp.concatenate([ones, cos, cos, jnp.ones((seq, tail), F32)], axis=1)
    sin_t = jnp.concatenate([0.0 * ones, -sin, sin, jnp.zeros((seq, tail), F32)], axis=1)
    return cos_t, sin_t


def _pick_tile(n, pref):
    t = min(n, pref)
    while n % t:
        t //= 2
    return t


def kernel(x, norm1_g, w_in, lru_conv_w, lru_conv_b, lru_wa, lru_ba, lru_wx, lru_bx, lru_lambda, lru_out_g,
           diff_q_g, diff_k_g, diff_lq1, diff_lk1, diff_lq2, diff_lk2, diff_sub_g,
           mla_cq_g, mla_ckv_g, mla_w_uq, mla_w_ukv, mla_q_g, mla_k_g, mla_out_g, w_out, norm2_g,
           router_g_w, router_g_b, router_e_w, router_e_b, exp_w1, exp_w3, exp_w2):
    B, S, D = x.shape
    N = B * S
    depth = w_in.shape[0]
    assert S % KEY_TILE == 0 and D == D_MODEL
    ts = _pick_tile(S, 512)
    tc = _pick_tile(S, 512)
    assert N % TOK_TILE == 0
    tm = _pick_tile(N, 512)

    cos_t, sin_t = _rotary_lane_tables(S)
    cos_q, sin_q = cos_t.T, sin_t.T
    per_row = lambda v: jnp.broadcast_to(v.astype(F32)[:, None], (v.shape[0], LANES))
    n_dgroups = DIFF_QK_WIDTH // DIFF_HEAD_DIM
    gsum = jnp.asarray(np.kron(np.eye(n_dgroups), np.ones((DIFF_HEAD_DIM, DIFF_HEAD_DIM))), BF16)
    slopes = np.asarray([2.0 ** (-ALIBI_MAX_BIAS * (h + 1) / DIFF_HEADS) * LOG2E for h in range(DIFF_HEADS)], np.float32)
    slopes = jnp.asarray(np.pad(slopes.reshape(DIFF_HEADS // 2, 1, 2), ((0, 0), (0, 0), (0, LANES - 2))))
    place = np.zeros((MLA_ROPE_DIM, MLA_HEADS, MLA_HEAD_PAD), np.float32)
    place[np.arange(MLA_ROPE_DIM), :, MLA_NOPE_DIM + np.arange(MLA_ROPE_DIM)] = 1.0
    place = jnp.asarray(place.reshape(MLA_ROPE_DIM, MLA_QK_PAD))
    row = lambda v: v.reshape(1, -1).astype(F32)

    n_blk = -(-(2 * N + N_EXPERTS * (RUN_CHUNK + ROW_BLOCK)) // ROW_BLOCK) + 1
    n_rows = n_blk * ROW_BLOCK

    for l in range(depth):
        dqg = per_row(jnp.tile(diff_q_g[l], n_dgroups) * (DIFF_HEAD_DIM ** -0.5 * LOG2E))
        dkg = row(jnp.tile(diff_k_g[l], n_dgroups))
        wuq = _pad_heads(mla_w_uq[l], MLA_QK_DIM).astype(BF16)
        mqg = per_row(_pad_heads(jnp.tile(mla_q_g[l], MLA_HEADS) * (MLA_QK_DIM ** -0.5 * LOG2E), MLA_QK_DIM))
        mkg = row(_pad_heads(jnp.tile(mla_k_g[l], MLA_HEADS), MLA_QK_DIM))
        wukv = mla_w_ukv[l].reshape(MLA_KV_RANK, MLA_HEADS, MLA_NOPE_DIM + MLA_V_DIM)
        wk_nope = _pad_heads(wukv[:, :, :MLA_NOPE_DIM].reshape(MLA_KV_RANK, -1), MLA_NOPE_DIM)
        wk = jnp.concatenate([wk_nope, place], axis=0).astype(BF16)
        wv = wukv[:, :, MLA_NOPE_DIM:].reshape(MLA_KV_RANK, MLA_WIDTH).astype(BF16)

        lru_in, dq, dk, dv, mq, mk, mv = _in_proj(
            x, row(norm1_g[l]), w_in, l, gsum, dqg, dkg, row(mla_cq_g[l]), wuq, mqg, row(mla_ckv_g[l]), wk, mkg, wv,
            cos_t, sin_t, cos_q, sin_q, ts)

        y_lru = _lru(lru_in, lru_conv_w[l], row(lru_conv_b[l]), _block_diag(lru_wa[l]).astype(BF16),
                     row(lru_ba[l]), _block_diag(lru_wx[l]).astype(BF16), row(lru_bx[l]), row(lru_lambda[l]),
                     row(lru_out_g[l]), tc)

        lambda_init = 0.8 - 0.6 * math.exp(-0.3 * l)
        y_diff = _diff_attn(dq, dk, dv, slopes, row(diff_lq1[l]), row(diff_lk1[l]), row(diff_lq2[l]),
                            row(diff_lk2[l]), row(jnp.tile(diff_sub_g[l], 2)), lambda_init)
        y_mla = _mla_attn(mq, mk, mv)

        wr = jnp.concatenate([router_g_w[l], router_e_w[l],
                              jnp.zeros((D, ROUTER_PAD - N_GROUPS - N_EXPERTS), F32)], axis=1)
        wrh = wr.astype(BF16)
        wrl = (wr - wrh.astype(F32)).astype(BF16)
        rb = jnp.pad(jnp.concatenate([router_g_b[l], router_e_b[l]]), (0, ROUTER_PAD - N_GROUPS - N_EXPERTS))
        x1, h2, eid, gate, cnt = _out_proj(
            x.reshape(N, D), y_lru.reshape(N, LRU_WIDTH), y_diff.reshape(N, DIFF_WIDTH), y_mla.reshape(N, MLA_WIDTH),
            row(mla_out_g[l]), w_out[l].astype(BF16), row(norm2_g[l]), wrh, wrl, row(rb), tm)

        counts = cnt[0, N_GROUPS:N_GROUPS + N_EXPERTS].astype(jnp.int32)
        padded = (counts + RUN_CHUNK + ROW_BLOCK - 1) // ROW_BLOCK * ROW_BLOCK
        pad_ends = jnp.cumsum(padded)
        pad_starts = pad_ends - padded
        blk_row0 = jnp.arange(n_blk, dtype=jnp.int32) * ROW_BLOCK
        blk_expert = jnp.minimum(jnp.sum((pad_ends[None, :] <= blk_row0[:, None]).astype(jnp.int32), axis=1),
                                 N_EXPERTS - 1)
        n_used = (pad_ends[-1:] // ROW_BLOCK).astype(jnp.int32)
        starts = jnp.pad(pad_starts.astype(F32), (0, LANES - N_EXPERTS)).reshape(1, LANES)
        tail_start = pad_starts + counts
        last = jnp.arange(N_EXPERTS) == N_EXPERTS - 1
        tail_ends = jnp.where(last, n_rows, pad_ends)
        aligned_start = (tail_start + RUN_CHUNK - 1) // RUN_CHUNK * RUN_CHUNK
        ztab = jnp.concatenate([tail_start, aligned_start, (tail_ends - aligned_start) // RUN_CHUNK]).astype(jnp.int32)

        loc, tab = _rank(eid, starts)
        tabs = tab.reshape(N // TOK_TILE, TAB_ROWS, LANES)[:, :3, :N_EXPERTS].reshape(-1)
        rows = _dispatch(tabs, ztab, h2, loc, n_rows)
        y_rows = _experts(blk_expert + l * N_EXPERTS, n_used, rows, exp_w1.reshape(-1, D, D_EXPERT),
                          exp_w3.reshape(-1, D, D_EXPERT), exp_w2.reshape(-1, D_EXPERT, D))
        x = _combine(tabs, x1, gate, loc, y_rows).reshape(B, S, D)
    return x
```

```python
import functools
import math

import jax
import jax.numpy as jnp
import numpy as np
from jax import lax
from jax.experimental import pallas as pl
from jax.experimental.pallas import tpu as pltpu

F32 = jnp.float32
BF16 = jnp.bfloat16

D_MODEL = 1024
CHUNK = 64
LRU_WIDTH = 256
CONV_WIDTH = 4
RG_C = 8.0
DIFF_HEADS = 6
DIFF_HEAD_DIM = 32
DIFF_V_DIM = 64
DIFF_QK_WIDTH = DIFF_HEADS * 2 * DIFF_HEAD_DIM
DIFF_WIDTH = DIFF_HEADS * DIFF_V_DIM
ALIBI_MAX_BIAS = 8.0
MLA_HEADS = 6
MLA_Q_RANK = 192
MLA_KV_RANK = 128
MLA_NOPE_DIM = 64
MLA_ROPE_DIM = 32
MLA_V_DIM = 64
MLA_QK_DIM = MLA_NOPE_DIM + MLA_ROPE_DIM
MLA_WIDTH = MLA_HEADS * MLA_V_DIM
ROPE_THETA = 10000.0
N_GROUPS = 4
EXPERTS_PER_GROUP = 8
N_EXPERTS = N_GROUPS * EXPERTS_PER_GROUP
D_EXPERT = 256
ROW_BLOCK = 256
EPS = 1e-6

LANES = 128
ROW_TILES = D_MODEL // LANES
MLA_HEAD_PAD = LANES
MLA_QK_PAD = MLA_HEADS * MLA_HEAD_PAD
VMEM_LIMIT = 56 * 1024 * 1024
NEG_BIG = -1e30
KEY_TILE = 256
QUERY_TILE = 512
TILES_PER_GROUP = 4
LOG2E = math.log2(math.e)

C_LRU = 0
C_DQ = 512
C_DK = 896
C_DV = 1280
C_CQ = 1664
C_CKV = 1856
C_KR = 1984
D_IN = 2016
D_IN_PAD = 2048


def _cparams(sem):
    return pltpu.CompilerParams(dimension_semantics=sem, vmem_limit_bytes=VMEM_LIMIT)


def _rms(v, width):
    return v * lax.rsqrt(jnp.sum(v * v, axis=-1, keepdims=True) * (1.0 / width) + EPS)


def _dot(a, b):
    return jnp.dot(a, b, preferred_element_type=F32)


def _split_bf16(v):
    hi = v.astype(BF16)
    lo = (v - hi.astype(F32)).astype(BF16)
    return hi, lo


def _in_proj_kernel(x_ref, g1_ref, w_ref, gsum_ref, dqg_ref, dkg_ref, cqg_ref, wuq_ref, mqg_ref, ckvg_ref,
                    wk_ref, mkg_ref, wv_ref, cos_ref, sin_ref, cosq_ref, sinq_ref,
                    lru_ref, dqt_ref, dk_ref, dvt_ref, mqt_ref, mk_ref, mvt_ref, w_bf):
    @pl.when((pl.program_id(0) == 0) & (pl.program_id(1) == 0))
    def _():
        w_bf[:, 0:D_IN] = w_ref[0].astype(BF16)
        w_bf[:, D_IN:] = jnp.zeros((D_MODEL, D_IN_PAD - D_IN), BF16)

    x = x_ref[0]
    hn = (_rms(x, D_MODEL) * g1_ref[...]).astype(BF16)
    p = _dot(hn, w_bf[...])
    lru_ref[0] = p[:, C_LRU:C_LRU + 2 * LRU_WIDTH]

    gsum = gsum_ref[...]

    def group_norm32(v, gain):
        hi, lo = _split_bf16(v * v)
        ss = _dot(hi, gsum) + _dot(lo, gsum)
        return v * lax.rsqrt(ss * (1.0 / DIFF_HEAD_DIM) + EPS) * gain

    def store_tiles(ref, vt):
        for c in range(vt.shape[1] // KEY_TILE):
            ref[0, c] = vt[:, c * KEY_TILE:(c + 1) * KEY_TILE].astype(BF16)

    t = x.shape[0]

    def along_lanes(per_feature):
        return jnp.concatenate([per_feature] * (t // LANES), axis=1)

    dq_t = p[:, C_DQ:C_DQ + DIFF_QK_WIDTH].T.reshape(DIFF_QK_WIDTH // DIFF_HEAD_DIM, DIFF_HEAD_DIM, t)
    dq_ss = jnp.sum(dq_t * dq_t, axis=1, keepdims=True) * (1.0 / DIFF_HEAD_DIM)
    dq_n = (dq_t * lax.rsqrt(dq_ss + EPS)).reshape(DIFF_QK_WIDTH, t) * along_lanes(dqg_ref[...])
    dqt_ref[0] = dq_n.astype(BF16)
    dk_ref[0] = group_norm32(p[:, C_DK:C_DK + DIFF_QK_WIDTH], dkg_ref[...]).astype(BF16)
    store_tiles(dvt_ref, p[:, C_DV:C_DV + DIFF_WIDTH].T)

    cos = cos_ref[...]
    sin = sin_ref[...]
    lane = lax.broadcasted_iota(jnp.int32, cos.shape, 1)
    first_half = lane < MLA_NOPE_DIM + MLA_ROPE_DIM // 2

    def head_norm_rotary(v, gain):
        outs = []
        for h in range(MLA_HEADS):
            c = v[:, h * MLA_HEAD_PAD:(h + 1) * MLA_HEAD_PAD]
            c = _rms(c, MLA_QK_DIM) * gain[:, h * MLA_HEAD_PAD:(h + 1) * MLA_HEAD_PAD]
            swapped = jnp.where(first_half, pltpu.roll(c, MLA_HEAD_PAD - MLA_ROPE_DIM // 2, 1),
                                pltpu.roll(c, MLA_ROPE_DIM // 2, 1))
            outs.append(c * cos + swapped * sin)
        return jnp.concatenate(outs, axis=1)

    cq = (_rms(p[:, C_CQ:C_CQ + MLA_Q_RANK], MLA_Q_RANK) * cqg_ref[...]).astype(BF16)
    q_t = _dot(cq, wuq_ref[...]).T
    cos_q, sin_q = cosq_ref[...], sinq_ref[...]
    gain_q = along_lanes(mqg_ref[...])
    half = MLA_ROPE_DIM // 2
    heads_t = []
    for h in range(MLA_HEADS):
        rows = slice(h * MLA_HEAD_PAD, (h + 1) * MLA_HEAD_PAD)
        c = q_t[rows]
        ss = jnp.sum(c * c, axis=0, keepdims=True) * (1.0 / MLA_QK_DIM)
        c = c * lax.rsqrt(ss + EPS) * gain_q[rows]
        swapped = jnp.concatenate([c[:MLA_NOPE_DIM], c[MLA_NOPE_DIM + half:MLA_QK_DIM],
                                   c[MLA_NOPE_DIM:MLA_NOPE_DIM + half], c[MLA_QK_DIM:]], axis=0)
        heads_t.append((c * cos_q + swapped * sin_q).astype(BF16))
    mqt_ref[0] = jnp.concatenate(heads_t, axis=0)

    ckv = (_rms(p[:, C_CKV:C_CKV + MLA_KV_RANK], MLA_KV_RANK) * ckvg_ref[...]).astype(BF16)
    kcat = jnp.concatenate([ckv, p[:, C_KR:C_KR + MLA_ROPE_DIM].astype(BF16)], axis=1)
    mk_ref[0] = head_norm_rotary(_dot(kcat, wk_ref[...]), mkg_ref[...]).astype(BF16)
    store_tiles(mvt_ref, _dot(ckv, wv_ref[...]).T)


def _in_proj(x, g1, w_all, layer, gsum, dqg, dkg, cqg, wuq, mqg, ckvg, wk, mkg, wv, cos_t, sin_t, cos_q, sin_q, ts):
    B, S, _ = x.shape
    full = lambda a: pl.BlockSpec(a.shape, lambda b, i: (0,) * a.ndim)
    w_spec = pl.BlockSpec((1,) + w_all.shape[1:], lambda b, i: (layer, 0, 0))
    tok = lambda width: pl.BlockSpec((1, ts, width), lambda b, i: (b, i, 0))
    tok_t = lambda width: pl.BlockSpec((1, width, ts), lambda b, i: (b, 0, i))
    nkt = ts // KEY_TILE
    tiles_t = lambda width: pl.BlockSpec((1, nkt, width, KEY_TILE), lambda b, i: (b, i, 0, 0))
    rot = pl.BlockSpec((ts, LANES), lambda b, i: (i, 0))
    rot_t = pl.BlockSpec((LANES, ts), lambda b, i: (0, i))
    shp = lambda shape, dt: jax.ShapeDtypeStruct(shape, dt)
    return pl.pallas_call(
        _in_proj_kernel,
        grid=(B, S // ts),
        in_specs=[tok(D_MODEL), full(g1), w_spec, full(gsum), full(dqg), full(dkg), full(cqg), full(wuq),
                  full(mqg), full(ckvg), full(wk), full(mkg), full(wv), rot, rot, rot_t, rot_t],
        out_specs=[tok(2 * LRU_WIDTH), tok_t(DIFF_QK_WIDTH), tok(DIFF_QK_WIDTH), tiles_t(DIFF_WIDTH),
                   tok_t(MLA_QK_PAD), tok(MLA_QK_PAD), tiles_t(MLA_WIDTH)],
        out_shape=[shp((B, S, 2 * LRU_WIDTH), F32), shp((B, DIFF_QK_WIDTH, S), BF16), shp((B, S, DIFF_QK_WIDTH), BF16),
                   shp((B, S // KEY_TILE, DIFF_WIDTH, KEY_TILE), BF16), shp((B, MLA_QK_PAD, S), BF16),
                   shp((B, S, MLA_QK_PAD), BF16), shp((B, S // KEY_TILE, MLA_WIDTH, KEY_TILE), BF16)],
        scratch_shapes=[pltpu.VMEM((D_MODEL, D_IN_PAD), BF16)],
        compiler_params=_cparams(("arbitrary", "arbitrary")),
        name="in_proj",
    )(x, g1, w_all, gsum, dqg, dkg, cqg, wuq, mqg, ckvg, wk, mkg, wv, cos_t, sin_t, cos_q, sin_q)


LRU_HALO = 8


def _lru_kernel(blk_ref, cw_ref, cb_ref, wa_ref, ba_ref, wx_ref, bx_ref, lam_ref, og_ref, y_ref,
                ubuf, a_scr, b_scr, h_scr):
    B, tc, _ = blk_ref.shape
    W = LRU_WIDTH

    @pl.when(pl.program_id(0) == 0)
    def _():
        ubuf[:, 0:LRU_HALO, :] = jnp.zeros((B, LRU_HALO, W), F32)
        h_scr[...] = jnp.zeros(h_scr.shape, F32)

    ubuf[:, LRU_HALO:, :] = blk_ref[:, :, 0:W]
    xc = cb_ref[...][None]
    for j in range(CONV_WIDTH):
        off = LRU_HALO - (CONV_WIDTH - 1) + j
        xc = xc + cw_ref[j:j + 1, :][None] * ubuf[:, off:off + tc, :]
    ubuf[:, 0:LRU_HALO, :] = ubuf[:, tc:tc + LRU_HALO, :]

    x2 = xc.reshape(B * tc, W)
    xb = x2.astype(BF16)
    r = jax.nn.sigmoid(_dot(xb, wa_ref[...]) + ba_ref[...])
    gi = jax.nn.sigmoid(_dot(xb, wx_ref[...]) + bx_ref[...])
    nl = -lam_ref[...]
    softplus = jnp.maximum(nl, 0.0) + jnp.log(1.0 + jnp.exp(-jnp.abs(nl)))
    a = jnp.exp(-RG_C * r * softplus)
    bb = jnp.sqrt(1.0 - a * a) * gi * x2
    a_scr[...] = a.reshape(B, tc, W)
    b_scr[...] = bb.reshape(B, tc, W)

    def step(t, h):
        h = a_scr[:, pl.ds(t, 1), :] * h + b_scr[:, pl.ds(t, 1), :]
        b_scr[:, pl.ds(t, 1), :] = h
        return h

    h_scr[...] = lax.fori_loop(0, tc, step, h_scr[...], unroll=8)

    gate = blk_ref[:, :, W:2 * W]
    gelu = 0.5 * gate * (1.0 + jnp.tanh(math.sqrt(2.0 / math.pi) * (gate + 0.044715 * gate * gate * gate)))
    y = b_scr[...] * gelu
    y_ref[...] = (_rms(y, W) * og_ref[...][None]).astype(y_ref.dtype)


def _lru(lru_in, cw, cb, wa, ba, wx, bx, lam, og, tc):
    B, S, _ = lru_in.shape
    W = LRU_WIDTH
    full = lambda a: pl.BlockSpec(a.shape, lambda i: (0,) * a.ndim)
    return pl.pallas_call(
        _lru_kernel,
        grid=(S // tc,),
        in_specs=[pl.BlockSpec((B, tc, 2 * W), lambda i: (0, i, 0)), full(cw), full(cb), full(wa), full(ba),
                  full(wx), full(bx), full(lam), full(og)],
        out_specs=pl.BlockSpec((B, tc, W), lambda i: (0, i, 0)),
        out_shape=jax.ShapeDtypeStruct((B, S, W), BF16),
        scratch_shapes=[pltpu.VMEM((B, tc + LRU_HALO, W), F32), pltpu.VMEM((B, tc, W), F32),
                        pltpu.VMEM((B, tc, W), F32), pltpu.VMEM((B, 1, W), F32)],
        compiler_params=_cparams(("arbitrary",)),
        name="lru",
    )(lru_in, cw, cb, wa, ba, wx, bx, lam, og)


def _flash_maps(qt_list, k_ref, vt_ref, k_lanes, bias_of, tile_shift, m_scr, l_scr, acc_scr, tiles_per_step):
    qi = pl.program_id(2)
    n_maps = len(qt_list)
    tq = qt_list[0].shape[1]
    maps = range(n_maps)

    own = tq // KEY_TILE
    first_own = own * qi
    last = first_own + own - 1

    def biased_scores(j):
        jc = jnp.minimum(j, last)
        start = pl.multiple_of(jc * KEY_TILE, KEY_TILE)
        kind = jnp.clip(jc - first_own + 1, 0, own)
        return tuple(_dot(k_ref[0, pl.ds(start, KEY_TILE), k_lanes[i]], qt_list[i]) + bias_of(i, kind)
                     for i in maps)

    for i in maps:
        m_scr[i] = jnp.full((1, tq), NEG_BIG, F32)
        l_scr[i] = jnp.zeros((1, tq), F32)
        acc_scr[i] = jnp.zeros(acc_scr.shape[1:], F32)

    def update(j, s_tile):
        vt = vt_ref[0, jnp.minimum(j, last)]
        for i in maps:
            s = s_tile[i]
            c = tile_shift(i, first_own - j)
            m_old = m_scr[i]
            m_new = jnp.maximum(m_old, jnp.max(s, axis=0, keepdims=True) + c)
            alpha = jnp.exp2(m_old - m_new)
            p = jnp.exp2(s - (m_new - c))
            m_scr[i] = m_new
            l_scr[i] = alpha * l_scr[i] + jnp.sum(p, axis=0, keepdims=True)
            acc_scr[i] = alpha * acc_scr[i] + _dot(vt, p.astype(BF16))

    def one_tile(j, s_tile):
        s_ahead = biased_scores(j + 1)
        update(j, s_tile)
        return s_ahead

    def group(t, s_tile):
        for u in range(tiles_per_step):
            s_tile = one_tile(tiles_per_step * t + u, s_tile)
        return s_tile

    n_groups = (last + 1) // tiles_per_step
    s_rest = lax.fori_loop(0, n_groups, group, biased_scores(0))
    lax.fori_loop(n_groups * tiles_per_step, last + 1, one_tile, s_rest)


def _own_tile_geometry(tq, u):
    krow = lax.broadcasted_iota(jnp.int32, (KEY_TILE, tq), 0) + u * KEY_TILE
    qcol = lax.broadcasted_iota(jnp.int32, (KEY_TILE, tq), 1)
    return krow.astype(F32), qcol.astype(F32), (krow // CHUNK) <= (qcol // CHUNK)


def _diff_attn_kernel(qt_ref, k_ref, vt_ref, slope_ref, lq1_ref, lk1_ref, lq2_ref, lk2_ref, subg_ref, o_ref,
                      m_scr, l_scr, acc_scr, bias_scr, *, lambda_init, tq):
    qt = qt_ref[0]
    feat = lax.broadcasted_iota(jnp.int32, qt.shape, 0)
    zero = jnp.zeros_like(qt)
    qt_list = [jnp.where(feat // DIFF_HEAD_DIM == i, qt, zero) for i in range(4)]
    slope = [slope_ref[0, 0:1, hh:hh + 1] for hh in range(2)]

    @pl.when(pl.program_id(2) == 0)
    def _():
        for hh in range(2):
            bias_scr[hh, 0] = slope[hh] * _own_tile_geometry(tq, 0)[0]
            for u in range(tq // KEY_TILE):
                kf, qf, allowed = _own_tile_geometry(tq, u)
                own_bias = slope[hh] * (qf - jnp.abs(qf - kf) - float(u * KEY_TILE))
                bias_scr[hh, u + 1] = jnp.where(allowed, own_bias, NEG_BIG)

    def tile_shift(i, n_tiles):
        return -slope[i // 2] * (n_tiles * KEY_TILE).astype(F32)

    _flash_maps(qt_list, k_ref, vt_ref, [slice(None)] * 4, lambda i, kind: bias_scr[i // 2, kind],
                tile_shift, m_scr, l_scr, acc_scr, TILES_PER_GROUP)

    lam = (jnp.exp(jnp.sum(lq1_ref[...] * lk1_ref[...], axis=-1, keepdims=True))
           - jnp.exp(jnp.sum(lq2_ref[...] * lk2_ref[...], axis=-1, keepdims=True)) + lambda_init)
    orow = lax.broadcasted_iota(jnp.int32, (LANES, tq), 0)
    out_t = jnp.zeros((LANES, tq), F32)
    for hh in range(2):
        o = (acc_scr[2 * hh] * (1.0 / l_scr[2 * hh])
             - acc_scr[2 * hh + 1] * (lam / l_scr[2 * hh + 1]))
        mine = (orow // DIFF_V_DIM) == hh
        ms = jnp.sum(jnp.where(mine, o * o, 0.0), axis=0, keepdims=True) * (1.0 / DIFF_V_DIM)
        out_t = jnp.where(mine, o * lax.rsqrt(ms + EPS), out_t)
    o_ref[0] = (out_t.T * subg_ref[...] * (1.0 - lambda_init)).astype(o_ref.dtype)


def _diff_attn(dqt, dk, dvt, slopes, lq1, lk1, lq2, lk2, subg, lambda_init):
    B, S, _ = dk.shape
    tq = min(QUERY_TILE, S)
    n_pairs = DIFF_HEADS // 2
    full = lambda a: pl.BlockSpec(a.shape, lambda b, p, i: (0,) * a.ndim)
    kern = functools.partial(_diff_attn_kernel, lambda_init=lambda_init, tq=tq)
    return pl.pallas_call(
        kern,
        grid=(B, n_pairs, S // tq),
        in_specs=[pl.BlockSpec((1, LANES, tq), lambda b, p, i: (b, p, i)),
                  pl.BlockSpec((1, S, LANES), lambda b, p, i: (b, 0, p)),
                  pl.BlockSpec((1, S // KEY_TILE, LANES, KEY_TILE), lambda b, p, i: (b, 0, p, 0)),
                  pl.BlockSpec((1, 1, LANES), lambda b, p, i: (p, 0, 0)),
                  full(lq1), full(lk1), full(lq2), full(lk2), full(subg)],
        out_specs=pl.BlockSpec((1, tq, LANES), lambda b, p, i: (b, i, p)),
        out_shape=jax.ShapeDtypeStruct((B, S, DIFF_WIDTH), BF16),
        scratch_shapes=[pltpu.VMEM((4, 1, tq), F32), pltpu.VMEM((4, 1, tq), F32), pltpu.VMEM((4, LANES, tq), F32),
                        pltpu.VMEM((2, 1 + tq // KEY_TILE, KEY_TILE, tq), F32)],
        compiler_params=_cparams(("parallel", "parallel", "arbitrary")),
        name="diff_attn",
    )(dqt, dk, dvt, slopes, lq1, lk1, lq2, lk2, subg)


def _mla_attn_kernel(qt_ref, k_ref, vt_ref, o_ref, m_scr, l_scr, acc_scr, bias_scr, *, tq):
    qt = qt_ref[0]
    qt_list = [qt[hh * MLA_HEAD_PAD:(hh + 1) * MLA_HEAD_PAD, :] for hh in range(2)]
    k_lanes = [slice(hh * MLA_HEAD_PAD, (hh + 1) * MLA_HEAD_PAD) for hh in range(2)]

    @pl.when(pl.program_id(2) == 0)
    def _():
        bias_scr[0] = jnp.zeros((KEY_TILE, tq), F32)
        for u in range(tq // KEY_TILE):
            bias_scr[u + 1] = jnp.where(_own_tile_geometry(tq, u)[2], 0.0, NEG_BIG)

    _flash_maps(qt_list, k_ref, vt_ref, k_lanes, lambda i, kind: bias_scr[kind], lambda i, n: 0.0,
                m_scr, l_scr, acc_scr, TILES_PER_GROUP)
    orow = lax.broadcasted_iota(jnp.int32, (LANES, tq), 0)
    out_t = jnp.where(orow < MLA_V_DIM, acc_scr[0] * (1.0 / l_scr[0]), acc_scr[1] * (1.0 / l_scr[1]))
    o_ref[0] = out_t.T.astype(o_ref.dtype)


def _mla_attn(mqt, mk, mvt):
    B, S, _ = mk.shape
    tq = min(QUERY_TILE, S)
    n_pairs = MLA_HEADS // 2
    kern = functools.partial(_mla_attn_kernel, tq=tq)
    return pl.pallas_call(
        kern,
        grid=(B, n_pairs, S // tq),
        in_specs=[pl.BlockSpec((1, 2 * MLA_HEAD_PAD, tq), lambda b, p, i: (b, p, i)),
                  pl.BlockSpec((1, S, 2 * MLA_HEAD_PAD), lambda b, p, i: (b, 0, p)),
                  pl.BlockSpec((1, S // KEY_TILE, LANES, KEY_TILE), lambda b, p, i: (b, 0, p, 0))],
        out_specs=pl.BlockSpec((1, tq, LANES), lambda b, p, i: (b, i, p)),
        out_shape=jax.ShapeDtypeStruct((B, S, MLA_WIDTH), F32),
        scratch_shapes=[pltpu.VMEM((2, 1, tq), F32), pltpu.VMEM((2, 1, tq), F32), pltpu.VMEM((2, LANES, tq), F32),
                        pltpu.VMEM((1 + tq // KEY_TILE, KEY_TILE, tq), F32)],
        compiler_params=_cparams(("parallel", "parallel", "arbitrary")),
        name="mla_attn",
    )(mqt, mk, mvt)


ROUTER_PAD = LANES


def _out_proj_kernel(x_ref, ylru_ref, ydiff_ref, ymla_ref, mlag_ref, wo_ref, g2_ref, wrh_ref, wrl_ref, rb_ref,
                     x1_ref, h2_ref, eid_ref, gate_ref, cnt_ref):
    @pl.when(pl.program_id(0) == 0)
    def _():
        cnt_ref[...] = jnp.zeros(cnt_ref.shape, F32)

    ymla = (_rms(ymla_ref[...], MLA_WIDTH) * mlag_ref[...]).astype(BF16)
    x1 = (x_ref[...]
          + _dot(ylru_ref[...], wo_ref[0:LRU_WIDTH, :])
          + _dot(ydiff_ref[...], wo_ref[LRU_WIDTH:LRU_WIDTH + DIFF_WIDTH, :])
          + _dot(ymla, wo_ref[LRU_WIDTH + DIFF_WIDTH:, :]))
    x1_ref[...] = x1

    h2 = _rms(x1, D_MODEL) * g2_ref[...]
    hi, lo = _split_bf16(h2)
    h2_ref[...] = hi
    wrh = wrh_ref[...]
    logits = _dot(hi, wrh) + _dot(hi, wrl_ref[...]) + _dot(lo, wrh) + rb_ref[...]

    lane = lax.broadcasted_iota(jnp.int32, logits.shape, 1)
    big = jnp.int32(1 << 20)
    is_group = lane < N_GROUPS
    gl = jnp.where(is_group, logits, NEG_BIG)
    gmax = jnp.max(gl, axis=-1, keepdims=True)
    g_idx = jnp.min(jnp.where(gl == gmax, lane, big), axis=-1, keepdims=True)
    pg_top = 1.0 / jnp.sum(jnp.where(is_group, jnp.exp(gl - gmax), 0.0), axis=-1, keepdims=True)

    e_lane = lane - N_GROUPS
    in_group = (e_lane >= 0) & (e_lane < N_EXPERTS) & ((e_lane // EXPERTS_PER_GROUP) == g_idx)
    el = jnp.where(in_group, logits, NEG_BIG)
    m1 = jnp.max(el, axis=-1, keepdims=True)
    i1 = jnp.min(jnp.where(el == m1, lane, big), axis=-1, keepdims=True)
    el2 = jnp.where(lane == i1, NEG_BIG, el)
    m2 = jnp.max(el2, axis=-1, keepdims=True)
    i2 = jnp.min(jnp.where(el2 == m2, lane, big), axis=-1, keepdims=True)
    e2 = jnp.exp(m2 - m1)
    g1 = pg_top / (1.0 + e2)
    g2 = pg_top * e2 / (1.0 + e2)

    two = lax.broadcasted_iota(jnp.int32, eid_ref.shape, 1)
    eid_ref[...] = jnp.where(two == 0, i1 - N_GROUPS, i2 - N_GROUPS)
    gate_ref[...] = jnp.where(two == 0, g1, g2)
    onehot = ((e_lane == i1 - N_GROUPS) | (e_lane == i2 - N_GROUPS)).astype(F32)
    cnt_ref[...] += jnp.sum(onehot, axis=0, keepdims=True)


def _out_proj(x, ylru, ydiff, ymla, mlag, wo, g2, wrh, wrl, rb, tm):
    N = x.shape[0]
    full = lambda a: pl.BlockSpec(a.shape, lambda i: (0,) * a.ndim)
    tok = lambda width: pl.BlockSpec((tm, width), lambda i: (i, 0))
    return pl.pallas_call(
        _out_proj_kernel,
        grid=(N // tm,),
        in_specs=[tok(D_MODEL), tok(LRU_WIDTH), tok(DIFF_WIDTH), tok(MLA_WIDTH), full(mlag), full(wo), full(g2),
                  full(wrh), full(wrl), full(rb)],
        out_specs=[tok(D_MODEL), tok(D_MODEL), tok(2), tok(2), pl.BlockSpec((1, ROUTER_PAD), lambda i: (0, 0))],
        out_shape=[jax.ShapeDtypeStruct((N, D_MODEL), F32), jax.ShapeDtypeStruct((N, D_MODEL), BF16),
                   jax.ShapeDtypeStruct((N, 2), jnp.int32),
                   jax.ShapeDtypeStruct((N, 2), F32), jax.ShapeDtypeStruct((1, ROUTER_PAD), F32)],
        compiler_params=_cparams(("arbitrary",)),
        name="out_proj",
    )(x, ylru, ydiff, ymla, mlag, wo, g2, wrh, wrl, rb)


TOK_TILE = 256
RUN_CHUNK = 8
TAB_ROWS = 8


def _rank_kernel(eid_ref, start_ref, loc_ref, tab_ref, carry):
    tm = eid_ref.shape[0]

    @pl.when(pl.program_id(0) == 0)
    def _():
        carry[...] = start_ref[...]

    lane = lax.broadcasted_iota(jnp.int32, (tm, LANES), 1)
    r = lax.broadcasted_iota(jnp.int32, (tm, tm), 0)
    c = lax.broadcasted_iota(jnp.int32, (tm, tm), 1)
    lower = (c < r).astype(BF16)
    eid = eid_ref[...]
    onehot = [lane == eid[:, k:k + 1] for k in range(2)]
    cnt = [jnp.sum(oh.astype(F32), axis=0, keepdims=True) for oh in onehot]
    cnt_tile = cnt[0] + cnt[1]

    chunks = jnp.floor((cnt_tile + (RUN_CHUNK - 1)) * (1.0 / RUN_CHUNK))
    lane8 = lax.broadcasted_iota(jnp.int32, (TAB_ROWS, LANES), 1)
    incl = jnp.broadcast_to(chunks, (TAB_ROWS, LANES))
    shift = 1
    while shift < LANES:
        incl = incl + jnp.where(lane8 >= shift, pltpu.roll(incl, shift, 1), 0.0)
        shift *= 2
    offs = (incl[0:1] - chunks) * RUN_CHUNK

    locs = []
    for k in range(2):
        before = _dot(lower, onehot[k].astype(BF16))
        first = offs if k == 0 else offs + cnt[0]
        locs.append(jnp.sum(jnp.where(onehot[k], before + first, 0.0), axis=-1, keepdims=True))
    two = lax.broadcasted_iota(jnp.int32, loc_ref.shape, 1)
    loc_ref[...] = jnp.where(two == 0, locs[0], locs[1]).astype(jnp.int32)

    row8 = lax.broadcasted_iota(jnp.int32, (TAB_ROWS, LANES), 0)
    base = carry[...]
    tab = jnp.where(row8 == 0, cnt_tile, jnp.where(row8 == 1, offs, jnp.where(row8 == 2, base, 0.0)))
    tab_ref[...] = tab.astype(jnp.int32)
    carry[...] = base + cnt_tile


def _rank(eid, starts):
    N = eid.shape[0]
    n_tiles = N // TOK_TILE
    return pl.pallas_call(
        _rank_kernel,
        grid=(n_tiles,),
        in_specs=[pl.BlockSpec((TOK_TILE, 2), lambda i: (i, 0)), pl.BlockSpec((1, LANES), lambda i: (0, 0))],
        out_specs=[pl.BlockSpec((TOK_TILE, 2), lambda i: (i, 0)), pl.BlockSpec((TAB_ROWS, LANES), lambda i: (i, 0))],
        out_shape=[jax.ShapeDtypeStruct((N, 2), jnp.int32),
                   jax.ShapeDtypeStruct((n_tiles * TAB_ROWS, LANES), jnp.int32)],
        scratch_shapes=[pltpu.VMEM((1, LANES), F32)],
        compiler_params=_cparams(("arbitrary",)),
        name="rank",
    )(eid, starts)


SORT_ROWS = 2 * TOK_TILE + N_EXPERTS * RUN_CHUNK


def _run_chunks(tab_ref, tile, e):
    cnt = tab_ref[(tile * 3 + 0) * N_EXPERTS + e]
    offs = tab_ref[(tile * 3 + 1) * N_EXPERTS + e]
    base = tab_ref[(tile * 3 + 2) * N_EXPERTS + e]
    return lax.shift_right_logical(cnt + (RUN_CHUNK - 1), RUN_CHUNK.bit_length() - 1), offs, base


def _rows(ref, row, n):
    return ref.at[pl.ds(pl.multiple_of(row * ROW_TILES, ROW_TILES), n * ROW_TILES)]


def _for_each_chunk(tab_ref, tile, fn):
    for e in range(N_EXPERTS):
        chunks, offs, base = _run_chunks(tab_ref, tile, e)

        def one(k, carry):
            fn(offs + k * RUN_CHUNK, base + k * RUN_CHUNK)
            return carry

        lax.fori_loop(0, chunks, one, 0)


def _drain_chunks(tab_ref, tile, wait_one):
    total = 0
    for e in range(N_EXPERTS):
        total = total + _run_chunks(tab_ref, tile, e)[0]

    def one(k, carry):
        wait_one()
        return carry

    lax.fori_loop(0, total, one, 0)


def _dispatch_kernel(tab_ref, ztab_ref, h_ref, loc_ref, rows_hbm, sbuf, zbuf, sems, zsem):
    i = pl.program_id(0)
    n_tiles = pl.num_programs(0)
    slot = lax.rem(i, 2)
    T = TOK_TILE

    def run_copy(s, offs_row, base_row):
        return pltpu.make_async_copy(_rows(sbuf, s * SORT_ROWS + offs_row, RUN_CHUNK),
                                     _rows(rows_hbm, base_row, RUN_CHUNK), sems.at[s])

    @pl.when(i == 0)
    def _():
        zbuf[...] = jnp.zeros(zbuf.shape, F32)

        def zero_chunk(first_row):
            return pltpu.make_async_copy(zbuf, _rows(rows_hbm, first_row, RUN_CHUNK), zsem)

        def done(k, carry):
            zero_chunk(0).wait()
            return carry

        for e in range(N_EXPERTS):
            zero_chunk(ztab_ref[e]).start()
        lax.fori_loop(0, N_EXPERTS, done, 0)
        total = 0
        for e in range(N_EXPERTS):
            start, chunks = ztab_ref[N_EXPERTS + e], ztab_ref[2 * N_EXPERTS + e]
            total = total + chunks

            def fill(k, carry, start=start):
                zero_chunk(start + k * RUN_CHUNK).start()
                return carry

            lax.fori_loop(0, chunks, fill, 0)
        lax.fori_loop(0, total, done, 0)

    locf = loc_ref[...].astype(F32)
    loc_lanes = [jnp.broadcast_to(locf[:, k:k + 1], (T, LANES)).T[0:1, :] for k in range(2)]
    pos = lax.broadcasted_iota(jnp.int32, (SORT_ROWS, T), 0).astype(F32)
    perm = jnp.where((pos == loc_lanes[0]) | (pos == loc_lanes[1]), 1.0, 0.0).astype(BF16)
    sorted_rows = _dot(perm, h_ref[...])
    for c in range(ROW_TILES):
        sbuf[pl.ds(slot * (SORT_ROWS * ROW_TILES) + c, SORT_ROWS, stride=ROW_TILES), :] = (
            sorted_rows[:, c * LANES:(c + 1) * LANES])

    @pl.when(i > 0)
    def _():
        _drain_chunks(tab_ref, i - 1, lambda: run_copy(1 - slot, 0, 0).wait())

    _for_each_chunk(tab_ref, i, lambda offs_row, base_row: run_copy(slot, offs_row, base_row).start())

    @pl.when(i == n_tiles - 1)
    def _():
        _drain_chunks(tab_ref, i, lambda: run_copy(slot, 0, 0).wait())


def _dispatch(tabs, ztab, h2, loc, n_rows):
    N = h2.shape[0]
    return pl.pallas_call(
        _dispatch_kernel,
        grid_spec=pltpu.PrefetchScalarGridSpec(
            num_scalar_prefetch=2,
            grid=(N // TOK_TILE,),
            in_specs=[pl.BlockSpec((TOK_TILE, D_MODEL), lambda i, t, z: (i, 0)),
                      pl.BlockSpec((TOK_TILE, 2), lambda i, t, z: (i, 0))],
            out_specs=pl.BlockSpec(memory_space=pl.ANY),
            scratch_shapes=[pltpu.VMEM((2 * SORT_ROWS * ROW_TILES, LANES), F32),
                            pltpu.VMEM((RUN_CHUNK * ROW_TILES, LANES), F32),
                            pltpu.SemaphoreType.DMA((2,)), pltpu.SemaphoreType.DMA(())],
        ),
        out_shape=jax.ShapeDtypeStruct((n_rows * ROW_TILES, LANES), F32),
        compiler_params=_cparams(("arbitrary",)),
        name="dispatch",
    )(tabs, ztab, h2, loc)


def _expert_kernel(be_ref, nused_ref, rows_ref, w1_ref, w3_ref, w2_ref, y_ref):
    del be_ref
    used = pl.program_id(0) < nused_ref[0]

    @pl.when(used)
    def _():
        h = jnp.concatenate([rows_ref[pl.ds(c, ROW_BLOCK, stride=ROW_TILES), :] for c in range(ROW_TILES)],
                            axis=1).astype(BF16)
        a = _dot(h, w1_ref[0].astype(BF16))
        b = _dot(h, w3_ref[0].astype(BF16))
        z = (a * jax.nn.sigmoid(a) * b).astype(BF16)
        y = _dot(z, w2_ref[0].astype(BF16))
        for c in range(ROW_TILES):
            y_ref[pl.ds(c, ROW_BLOCK, stride=ROW_TILES), :] = y[:, c * LANES:(c + 1) * LANES]

    @pl.when(jnp.logical_not(used))
    def _():
        y_ref[...] = jnp.zeros(y_ref.shape, y_ref.dtype)


def _experts(blk_expert, n_used, rows, w1, w3, w2):
    n_blk = blk_expert.shape[0]
    row_blk = lambda i, be, nu: (jnp.minimum(i, nu[0] - 1), 0)
    wspec = lambda shape: pl.BlockSpec((1,) + shape, lambda i, be, nu: (be[i], 0, 0))
    return pl.pallas_call(
        _expert_kernel,
        grid_spec=pltpu.PrefetchScalarGridSpec(
            num_scalar_prefetch=2,
            grid=(n_blk,),
            in_specs=[pl.BlockSpec((ROW_BLOCK * ROW_TILES, LANES), row_blk),
                      wspec((D_MODEL, D_EXPERT)), wspec((D_MODEL, D_EXPERT)), wspec((D_EXPERT, D_MODEL))],
            out_specs=pl.BlockSpec((ROW_BLOCK * ROW_TILES, LANES), lambda i, be, nu: (i, 0)),
        ),
        out_shape=jax.ShapeDtypeStruct((n_blk * ROW_BLOCK * ROW_TILES, LANES), F32),
        compiler_params=_cparams(("arbitrary",)),
        name="experts",
    )(blk_expert, n_used, rows, w1, w3, w2)


def _combine_kernel(tab_ref, x1_ref, gate_ref, loc_ref, y_hbm, o_ref, ybuf, sems):
    i = pl.program_id(0)
    n_tiles = pl.num_programs(0)
    slot = lax.rem(i, 2)
    T = TOK_TILE

    def run_copy(s, offs_row, base_row):
        return pltpu.make_async_copy(_rows(y_hbm, base_row, RUN_CHUNK),
                                     _rows(ybuf, s * SORT_ROWS + offs_row, RUN_CHUNK), sems.at[s])

    def fetch(tile, s):
        _for_each_chunk(tab_ref, tile, lambda offs_row, base_row: run_copy(s, offs_row, base_row).start())

    @pl.when(i == 0)
    def _():
        ybuf[...] = jnp.zeros(ybuf.shape, F32)
        fetch(0, 0)

    @pl.when(i + 1 < n_tiles)
    def _():
        fetch(i + 1, 1 - slot)

    _drain_chunks(tab_ref, i, lambda: run_copy(slot, 0, 0).wait())

    y_sorted = jnp.concatenate([ybuf[pl.ds(slot * (SORT_ROWS * ROW_TILES) + c, SORT_ROWS, stride=ROW_TILES), :]
                                for c in range(ROW_TILES)], axis=1).astype(BF16)
    gate = gate_ref[...]
    loc = loc_ref[...]
    pos = lax.broadcasted_iota(jnp.int32, (T, SORT_ROWS), 1)
    g = (jnp.where(pos == loc[:, 0:1], gate[:, 0:1], 0.0) + jnp.where(pos == loc[:, 1:2], gate[:, 1:2], 0.0))
    g_hi, g_lo = _split_bf16(g)
    o_ref[...] = x1_ref[...] + _dot(g_hi, y_sorted) + _dot(g_lo, y_sorted)


def _combine(tabs, x1, gate, loc, y_rows):
    N = gate.shape[0]
    tok = lambda width: pl.BlockSpec((TOK_TILE, width), lambda i, t: (i, 0))
    return pl.pallas_call(
        _combine_kernel,
        grid_spec=pltpu.PrefetchScalarGridSpec(
            num_scalar_prefetch=1,
            grid=(N // TOK_TILE,),
            in_specs=[tok(D_MODEL), tok(2), tok(2), pl.BlockSpec(memory_space=pl.ANY)],
            out_specs=tok(D_MODEL),
            scratch_shapes=[pltpu.VMEM((2 * SORT_ROWS * ROW_TILES, LANES), F32), pltpu.SemaphoreType.DMA((2,))],
        ),
        out_shape=jax.ShapeDtypeStruct((N, D_MODEL), F32),
        compiler_params=_cparams(("arbitrary",)),
        name="combine",
    )(tabs, x1, gate, loc, y_rows)


def _block_diag(w):
    n, a, b = w.shape
    eye = np.eye(n, dtype=np.float32)
    return (w[:, :, None, :] * eye[:, None, :, None]).reshape(n * a, n * b)


def _pad_heads(v, used):
    lead = v.shape[:-1]
    v = v.reshape(lead + (MLA_HEADS, used))
    v = jnp.pad(v, [(0, 0)] * len(lead) + [(0, 0), (0, MLA_HEAD_PAD - used)])
    return v.reshape(lead + (MLA_QK_PAD,))


def _rotary_lane_tables(seq):
    half = MLA_ROPE_DIM // 2
    inv_freq = ROPE_THETA ** (-jnp.arange(half, dtype=F32) / half)
    ang = jnp.arange(seq, dtype=F32)[:, None] * inv_freq[None, :]
    cos, sin = jnp.cos(ang), jnp.sin(ang)
    ones = jnp.ones((seq, MLA_NOPE_DIM), F32)
    tail = MLA_HEAD_PAD - MLA_QK_DIM
    cos_t = jnp.concatenate([ones, cos, cos, jnp.ones((seq, tail), F32)], axis=1)
    sin_t = jnp.concatenate([0.0 * ones, -sin, sin, jnp.zeros((seq, tail), F32)], axis=1)
    return cos_t, sin_t


def _pick_tile(n, pref):
    t = min(n, pref)
    while n % t:
        t //= 2
    return t


def kernel(x, norm1_g, w_in, lru_conv_w, lru_conv_b, lru_wa, lru_ba, lru_wx, lru_bx, lru_lambda, lru_out_g,
           diff_q_g, diff_k_g, diff_lq1, diff_lk1, diff_lq2, diff_lk2, diff_sub_g,
           mla_cq_g, mla_ckv_g, mla_w_uq, mla_w_ukv, mla_q_g, mla_k_g, mla_out_g, w_out, norm2_g,
           router_g_w, router_g_b, router_e_w, router_e_b, exp_w1, exp_w3, exp_w2):
    B, S, D = x.shape
    N = B * S
    depth = w_in.shape[0]
    assert S % KEY_TILE == 0 and S % min(QUERY_TILE, S) == 0 and D == D_MODEL
    ts = _pick_tile(S, 512)
    tc = _pick_tile(S, 512)
    assert N % TOK_TILE == 0
    tm = _pick_tile(N, 512)

    cos_t, sin_t = _rotary_lane_tables(S)
    cos_q, sin_q = cos_t.T, sin_t.T
    per_row = lambda v: jnp.broadcast_to(v.astype(F32)[:, None], (v.shape[0], LANES))
    n_dgroups = DIFF_QK_WIDTH // DIFF_HEAD_DIM
    gsum = jnp.asarray(np.kron(np.eye(n_dgroups), np.ones((DIFF_HEAD_DIM, DIFF_HEAD_DIM))), BF16)
    slopes = np.asarray([2.0 ** (-ALIBI_MAX_BIAS * (h + 1) / DIFF_HEADS) * LOG2E for h in range(DIFF_HEADS)], np.float32)
    slopes = jnp.asarray(np.pad(slopes.reshape(DIFF_HEADS // 2, 1, 2), ((0, 0), (0, 0), (0, LANES - 2))))
    place = np.zeros((MLA_ROPE_DIM, MLA_HEADS, MLA_HEAD_PAD), np.float32)
    place[np.arange(MLA_ROPE_DIM), :, MLA_NOPE_DIM + np.arange(MLA_ROPE_DIM)] = 1.0
    place = jnp.asarray(place.reshape(MLA_ROPE_DIM, MLA_QK_PAD))
    row = lambda v: v.reshape(1, -1).astype(F32)

    n_blk = -(-(2 * N + N_EXPERTS * (RUN_CHUNK + ROW_BLOCK)) // ROW_BLOCK) + 1
    n_rows = n_blk * ROW_BLOCK

    for l in range(depth):
        dqg = per_row(jnp.tile(diff_q_g[l], n_dgroups) * (DIFF_HEAD_DIM ** -0.5 * LOG2E))
        dkg = row(jnp.tile(diff_k_g[l], n_dgroups))
        wuq = _pad_heads(mla_w_uq[l], MLA_QK_DIM).astype(BF16)
        mqg = per_row(_pad_heads(jnp.tile(mla_q_g[l], MLA_HEADS) * (MLA_QK_DIM ** -0.5 * LOG2E), MLA_QK_DIM))
        mkg = row(_pad_heads(jnp.tile(mla_k_g[l], MLA_HEADS), MLA_QK_DIM))
        wukv = mla_w_ukv[l].reshape(MLA_KV_RANK, MLA_HEADS, MLA_NOPE_DIM + MLA_V_DIM)
        wk_nope = _pad_heads(wukv[:, :, :MLA_NOPE_DIM].reshape(MLA_KV_RANK, -1), MLA_NOPE_DIM)
        wk = jnp.concatenate([wk_nope, place], axis=0).astype(BF16)
        wv = wukv[:, :, MLA_NOPE_DIM:].reshape(MLA_KV_RANK, MLA_WIDTH).astype(BF16)

        lru_in, dq, dk, dv, mq, mk, mv = _in_proj(
            x, row(norm1_g[l]), w_in, l, gsum, dqg, dkg, row(mla_cq_g[l]), wuq, mqg, row(mla_ckv_g[l]), wk, mkg, wv,
            cos_t, sin_t, cos_q, sin_q, ts)

        y_lru = _lru(lru_in, lru_conv_w[l], row(lru_conv_b[l]), _block_diag(lru_wa[l]).astype(BF16),
                     row(lru_ba[l]), _block_diag(lru_wx[l]).astype(BF16), row(lru_bx[l]), row(lru_lambda[l]),
                     row(lru_out_g[l]), tc)

        lambda_init = 0.8 - 0.6 * math.exp(-0.3 * l)
        y_diff = _diff_attn(dq, dk, dv, slopes, row(diff_lq1[l]), row(diff_lk1[l]), row(diff_lq2[l]),
                            row(diff_lk2[l]), row(jnp.tile(diff_sub_g[l], 2)), lambda_init)
        y_mla = _mla_attn(mq, mk, mv)

        wr = jnp.concatenate([router_g_w[l], router_e_w[l],
                              jnp.zeros((D, ROUTER_PAD - N_GROUPS - N_EXPERTS), F32)], axis=1)
        wrh = wr.astype(BF16)
        wrl = (wr - wrh.astype(F32)).astype(BF16)
        rb = jnp.pad(jnp.concatenate([router_g_b[l], router_e_b[l]]), (0, ROUTER_PAD - N_GROUPS - N_EXPERTS))
        x1, h2, eid, gate, cnt = _out_proj(
            x.reshape(N, D), y_lru.reshape(N, LRU_WIDTH), y_diff.reshape(N, DIFF_WIDTH), y_mla.reshape(N, MLA_WIDTH),
            row(mla_out_g[l]), w_out[l].astype(BF16), row(norm2_g[l]), wrh, wrl, row(rb), tm)

        counts = cnt[0, N_GROUPS:N_GROUPS + N_EXPERTS].astype(jnp.int32)
        padded = (counts + RUN_CHUNK + ROW_BLOCK - 1) // ROW_BLOCK * ROW_BLOCK
        pad_ends = jnp.cumsum(padded)
        pad_starts = pad_ends - padded
        blk_row0 = jnp.arange(n_blk, dtype=jnp.int32) * ROW_BLOCK
        blk_expert = jnp.minimum(jnp.sum((pad_ends[None, :] <= blk_row0[:, None]).astype(jnp.int32), axis=1),
                                 N_EXPERTS - 1)
        n_used = (pad_ends[-1:] // ROW_BLOCK).astype(jnp.int32)
        starts = jnp.pad(pad_starts.astype(F32), (0, LANES - N_EXPERTS)).reshape(1, LANES)
        tail_start = pad_starts + counts
        last = jnp.arange(N_EXPERTS) == N_EXPERTS - 1
        tail_ends = jnp.where(last, n_rows, pad_ends)
        aligned_start = (tail_start + RUN_CHUNK - 1) // RUN_CHUNK * RUN_CHUNK
        ztab = jnp.concatenate([tail_start, aligned_start, (tail_ends - aligned_start) // RUN_CHUNK]).astype(jnp.int32)

        loc, tab = _rank(eid, starts)
        tabs = tab.reshape(N // TOK_TILE, TAB_ROWS, LANES)[:, :3, :N_EXPERTS].reshape(-1)
        rows = _dispatch(tabs, ztab, h2, loc, n_rows)
        y_rows = _experts(blk_expert + l * N_EXPERTS, n_used, rows, exp_w1.reshape(-1, D, D_EXPERT),
                          exp_w3.reshape(-1, D, D_EXPERT), exp_w2.reshape(-1, D_EXPERT, D))
        x = _combine(tabs, x1, gate, loc, y_rows).reshape(B, S, D)
    return x
```

```python
import functools
import math

import jax
import jax.numpy as jnp
import numpy as np
from jax import lax
from jax.experimental import pallas as pl
from jax.experimental.pallas import tpu as pltpu

F32 = jnp.float32
BF16 = jnp.bfloat16

D_MODEL = 1024
CHUNK = 64
LRU_WIDTH = 256
CONV_WIDTH = 4
RG_C = 8.0
DIFF_HEADS = 6
DIFF_HEAD_DIM = 32
DIFF_V_DIM = 64
DIFF_QK_WIDTH = DIFF_HEADS * 2 * DIFF_HEAD_DIM
DIFF_WIDTH = DIFF_HEADS * DIFF_V_DIM
ALIBI_MAX_BIAS = 8.0
MLA_HEADS = 6
MLA_Q_RANK = 192
MLA_KV_RANK = 128
MLA_NOPE_DIM = 64
MLA_ROPE_DIM = 32
MLA_V_DIM = 64
MLA_QK_DIM = MLA_NOPE_DIM + MLA_ROPE_DIM
MLA_WIDTH = MLA_HEADS * MLA_V_DIM
ROPE_THETA = 10000.0
N_GROUPS = 4
EXPERTS_PER_GROUP = 8
N_EXPERTS = N_GROUPS * EXPERTS_PER_GROUP
D_EXPERT = 256
ROW_BLOCK = 256
EPS = 1e-6

LANES = 128
ROW_TILES = D_MODEL // LANES
MLA_HEAD_PAD = LANES
MLA_QK_PAD = MLA_HEADS * MLA_HEAD_PAD
VMEM_LIMIT = 56 * 1024 * 1024
NEG_BIG = -1e30
KEY_TILE = 256
QUERY_TILE = 512
TILES_PER_GROUP = 4
LOG2E = math.log2(math.e)

C_LRU = 0
C_DQ = 512
C_DK = 896
C_DV = 1280
C_CQ = 1664
C_CKV = 1856
C_KR = 1984
D_IN = 2016
D_IN_PAD = 2048


def _cparams(sem):
    return pltpu.CompilerParams(dimension_semantics=sem, vmem_limit_bytes=VMEM_LIMIT)


def _rms(v, width):
    return v * lax.rsqrt(jnp.sum(v * v, axis=-1, keepdims=True) * (1.0 / width) + EPS)


def _dot(a, b):
    return jnp.dot(a, b, preferred_element_type=F32)


def _split_bf16(v):
    hi = v.astype(BF16)
    lo = (v - hi.astype(F32)).astype(BF16)
    return hi, lo


def _in_proj_kernel(x_ref, g1_ref, w_ref, gsum_ref, dqg_ref, dkg_ref, cqg_ref, wuq_ref, mqg_ref, ckvg_ref,
                    wk_ref, mkg_ref, wv_ref, cos_ref, sin_ref, cosq_ref, sinq_ref,
                    lru_ref, dqt_ref, dk_ref, dvt_ref, mqt_ref, mk_ref, mvt_ref, w_bf):
    @pl.when((pl.program_id(0) == 0) & (pl.program_id(1) == 0))
    def _():
        w_bf[:, 0:D_IN] = w_ref[0].astype(BF16)
        w_bf[:, D_IN:] = jnp.zeros((D_MODEL, D_IN_PAD - D_IN), BF16)

    x = x_ref[0]
    hn = (_rms(x, D_MODEL) * g1_ref[...]).astype(BF16)
    p = _dot(hn, w_bf[...])
    lru_ref[0] = p[:, C_LRU:C_LRU + 2 * LRU_WIDTH]

    gsum = gsum_ref[...]

    def group_norm32(v, gain):
        hi, lo = _split_bf16(v * v)
        ss = _dot(hi, gsum) + _dot(lo, gsum)
        return v * lax.rsqrt(ss * (1.0 / DIFF_HEAD_DIM) + EPS) * gain

    def store_tiles(ref, vt):
        for c in range(vt.shape[1] // KEY_TILE):
            ref[0, c] = vt[:, c * KEY_TILE:(c + 1) * KEY_TILE].astype(BF16)

    t = x.shape[0]

    def along_lanes(per_feature):
        return jnp.concatenate([per_feature] * (t // LANES), axis=1)

    dq_t = p[:, C_DQ:C_DQ + DIFF_QK_WIDTH].T.reshape(DIFF_QK_WIDTH // DIFF_HEAD_DIM, DIFF_HEAD_DIM, t)
    dq_ss = jnp.sum(dq_t * dq_t, axis=1, keepdims=True) * (1.0 / DIFF_HEAD_DIM)
    dq_n = (dq_t * lax.rsqrt(dq_ss + EPS)).reshape(DIFF_QK_WIDTH, t) * along_lanes(dqg_ref[...])
    dqt_ref[0] = dq_n.astype(BF16)
    dk_ref[0] = group_norm32(p[:, C_DK:C_DK + DIFF_QK_WIDTH], dkg_ref[...]).astype(BF16)
    store_tiles(dvt_ref, p[:, C_DV:C_DV + DIFF_WIDTH].T)

    cos = cos_ref[...]
    sin = sin_ref[...]
    lane = lax.broadcasted_iota(jnp.int32, cos.shape, 1)
    first_half = lane < MLA_NOPE_DIM + MLA_ROPE_DIM // 2

    def head_norm_rotary(v, gain):
        outs = []
        for h in range(MLA_HEADS):
            c = v[:, h * MLA_HEAD_PAD:(h + 1) * MLA_HEAD_PAD]
            c = _rms(c, MLA_QK_DIM) * gain[:, h * MLA_HEAD_PAD:(h + 1) * MLA_HEAD_PAD]
            swapped = jnp.where(first_half, pltpu.roll(c, MLA_HEAD_PAD - MLA_ROPE_DIM // 2, 1),
                                pltpu.roll(c, MLA_ROPE_DIM // 2, 1))
            outs.append(c * cos + swapped * sin)
        return jnp.concatenate(outs, axis=1)

    cq = (_rms(p[:, C_CQ:C_CQ + MLA_Q_RANK], MLA_Q_RANK) * cqg_ref[...]).astype(BF16)
    q_t = _dot(cq, wuq_ref[...]).T
    cos_q, sin_q = cosq_ref[...], sinq_ref[...]
    gain_q = along_lanes(mqg_ref[...])
    half = MLA_ROPE_DIM // 2
    heads_t = []
    for h in range(MLA_HEADS):
        rows = slice(h * MLA_HEAD_PAD, (h + 1) * MLA_HEAD_PAD)
        c = q_t[rows]
        ss = jnp.sum(c * c, axis=0, keepdims=True) * (1.0 / MLA_QK_DIM)
        c = c * lax.rsqrt(ss + EPS) * gain_q[rows]
        swapped = jnp.concatenate([c[:MLA_NOPE_DIM], c[MLA_NOPE_DIM + half:MLA_QK_DIM],
                                   c[MLA_NOPE_DIM:MLA_NOPE_DIM + half], c[MLA_QK_DIM:]], axis=0)
        heads_t.append((c * cos_q + swapped * sin_q).astype(BF16))
    mqt_ref[0] = jnp.concatenate(heads_t, axis=0)

    ckv = (_rms(p[:, C_CKV:C_CKV + MLA_KV_RANK], MLA_KV_RANK) * ckvg_ref[...]).astype(BF16)
    kcat = jnp.concatenate([ckv, p[:, C_KR:C_KR + MLA_ROPE_DIM].astype(BF16)], axis=1)
    mk_ref[0] = head_norm_rotary(_dot(kcat, wk_ref[...]), mkg_ref[...]).astype(BF16)
    store_tiles(mvt_ref, _dot(ckv, wv_ref[...]).T)


def _in_proj(x, g1, w_all, layer, gsum, dqg, dkg, cqg, wuq, mqg, ckvg, wk, mkg, wv, cos_t, sin_t, cos_q, sin_q, ts):
    B, S, _ = x.shape
    full = lambda a: pl.BlockSpec(a.shape, lambda b, i: (0,) * a.ndim)
    w_spec = pl.BlockSpec((1,) + w_all.shape[1:], lambda b, i: (layer, 0, 0))
    tok = lambda width: pl.BlockSpec((1, ts, width), lambda b, i: (b, i, 0))
    tok_t = lambda width: pl.BlockSpec((1, width, ts), lambda b, i: (b, 0, i))
    nkt = ts // KEY_TILE
    tiles_t = lambda width: pl.BlockSpec((1, nkt, width, KEY_TILE), lambda b, i: (b, i, 0, 0))
    rot = pl.BlockSpec((ts, LANES), lambda b, i: (i, 0))
    rot_t = pl.BlockSpec((LANES, ts), lambda b, i: (0, i))
    shp = lambda shape, dt: jax.ShapeDtypeStruct(shape, dt)
    return pl.pallas_call(
        _in_proj_kernel,
        grid=(B, S // ts),
        in_specs=[tok(D_MODEL), full(g1), w_spec, full(gsum), full(dqg), full(dkg), full(cqg), full(wuq),
                  full(mqg), full(ckvg), full(wk), full(mkg), full(wv), rot, rot, rot_t, rot_t],
        out_specs=[tok(2 * LRU_WIDTH), tok_t(DIFF_QK_WIDTH), tok(DIFF_QK_WIDTH), tiles_t(DIFF_WIDTH),
                   tok_t(MLA_QK_PAD), tok(MLA_QK_PAD), tiles_t(MLA_WIDTH)],
        out_shape=[shp((B, S, 2 * LRU_WIDTH), F32), shp((B, DIFF_QK_WIDTH, S), BF16), shp((B, S, DIFF_QK_WIDTH), BF16),
                   shp((B, S // KEY_TILE, DIFF_WIDTH, KEY_TILE), BF16), shp((B, MLA_QK_PAD, S), BF16),
                   shp((B, S, MLA_QK_PAD), BF16), shp((B, S // KEY_TILE, MLA_WIDTH, KEY_TILE), BF16)],
        scratch_shapes=[pltpu.VMEM((D_MODEL, D_IN_PAD), BF16)],
        compiler_params=_cparams(("arbitrary", "arbitrary")),
        name="in_proj",
    )(x, g1, w_all, gsum, dqg, dkg, cqg, wuq, mqg, ckvg, wk, mkg, wv, cos_t, sin_t, cos_q, sin_q)


LRU_HALO = 8


def _lru_kernel(blk_ref, cw_ref, cb_ref, wa_ref, ba_ref, wx_ref, bx_ref, lam_ref, og_ref, y_ref,
                ubuf, a_scr, b_scr, h_scr):
    B, tc, _ = blk_ref.shape
    W = LRU_WIDTH

    @pl.when(pl.program_id(0) == 0)
    def _():
        ubuf[:, 0:LRU_HALO, :] = jnp.zeros((B, LRU_HALO, W), F32)
        h_scr[...] = jnp.zeros(h_scr.shape, F32)

    ubuf[:, LRU_HALO:, :] = blk_ref[:, :, 0:W]
    xc = cb_ref[...][None]
    for j in range(CONV_WIDTH):
        off = LRU_HALO - (CONV_WIDTH - 1) + j
        xc = xc + cw_ref[j:j + 1, :][None] * ubuf[:, off:off + tc, :]
    ubuf[:, 0:LRU_HALO, :] = ubuf[:, tc:tc + LRU_HALO, :]

    x2 = xc.reshape(B * tc, W)
    xb = x2.astype(BF16)
    r = jax.nn.sigmoid(_dot(xb, wa_ref[...]) + ba_ref[...])
    gi = jax.nn.sigmoid(_dot(xb, wx_ref[...]) + bx_ref[...])
    nl = -lam_ref[...]
    softplus = jnp.maximum(nl, 0.0) + jnp.log(1.0 + jnp.exp(-jnp.abs(nl)))
    a = jnp.exp(-RG_C * r * softplus)
    bb = jnp.sqrt(1.0 - a * a) * gi * x2
    a_scr[...] = a.reshape(B, tc, W)
    b_scr[...] = bb.reshape(B, tc, W)

    def step(t, h):
        h = a_scr[:, pl.ds(t, 1), :] * h + b_scr[:, pl.ds(t, 1), :]
        b_scr[:, pl.ds(t, 1), :] = h
        return h

    h_scr[...] = lax.fori_loop(0, tc, step, h_scr[...], unroll=8)

    gate = blk_ref[:, :, W:2 * W]
    gelu = 0.5 * gate * (1.0 + jnp.tanh(math.sqrt(2.0 / math.pi) * (gate + 0.044715 * gate * gate * gate)))
    y = b_scr[...] * gelu
    y_ref[...] = (_rms(y, W) * og_ref[...][None]).astype(y_ref.dtype)


def _lru(lru_in, cw, cb, wa, ba, wx, bx, lam, og, tc):
    B, S, _ = lru_in.shape
    W = LRU_WIDTH
    full = lambda a: pl.BlockSpec(a.shape, lambda i: (0,) * a.ndim)
    return pl.pallas_call(
        _lru_kernel,
        grid=(S // tc,),
        in_specs=[pl.BlockSpec((B, tc, 2 * W), lambda i: (0, i, 0)), full(cw), full(cb), full(wa), full(ba),
                  full(wx), full(bx), full(lam), full(og)],
        out_specs=pl.BlockSpec((B, tc, W), lambda i: (0, i, 0)),
        out_shape=jax.ShapeDtypeStruct((B, S, W), BF16),
        scratch_shapes=[pltpu.VMEM((B, tc + LRU_HALO, W), F32), pltpu.VMEM((B, tc, W), F32),
                        pltpu.VMEM((B, tc, W), F32), pltpu.VMEM((B, 1, W), F32)],
        compiler_params=_cparams(("arbitrary",)),
        name="lru",
    )(lru_in, cw, cb, wa, ba, wx, bx, lam, og)


def _flash_maps(qt_list, k_ref, vt_ref, k_lanes, bias_of, tile_shift, m_scr, l_scr, acc_scr, tiles_per_step):
    qi = pl.program_id(2)
    n_maps = len(qt_list)
    tq = qt_list[0].shape[1]
    maps = range(n_maps)

    own = tq // KEY_TILE
    first_own = own * qi
    last = first_own + own - 1

    def biased_scores(j):
        jc = jnp.minimum(j, last)
        start = pl.multiple_of(jc * KEY_TILE, KEY_TILE)
        kind = jnp.clip(jc - first_own + 1, 0, own)
        return tuple(_dot(k_ref[0, pl.ds(start, KEY_TILE), k_lanes[i]], qt_list[i]) + bias_of(i, kind)
                     for i in maps)

    for i in maps:
        m_scr[i] = jnp.full((1, tq), NEG_BIG, F32)
        l_scr[i] = jnp.zeros((1, tq), F32)
        acc_scr[i] = jnp.zeros(acc_scr.shape[1:], F32)

    def update(j, s_tile):
        vt = vt_ref[0, jnp.minimum(j, last)]
        for i in maps:
            s = s_tile[i]
            c = tile_shift(i, first_own - j)
            m_old = m_scr[i]
            m_new = jnp.maximum(m_old, jnp.max(s, axis=0, keepdims=True) + c)
            alpha = jnp.exp2(m_old - m_new)
            p = jnp.exp2(s - (m_new - c))
            m_scr[i] = m_new
            l_scr[i] = alpha * l_scr[i] + jnp.sum(p, axis=0, keepdims=True)
            acc_scr[i] = alpha * acc_scr[i] + _dot(vt, p.astype(BF16))

    def one_tile(j, s_tile):
        s_ahead = biased_scores(j + 1)
        update(j, s_tile)
        return s_ahead

    def group_of(size, first):
        def body(t, s_tile):
            for u in range(size):
                s_tile = one_tile(first + size * t + u, s_tile)
            return s_tile
        return body

    rest_size = math.gcd(tiles_per_step, own)
    n_groups = (last + 1) // tiles_per_step
    s_rest = lax.fori_loop(0, n_groups, group_of(tiles_per_step, 0), biased_scores(0))
    done = n_groups * tiles_per_step
    lax.fori_loop(0, (last + 1 - done) // rest_size, group_of(rest_size, done), s_rest)


def _own_tile_geometry(tq, u):
    krow = lax.broadcasted_iota(jnp.int32, (KEY_TILE, tq), 0) + u * KEY_TILE
    qcol = lax.broadcasted_iota(jnp.int32, (KEY_TILE, tq), 1)
    return krow.astype(F32), qcol.astype(F32), (krow // CHUNK) <= (qcol // CHUNK)


def _diff_attn_kernel(qt_ref, k_ref, vt_ref, slope_ref, lq1_ref, lk1_ref, lq2_ref, lk2_ref, subg_ref, o_ref,
                      m_scr, l_scr, acc_scr, bias_scr, *, lambda_init, tq):
    qt = qt_ref[0]
    feat = lax.broadcasted_iota(jnp.int32, qt.shape, 0)
    zero = jnp.zeros_like(qt)
    qt_list = [jnp.where(feat // DIFF_HEAD_DIM == i, qt, zero) for i in range(4)]
    slope = [slope_ref[0, 0:1, hh:hh + 1] for hh in range(2)]

    @pl.when(pl.program_id(2) == 0)
    def _():
        for hh in range(2):
            bias_scr[hh, 0] = slope[hh] * _own_tile_geometry(tq, 0)[0]
            for u in range(tq // KEY_TILE):
                kf, qf, allowed = _own_tile_geometry(tq, u)
                own_bias = slope[hh] * (qf - jnp.abs(qf - kf) - float(u * KEY_TILE))
                bias_scr[hh, u + 1] = jnp.where(allowed, own_bias, NEG_BIG)

    def tile_shift(i, n_tiles):
        return -slope[i // 2] * (n_tiles * KEY_TILE).astype(F32)

    _flash_maps(qt_list, k_ref, vt_ref, [slice(None)] * 4, lambda i, kind: bias_scr[i // 2, kind],
                tile_shift, m_scr, l_scr, acc_scr, TILES_PER_GROUP)

    lam = (jnp.exp(jnp.sum(lq1_ref[...] * lk1_ref[...], axis=-1, keepdims=True))
           - jnp.exp(jnp.sum(lq2_ref[...] * lk2_ref[...], axis=-1, keepdims=True)) + lambda_init)
    orow = lax.broadcasted_iota(jnp.int32, (LANES, tq), 0)
    out_t = jnp.zeros((LANES, tq), F32)
    for hh in range(2):
        o = (acc_scr[2 * hh] * (1.0 / l_scr[2 * hh])
             - acc_scr[2 * hh + 1] * (lam / l_scr[2 * hh + 1]))
        mine = (orow // DIFF_V_DIM) == hh
        ms = jnp.sum(jnp.where(mine, o * o, 0.0), axis=0, keepdims=True) * (1.0 / DIFF_V_DIM)
        out_t = jnp.where(mine, o * lax.rsqrt(ms + EPS), out_t)
    o_ref[0] = (out_t.T * subg_ref[...] * (1.0 - lambda_init)).astype(o_ref.dtype)


def _diff_attn(dqt, dk, dvt, slopes, lq1, lk1, lq2, lk2, subg, lambda_init):
    B, S, _ = dk.shape
    tq = min(QUERY_TILE, S)
    n_pairs = DIFF_HEADS // 2
    full = lambda a: pl.BlockSpec(a.shape, lambda b, p, i: (0,) * a.ndim)
    kern = functools.partial(_diff_attn_kernel, lambda_init=lambda_init, tq=tq)
    return pl.pallas_call(
        kern,
        grid=(B, n_pairs, S // tq),
        in_specs=[pl.BlockSpec((1, LANES, tq), lambda b, p, i: (b, p, i)),
                  pl.BlockSpec((1, S, LANES), lambda b, p, i: (b, 0, p)),
                  pl.BlockSpec((1, S // KEY_TILE, LANES, KEY_TILE), lambda b, p, i: (b, 0, p, 0)),
                  pl.BlockSpec((1, 1, LANES), lambda b, p, i: (p, 0, 0)),
                  full(lq1), full(lk1), full(lq2), full(lk2), full(subg)],
        out_specs=pl.BlockSpec((1, tq, LANES), lambda b, p, i: (b, i, p)),
        out_shape=jax.ShapeDtypeStruct((B, S, DIFF_WIDTH), BF16),
        scratch_shapes=[pltpu.VMEM((4, 1, tq), F32), pltpu.VMEM((4, 1, tq), F32), pltpu.VMEM((4, LANES, tq), F32),
                        pltpu.VMEM((2, 1 + tq // KEY_TILE, KEY_TILE, tq), F32)],
        compiler_params=_cparams(("parallel", "parallel", "arbitrary")),
        name="diff_attn",
    )(dqt, dk, dvt, slopes, lq1, lk1, lq2, lk2, subg)


def _mla_attn_kernel(qt_ref, k_ref, vt_ref, o_ref, m_scr, l_scr, acc_scr, bias_scr, *, tq):
    qt = qt_ref[0]
    qt_list = [qt[hh * MLA_HEAD_PAD:(hh + 1) * MLA_HEAD_PAD, :] for hh in range(2)]
    k_lanes = [slice(hh * MLA_HEAD_PAD, (hh + 1) * MLA_HEAD_PAD) for hh in range(2)]

    @pl.when(pl.program_id(2) == 0)
    def _():
        bias_scr[0] = jnp.zeros((KEY_TILE, tq), F32)
        for u in range(tq // KEY_TILE):
            bias_scr[u + 1] = jnp.where(_own_tile_geometry(tq, u)[2], 0.0, NEG_BIG)

    _flash_maps(qt_list, k_ref, vt_ref, k_lanes, lambda i, kind: bias_scr[kind], lambda i, n: 0.0,
                m_scr, l_scr, acc_scr, TILES_PER_GROUP)
    orow = lax.broadcasted_iota(jnp.int32, (LANES, tq), 0)
    out_t = jnp.where(orow < MLA_V_DIM, acc_scr[0] * (1.0 / l_scr[0]), acc_scr[1] * (1.0 / l_scr[1]))
    o_ref[0] = out_t.T.astype(o_ref.dtype)


def _mla_attn(mqt, mk, mvt):
    B, S, _ = mk.shape
    tq = min(QUERY_TILE, S)
    n_pairs = MLA_HEADS // 2
    kern = functools.partial(_mla_attn_kernel, tq=tq)
    return pl.pallas_call(
        kern,
        grid=(B, n_pairs, S // tq),
        in_specs=[pl.BlockSpec((1, 2 * MLA_HEAD_PAD, tq), lambda b, p, i: (b, p, i)),
                  pl.BlockSpec((1, S, 2 * MLA_HEAD_PAD), lambda b, p, i: (b, 0, p)),
                  pl.BlockSpec((1, S // KEY_TILE, LANES, KEY_TILE), lambda b, p, i: (b, 0, p, 0))],
        out_specs=pl.BlockSpec((1, tq, LANES), lambda b, p, i: (b, i, p)),
        out_shape=jax.ShapeDtypeStruct((B, S, MLA_WIDTH), F32),
        scratch_shapes=[pltpu.VMEM((2, 1, tq), F32), pltpu.VMEM((2, 1, tq), F32), pltpu.VMEM((2, LANES, tq), F32),
                        pltpu.VMEM((1 + tq // KEY_TILE, KEY_TILE, tq), F32)],
        compiler_params=_cparams(("parallel", "parallel", "arbitrary")),
        name="mla_attn",
    )(mqt, mk, mvt)


ROUTER_PAD = LANES


def _out_proj_kernel(x_ref, ylru_ref, ydiff_ref, ymla_ref, mlag_ref, wo_ref, g2_ref, wrh_ref, wrl_ref, rb_ref,
                     x1_ref, h2_ref, eid_ref, gate_ref, cnt_ref):
    @pl.when(pl.program_id(0) == 0)
    def _():
        cnt_ref[...] = jnp.zeros(cnt_ref.shape, F32)

    ymla = (_rms(ymla_ref[...], MLA_WIDTH) * mlag_ref[...]).astype(BF16)
    x1 = (x_ref[...]
          + _dot(ylru_ref[...], wo_ref[0:LRU_WIDTH, :])
          + _dot(ydiff_ref[...], wo_ref[LRU_WIDTH:LRU_WIDTH + DIFF_WIDTH, :])
          + _dot(ymla, wo_ref[LRU_WIDTH + DIFF_WIDTH:, :]))
    x1_ref[...] = x1

    h2 = _rms(x1, D_MODEL) * g2_ref[...]
    hi, lo = _split_bf16(h2)
    h2_ref[...] = hi
    wrh = wrh_ref[...]
    logits = _dot(hi, wrh) + _dot(hi, wrl_ref[...]) + _dot(lo, wrh) + rb_ref[...]

    lane = lax.broadcasted_iota(jnp.int32, logits.shape, 1)
    big = jnp.int32(1 << 20)
    is_group = lane < N_GROUPS
    gl = jnp.where(is_group, logits, NEG_BIG)
    gmax = jnp.max(gl, axis=-1, keepdims=True)
    g_idx = jnp.min(jnp.where(gl == gmax, lane, big), axis=-1, keepdims=True)
    pg_top = 1.0 / jnp.sum(jnp.where(is_group, jnp.exp(gl - gmax), 0.0), axis=-1, keepdims=True)

    e_lane = lane - N_GROUPS
    in_group = (e_lane >= 0) & (e_lane < N_EXPERTS) & ((e_lane // EXPERTS_PER_GROUP) == g_idx)
    el = jnp.where(in_group, logits, NEG_BIG)
    m1 = jnp.max(el, axis=-1, keepdims=True)
    i1 = jnp.min(jnp.where(el == m1, lane, big), axis=-1, keepdims=True)
    el2 = jnp.where(lane == i1, NEG_BIG, el)
    m2 = jnp.max(el2, axis=-1, keepdims=True)
    i2 = jnp.min(jnp.where(el2 == m2, lane, big), axis=-1, keepdims=True)
    e2 = jnp.exp(m2 - m1)
    g1 = pg_top / (1.0 + e2)
    g2 = pg_top * e2 / (1.0 + e2)

    two = lax.broadcasted_iota(jnp.int32, eid_ref.shape, 1)
    eid_ref[...] = jnp.where(two == 0, i1 - N_GROUPS, i2 - N_GROUPS)
    gate_ref[...] = jnp.where(two == 0, g1, g2)
    onehot = ((e_lane == i1 - N_GROUPS) | (e_lane == i2 - N_GROUPS)).astype(F32)
    cnt_ref[...] += jnp.sum(onehot, axis=0, keepdims=True)


def _out_proj(x, ylru, ydiff, ymla, mlag, wo, g2, wrh, wrl, rb, tm):
    N = x.shape[0]
    full = lambda a: pl.BlockSpec(a.shape, lambda i: (0,) * a.ndim)
    tok = lambda width: pl.BlockSpec((tm, width), lambda i: (i, 0))
    return pl.pallas_call(
        _out_proj_kernel,
        grid=(N // tm,),
        in_specs=[tok(D_MODEL), tok(LRU_WIDTH), tok(DIFF_WIDTH), tok(MLA_WIDTH), full(mlag), full(wo), full(g2),
                  full(wrh), full(wrl), full(rb)],
        out_specs=[tok(D_MODEL), tok(D_MODEL), tok(2), tok(2), pl.BlockSpec((1, ROUTER_PAD), lambda i: (0, 0))],
        out_shape=[jax.ShapeDtypeStruct((N, D_MODEL), F32), jax.ShapeDtypeStruct((N, D_MODEL), BF16),
                   jax.ShapeDtypeStruct((N, 2), jnp.int32),
                   jax.ShapeDtypeStruct((N, 2), F32), jax.ShapeDtypeStruct((1, ROUTER_PAD), F32)],
        compiler_params=_cparams(("arbitrary",)),
        name="out_proj",
    )(x, ylru, ydiff, ymla, mlag, wo, g2, wrh, wrl, rb)


TOK_TILE = 256
RUN_CHUNK = 8
TAB_ROWS = 8


def _rank_kernel(eid_ref, start_ref, loc_ref, tab_ref, carry):
    tm = eid_ref.shape[0]

    @pl.when(pl.program_id(0) == 0)
    def _():
        carry[...] = start_ref[...]

    lane = lax.broadcasted_iota(jnp.int32, (tm, LANES), 1)
    r = lax.broadcasted_iota(jnp.int32, (tm, tm), 0)
    c = lax.broadcasted_iota(jnp.int32, (tm, tm), 1)
    lower = (c < r).astype(BF16)
    eid = eid_ref[...]
    onehot = [lane == eid[:, k:k + 1] for k in range(2)]
    cnt = [jnp.sum(oh.astype(F32), axis=0, keepdims=True) for oh in onehot]
    cnt_tile = cnt[0] + cnt[1]

    chunks = jnp.floor((cnt_tile + (RUN_CHUNK - 1)) * (1.0 / RUN_CHUNK))
    lane8 = lax.broadcasted_iota(jnp.int32, (TAB_ROWS, LANES), 1)
    incl = jnp.broadcast_to(chunks, (TAB_ROWS, LANES))
    shift = 1
    while shift < LANES:
        incl = incl + jnp.where(lane8 >= shift, pltpu.roll(incl, shift, 1), 0.0)
        shift *= 2
    offs = (incl[0:1] - chunks) * RUN_CHUNK

    locs = []
    for k in range(2):
        before = _dot(lower, onehot[k].astype(BF16))
        first = offs if k == 0 else offs + cnt[0]
        locs.append(jnp.sum(jnp.where(onehot[k], before + first, 0.0), axis=-1, keepdims=True))
    two = lax.broadcasted_iota(jnp.int32, loc_ref.shape, 1)
    loc_ref[...] = jnp.where(two == 0, locs[0], locs[1]).astype(jnp.int32)

    row8 = lax.broadcasted_iota(jnp.int32, (TAB_ROWS, LANES), 0)
    base = carry[...]
    tab = jnp.where(row8 == 0, cnt_tile, jnp.where(row8 == 1, offs, jnp.where(row8 == 2, base, 0.0)))
    tab_ref[...] = tab.astype(jnp.int32)
    carry[...] = base + cnt_tile


def _rank(eid, starts):
    N = eid.shape[0]
    n_tiles = N // TOK_TILE
    return pl.pallas_call(
        _rank_kernel,
        grid=(n_tiles,),
        in_specs=[pl.BlockSpec((TOK_TILE, 2), lambda i: (i, 0)), pl.BlockSpec((1, LANES), lambda i: (0, 0))],
        out_specs=[pl.BlockSpec((TOK_TILE, 2), lambda i: (i, 0)), pl.BlockSpec((TAB_ROWS, LANES), lambda i: (i, 0))],
        out_shape=[jax.ShapeDtypeStruct((N, 2), jnp.int32),
                   jax.ShapeDtypeStruct((n_tiles * TAB_ROWS, LANES), jnp.int32)],
        scratch_shapes=[pltpu.VMEM((1, LANES), F32)],
        compiler_params=_cparams(("arbitrary",)),
        name="rank",
    )(eid, starts)


SORT_ROWS = 2 * TOK_TILE + N_EXPERTS * RUN_CHUNK


def _run_chunks(tab_ref, tile, e):
    cnt = tab_ref[(tile * 3 + 0) * N_EXPERTS + e]
    offs = tab_ref[(tile * 3 + 1) * N_EXPERTS + e]
    base = tab_ref[(tile * 3 + 2) * N_EXPERTS + e]
    return lax.shift_right_logical(cnt + (RUN_CHUNK - 1), RUN_CHUNK.bit_length() - 1), offs, base


def _rows(ref, row, n):
    return ref.at[pl.ds(pl.multiple_of(row * ROW_TILES, ROW_TILES), n * ROW_TILES)]


def _for_each_chunk(tab_ref, tile, fn):
    for e in range(N_EXPERTS):
        chunks, offs, base = _run_chunks(tab_ref, tile, e)

        def one(k, carry):
            fn(offs + k * RUN_CHUNK, base + k * RUN_CHUNK)
            return carry

        lax.fori_loop(0, chunks, one, 0)


def _drain_chunks(tab_ref, tile, wait_one):
    total = 0
    for e in range(N_EXPERTS):
        total = total + _run_chunks(tab_ref, tile, e)[0]

    def one(k, carry):
        wait_one()
        return carry

    lax.fori_loop(0, total, one, 0)


def _dispatch_kernel(tab_ref, ztab_ref, h_ref, loc_ref, rows_hbm, sbuf, zbuf, sems, zsem):
    i = pl.program_id(0)
    n_tiles = pl.num_programs(0)
    slot = lax.rem(i, 2)
    T = TOK_TILE

    def run_copy(s, offs_row, base_row):
        return pltpu.make_async_copy(_rows(sbuf, s * SORT_ROWS + offs_row, RUN_CHUNK),
                                     _rows(rows_hbm, base_row, RUN_CHUNK), sems.at[s])

    @pl.when(i == 0)
    def _():
        zbuf[...] = jnp.zeros(zbuf.shape, F32)

        def zero_chunk(first_row):
            return pltpu.make_async_copy(zbuf, _rows(rows_hbm, first_row, RUN_CHUNK), zsem)

        def done(k, carry):
            zero_chunk(0).wait()
            return carry

        for e in range(N_EXPERTS):
            zero_chunk(ztab_ref[e]).start()
        lax.fori_loop(0, N_EXPERTS, done, 0)
        total = 0
        for e in range(N_EXPERTS):
            start, chunks = ztab_ref[N_EXPERTS + e], ztab_ref[2 * N_EXPERTS + e]
            total = total + chunks

            def fill(k, carry, start=start):
                zero_chunk(start + k * RUN_CHUNK).start()
                return carry

            lax.fori_loop(0, chunks, fill, 0)
        lax.fori_loop(0, total, done, 0)

    locf = loc_ref[...].astype(F32)
    loc_lanes = [jnp.broadcast_to(locf[:, k:k + 1], (T, LANES)).T[0:1, :] for k in range(2)]
    pos = lax.broadcasted_iota(jnp.int32, (SORT_ROWS, T), 0).astype(F32)
    perm = jnp.where((pos == loc_lanes[0]) | (pos == loc_lanes[1]), 1.0, 0.0).astype(BF16)
    sorted_rows = _dot(perm, h_ref[...])
    for c in range(ROW_TILES):
        sbuf[pl.ds(slot * (SORT_ROWS * ROW_TILES) + c, SORT_ROWS, stride=ROW_TILES), :] = (
            sorted_rows[:, c * LANES:(c + 1) * LANES])

    @pl.when(i > 0)
    def _():
        _drain_chunks(tab_ref, i - 1, lambda: run_copy(1 - slot, 0, 0).wait())

    _for_each_chunk(tab_ref, i, lambda offs_row, base_row: run_copy(slot, offs_row, base_row).start())

    @pl.when(i == n_tiles - 1)
    def _():
        _drain_chunks(tab_ref, i, lambda: run_copy(slot, 0, 0).wait())


def _dispatch(tabs, ztab, h2, loc, n_rows):
    N = h2.shape[0]
    return pl.pallas_call(
        _dispatch_kernel,
        grid_spec=pltpu.PrefetchScalarGridSpec(
            num_scalar_prefetch=2,
            grid=(N // TOK_TILE,),
            in_specs=[pl.BlockSpec((TOK_TILE, D_MODEL), lambda i, t, z: (i, 0)),
                      pl.BlockSpec((TOK_TILE, 2), lambda i, t, z: (i, 0))],
            out_specs=pl.BlockSpec(memory_space=pl.ANY),
            scratch_shapes=[pltpu.VMEM((2 * SORT_ROWS * ROW_TILES, LANES), F32),
                            pltpu.VMEM((RUN_CHUNK * ROW_TILES, LANES), F32),
                            pltpu.SemaphoreType.DMA((2,)), pltpu.SemaphoreType.DMA(())],
        ),
        out_shape=jax.ShapeDtypeStruct((n_rows * ROW_TILES, LANES), F32),
        compiler_params=_cparams(("arbitrary",)),
        name="dispatch",
    )(tabs, ztab, h2, loc)


def _expert_kernel(be_ref, nused_ref, rows_ref, w1_ref, w3_ref, w2_ref, y_ref):
    del be_ref
    used = pl.program_id(0) < nused_ref[0]

    @pl.when(used)
    def _():
        h = jnp.concatenate([rows_ref[pl.ds(c, ROW_BLOCK, stride=ROW_TILES), :] for c in range(ROW_TILES)],
                            axis=1).astype(BF16)
        a = _dot(h, w1_ref[0].astype(BF16))
        b = _dot(h, w3_ref[0].astype(BF16))
        z = (a * jax.nn.sigmoid(a) * b).astype(BF16)
        y = _dot(z, w2_ref[0].astype(BF16))
        for c in range(ROW_TILES):
            y_ref[pl.ds(c, ROW_BLOCK, stride=ROW_TILES), :] = y[:, c * LANES:(c + 1) * LANES]

    @pl.when(jnp.logical_not(used))
    def _():
        y_ref[...] = jnp.zeros(y_ref.shape, y_ref.dtype)


def _experts(blk_expert, n_used, rows, w1, w3, w2):
    n_blk = blk_expert.shape[0]
    row_blk = lambda i, be, nu: (jnp.minimum(i, nu[0] - 1), 0)
    wspec = lambda shape: pl.BlockSpec((1,) + shape, lambda i, be, nu: (be[i], 0, 0))
    return pl.pallas_call(
        _expert_kernel,
        grid_spec=pltpu.PrefetchScalarGridSpec(
            num_scalar_prefetch=2,
            grid=(n_blk,),
            in_specs=[pl.BlockSpec((ROW_BLOCK * ROW_TILES, LANES), row_blk),
                      wspec((D_MODEL, D_EXPERT)), wspec((D_MODEL, D_EXPERT)), wspec((D_EXPERT, D_MODEL))],
            out_specs=pl.BlockSpec((ROW_BLOCK * ROW_TILES, LANES), lambda i, be, nu: (i, 0)),
        ),
        out_shape=jax.ShapeDtypeStruct((n_blk * ROW_BLOCK * ROW_TILES, LANES), F32),
        compiler_params=_cparams(("arbitrary",)),
        name="experts",
    )(blk_expert, n_used, rows, w1, w3, w2)


def _combine_kernel(tab_ref, x1_ref, gate_ref, loc_ref, y_hbm, o_ref, ybuf, sems):
    i = pl.program_id(0)
    n_tiles = pl.num_programs(0)
    slot = lax.rem(i, 2)
    T = TOK_TILE

    def run_copy(s, offs_row, base_row):
        return pltpu.make_async_copy(_rows(y_hbm, base_row, RUN_CHUNK),
                                     _rows(ybuf, s * SORT_ROWS + offs_row, RUN_CHUNK), sems.at[s])

    def fetch(tile, s):
        _for_each_chunk(tab_ref, tile, lambda offs_row, base_row: run_copy(s, offs_row, base_row).start())

    @pl.when(i == 0)
    def _():
        ybuf[...] = jnp.zeros(ybuf.shape, F32)
        fetch(0, 0)

    @pl.when(i + 1 < n_tiles)
    def _():
        fetch(i + 1, 1 - slot)

    _drain_chunks(tab_ref, i, lambda: run_copy(slot, 0, 0).wait())

    y_sorted = jnp.concatenate([ybuf[pl.ds(slot * (SORT_ROWS * ROW_TILES) + c, SORT_ROWS, stride=ROW_TILES), :]
                                for c in range(ROW_TILES)], axis=1).astype(BF16)
    gate = gate_ref[...]
    loc = loc_ref[...]
    pos = lax.broadcasted_iota(jnp.int32, (T, SORT_ROWS), 1)
    g = (jnp.where(pos == loc[:, 0:1], gate[:, 0:1], 0.0) + jnp.where(pos == loc[:, 1:2], gate[:, 1:2], 0.0))
    g_hi, g_lo = _split_bf16(g)
    o_ref[...] = x1_ref[...] + _dot(g_hi, y_sorted) + _dot(g_lo, y_sorted)


def _combine(tabs, x1, gate, loc, y_rows):
    N = gate.shape[0]
    tok = lambda width: pl.BlockSpec((TOK_TILE, width), lambda i, t: (i, 0))
    return pl.pallas_call(
        _combine_kernel,
        grid_spec=pltpu.PrefetchScalarGridSpec(
            num_scalar_prefetch=1,
            grid=(N // TOK_TILE,),
            in_specs=[tok(D_MODEL), tok(2), tok(2), pl.BlockSpec(memory_space=pl.ANY)],
            out_specs=tok(D_MODEL),
            scratch_shapes=[pltpu.VMEM((2 * SORT_ROWS * ROW_TILES, LANES), F32), pltpu.SemaphoreType.DMA((2,))],
        ),
        out_shape=jax.ShapeDtypeStruct((N, D_MODEL), F32),
        compiler_params=_cparams(("arbitrary",)),
        name="combine",
    )(tabs, x1, gate, loc, y_rows)


def _block_diag(w):
    n, a, b = w.shape
    eye = np.eye(n, dtype=np.float32)
    return (w[:, :, None, :] * eye[:, None, :, None]).reshape(n * a, n * b)


def _pad_heads(v, used):
    lead = v.shape[:-1]
    v = v.reshape(lead + (MLA_HEADS, used))
    v = jnp.pad(v, [(0, 0)] * len(lead) + [(0, 0), (0, MLA_HEAD_PAD - used)])
    return v.reshape(lead + (MLA_QK_PAD,))


def _rotary_lane_tables(seq):
    half = MLA_ROPE_DIM // 2
    inv_freq = ROPE_THETA ** (-jnp.arange(half, dtype=F32) / half)
    ang = jnp.arange(seq, dtype=F32)[:, None] * inv_freq[None, :]
    cos, sin = jnp.cos(ang), jnp.sin(ang)
    ones = jnp.ones((seq, MLA_NOPE_DIM), F32)
    tail = MLA_HEAD_PAD - MLA_QK_DIM
    cos_t = jnp.concatenate([ones, cos, cos, jnp.ones((seq, tail), F32)], axis=1)
    sin_t = jnp.concatenate([0.0 * ones, -sin, sin, jnp.zeros((seq, tail), F32)], axis=1)
    return cos_t, sin_t


def _pick_tile(n, pref):
    t = min(n, pref)
    while n % t:
        t //= 2
    return t


def kernel(x, norm1_g, w_in, lru_conv_w, lru_conv_b, lru_wa, lru_ba, lru_wx, lru_bx, lru_lambda, lru_out_g,
           diff_q_g, diff_k_g, diff_lq1, diff_lk1, diff_lq2, diff_lk2, diff_sub_g,
           mla_cq_g, mla_ckv_g, mla_w_uq, mla_w_ukv, mla_q_g, mla_k_g, mla_out_g, w_out, norm2_g,
           router_g_w, router_g_b, router_e_w, router_e_b, exp_w1, exp_w3, exp_w2):
    B, S, D = x.shape
    N = B * S
    depth = w_in.shape[0]
    assert S % KEY_TILE == 0 and S % min(QUERY_TILE, S) == 0 and D == D_MODEL
    ts = _pick_tile(S, 512)
    tc = _pick_tile(S, 512)
    assert N % TOK_TILE == 0
    tm = _pick_tile(N, 512)

    cos_t, sin_t = _rotary_lane_tables(S)
    cos_q, sin_q = cos_t.T, sin_t.T
    per_row = lambda v: jnp.broadcast_to(v.astype(F32)[:, None], (v.shape[0], LANES))
    n_dgroups = DIFF_QK_WIDTH // DIFF_HEAD_DIM
    gsum = jnp.asarray(np.kron(np.eye(n_dgroups), np.ones((DIFF_HEAD_DIM, DIFF_HEAD_DIM))), BF16)
    slopes = np.asarray([2.0 ** (-ALIBI_MAX_BIAS * (h + 1) / DIFF_HEADS) * LOG2E for h in range(DIFF_HEADS)], np.float32)
    slopes = jnp.asarray(np.pad(slopes.reshape(DIFF_HEADS // 2, 1, 2), ((0, 0), (0, 0), (0, LANES - 2))))
    place = np.zeros((MLA_ROPE_DIM, MLA_HEADS, MLA_HEAD_PAD), np.float32)
    place[np.arange(MLA_ROPE_DIM), :, MLA_NOPE_DIM + np.arange(MLA_ROPE_DIM)] = 1.0
    place = jnp.asarray(place.reshape(MLA_ROPE_DIM, MLA_QK_PAD))
    row = lambda v: v.reshape(1, -1).astype(F32)

    n_blk = -(-(2 * N + N_EXPERTS * (RUN_CHUNK + ROW_BLOCK)) // ROW_BLOCK) + 1
    n_rows = n_blk * ROW_BLOCK

    for l in range(depth):
        dqg = per_row(jnp.tile(diff_q_g[l], n_dgroups) * (DIFF_HEAD_DIM ** -0.5 * LOG2E))
        dkg = row(jnp.tile(diff_k_g[l], n_dgroups))
        wuq = _pad_heads(mla_w_uq[l], MLA_QK_DIM).astype(BF16)
        mqg = per_row(_pad_heads(jnp.tile(mla_q_g[l], MLA_HEADS) * (MLA_QK_DIM ** -0.5 * LOG2E), MLA_QK_DIM))
        mkg = row(_pad_heads(jnp.tile(mla_k_g[l], MLA_HEADS), MLA_QK_DIM))
        wukv = mla_w_ukv[l].reshape(MLA_KV_RANK, MLA_HEADS, MLA_NOPE_DIM + MLA_V_DIM)
        wk_nope = _pad_heads(wukv[:, :, :MLA_NOPE_DIM].reshape(MLA_KV_RANK, -1), MLA_NOPE_DIM)
        wk = jnp.concatenate([wk_nope, place], axis=0).astype(BF16)
        wv = wukv[:, :, MLA_NOPE_DIM:].reshape(MLA_KV_RANK, MLA_WIDTH).astype(BF16)

        lru_in, dq, dk, dv, mq, mk, mv = _in_proj(
            x, row(norm1_g[l]), w_in, l, gsum, dqg, dkg, row(mla_cq_g[l]), wuq, mqg, row(mla_ckv_g[l]), wk, mkg, wv,
            cos_t, sin_t, cos_q, sin_q, ts)

        y_lru = _lru(lru_in, lru_conv_w[l], row(lru_conv_b[l]), _block_diag(lru_wa[l]).astype(BF16),
                     row(lru_ba[l]), _block_diag(lru_wx[l]).astype(BF16), row(lru_bx[l]), row(lru_lambda[l]),
                     row(lru_out_g[l]), tc)

        lambda_init = 0.8 - 0.6 * math.exp(-0.3 * l)
        y_diff = _diff_attn(dq, dk, dv, slopes, row(diff_lq1[l]), row(diff_lk1[l]), row(diff_lq2[l]),
                            row(diff_lk2[l]), row(jnp.tile(diff_sub_g[l], 2)), lambda_init)
        y_mla = _mla_attn(mq, mk, mv)

        wr = jnp.concatenate([router_g_w[l], router_e_w[l],
                              jnp.zeros((D, ROUTER_PAD - N_GROUPS - N_EXPERTS), F32)], axis=1)
        wrh = wr.astype(BF16)
        wrl = (wr - wrh.astype(F32)).astype(BF16)
        rb = jnp.pad(jnp.concatenate([router_g_b[l], router_e_b[l]]), (0, ROUTER_PAD - N_GROUPS - N_EXPERTS))
        x1, h2, eid, gate, cnt = _out_proj(
            x.reshape(N, D), y_lru.reshape(N, LRU_WIDTH), y_diff.reshape(N, DIFF_WIDTH), y_mla.reshape(N, MLA_WIDTH),
            row(mla_out_g[l]), w_out[l].astype(BF16), row(norm2_g[l]), wrh, wrl, row(rb), tm)

        counts = cnt[0, N_GROUPS:N_GROUPS + N_EXPERTS].astype(jnp.int32)
        padded = (counts + RUN_CHUNK + ROW_BLOCK - 1) // ROW_BLOCK * ROW_BLOCK
        pad_ends = jnp.cumsum(padded)
        pad_starts = pad_ends - padded
        blk_row0 = jnp.arange(n_blk, dtype=jnp.int32) * ROW_BLOCK
        blk_expert = jnp.minimum(jnp.sum((pad_ends[None, :] <= blk_row0[:, None]).astype(jnp.int32), axis=1),
                                 N_EXPERTS - 1)
        n_used = (pad_ends[-1:] // ROW_BLOCK).astype(jnp.int32)
        starts = jnp.pad(pad_starts.astype(F32), (0, LANES - N_EXPERTS)).reshape(1, LANES)
        tail_start = pad_starts + counts
        last = jnp.arange(N_EXPERTS) == N_EXPERTS - 1
        tail_ends = jnp.where(last, n_rows, pad_ends)
        aligned_start = (tail_start + RUN_CHUNK - 1) // RUN_CHUNK * RUN_CHUNK
        ztab = jnp.concatenate([tail_start, aligned_start, (tail_ends - aligned_start) // RUN_CHUNK]).astype(jnp.int32)

        loc, tab = _rank(eid, starts)
        tabs = tab.reshape(N // TOK_TILE, TAB_ROWS, LANES)[:, :3, :N_EXPERTS].reshape(-1)
        rows = _dispatch(tabs, ztab, h2, loc, n_rows)
        y_rows = _experts(blk_expert + l * N_EXPERTS, n_used, rows, exp_w1.reshape(-1, D, D_EXPERT),
                          exp_w3.reshape(-1, D, D_EXPERT), exp_w2.reshape(-1, D_EXPERT, D))
        x = _combine(tabs, x1, gate, loc, y_rows).reshape(B, S, D)
    return x
```

```python
import functools
import math

import jax
import jax.numpy as jnp
import numpy as np
from jax import lax
from jax.experimental import pallas as pl
from jax.experimental.pallas import tpu as pltpu

F32 = jnp.float32
BF16 = jnp.bfloat16

D_MODEL = 1024
CHUNK = 64
LRU_WIDTH = 256
CONV_WIDTH = 4
RG_C = 8.0
DIFF_HEADS = 6
DIFF_HEAD_DIM = 32
DIFF_V_DIM = 64
DIFF_QK_WIDTH = DIFF_HEADS * 2 * DIFF_HEAD_DIM
DIFF_WIDTH = DIFF_HEADS * DIFF_V_DIM
ALIBI_MAX_BIAS = 8.0
MLA_HEADS = 6
MLA_Q_RANK = 192
MLA_KV_RANK = 128
MLA_NOPE_DIM = 64
MLA_ROPE_DIM = 32
MLA_V_DIM = 64
MLA_QK_DIM = MLA_NOPE_DIM + MLA_ROPE_DIM
MLA_WIDTH = MLA_HEADS * MLA_V_DIM
ROPE_THETA = 10000.0
N_GROUPS = 4
EXPERTS_PER_GROUP = 8
N_EXPERTS = N_GROUPS * EXPERTS_PER_GROUP
D_EXPERT = 256
ROW_BLOCK = 256
EPS = 1e-6

LANES = 128
ROW_TILES = D_MODEL // LANES
MLA_HEAD_PAD = LANES
MLA_QK_PAD = MLA_HEADS * MLA_HEAD_PAD
VMEM_LIMIT = 56 * 1024 * 1024
NEG_BIG = -1e30
KEY_TILE = 256
QUERY_TILE = 512
TILES_PER_GROUP = 4
LOG2E = math.log2(math.e)

C_LRU = 0
C_DQ = 512
C_DK = 896
C_DV = 1280
C_CQ = 1664
C_CKV = 1856
C_KR = 1984
D_IN = 2016
D_IN_PAD = 2048


def _cparams(sem):
    return pltpu.CompilerParams(dimension_semantics=sem, vmem_limit_bytes=VMEM_LIMIT)


def _rms(v, width):
    return v * lax.rsqrt(jnp.sum(v * v, axis=-1, keepdims=True) * (1.0 / width) + EPS)


def _dot(a, b):
    return jnp.dot(a, b, preferred_element_type=F32)


def _split_bf16(v):
    hi = v.astype(BF16)
    lo = (v - hi.astype(F32)).astype(BF16)
    return hi, lo


def _in_proj_kernel(x_ref, g1_ref, w_ref, gsum_ref, dqg_ref, dkg_ref, cqg_ref, wuq_ref, mqg_ref, ckvg_ref,
                    wk_ref, mkg_ref, wv_ref, cos_ref, sin_ref, cosq_ref, sinq_ref,
                    lru_ref, dqt_ref, dk_ref, dvt_ref, mqt_ref, mk_ref, mvt_ref, w_bf):
    @pl.when((pl.program_id(0) == 0) & (pl.program_id(1) == 0))
    def _():
        w_bf[:, 0:D_IN] = w_ref[0].astype(BF16)
        w_bf[:, D_IN:] = jnp.zeros((D_MODEL, D_IN_PAD - D_IN), BF16)

    x = x_ref[0]
    hn = (_rms(x, D_MODEL) * g1_ref[...]).astype(BF16)
    p = _dot(hn, w_bf[...])
    lru_ref[0] = p[:, C_LRU:C_LRU + 2 * LRU_WIDTH]

    gsum = gsum_ref[...]

    def group_norm32(v, gain):
        hi, lo = _split_bf16(v * v)
        ss = _dot(hi, gsum) + _dot(lo, gsum)
        return v * lax.rsqrt(ss * (1.0 / DIFF_HEAD_DIM) + EPS) * gain

    def store_tiles(ref, vt):
        for c in range(vt.shape[1] // KEY_TILE):
            ref[0, c] = vt[:, c * KEY_TILE:(c + 1) * KEY_TILE].astype(BF16)

    t = x.shape[0]

    def along_lanes(per_feature):
        return jnp.concatenate([per_feature] * (t // LANES), axis=1)

    dq_t = p[:, C_DQ:C_DQ + DIFF_QK_WIDTH].T.reshape(DIFF_QK_WIDTH // DIFF_HEAD_DIM, DIFF_HEAD_DIM, t)
    dq_ss = jnp.sum(dq_t * dq_t, axis=1, keepdims=True) * (1.0 / DIFF_HEAD_DIM)
    dq_n = (dq_t * lax.rsqrt(dq_ss + EPS)).reshape(DIFF_QK_WIDTH, t) * along_lanes(dqg_ref[...])
    dqt_ref[0] = dq_n.astype(BF16)
    dk_ref[0] = group_norm32(p[:, C_DK:C_DK + DIFF_QK_WIDTH], dkg_ref[...]).astype(BF16)
    store_tiles(dvt_ref, p[:, C_DV:C_DV + DIFF_WIDTH].T)

    cos = cos_ref[...]
    sin = sin_ref[...]
    lane = lax.broadcasted_iota(jnp.int32, cos.shape, 1)
    first_half = lane < MLA_NOPE_DIM + MLA_ROPE_DIM // 2

    def head_norm_rotary(v, gain):
        outs = []
        for h in range(MLA_HEADS):
            c = v[:, h * MLA_HEAD_PAD:(h + 1) * MLA_HEAD_PAD]
            c = _rms(c, MLA_QK_DIM) * gain[:, h * MLA_HEAD_PAD:(h + 1) * MLA_HEAD_PAD]
            swapped = jnp.where(first_half, pltpu.roll(c, MLA_HEAD_PAD - MLA_ROPE_DIM // 2, 1),
                                pltpu.roll(c, MLA_ROPE_DIM // 2, 1))
            outs.append(c * cos + swapped * sin)
        return jnp.concatenate(outs, axis=1)

    cq = (_rms(p[:, C_CQ:C_CQ + MLA_Q_RANK], MLA_Q_RANK) * cqg_ref[...]).astype(BF16)
    q_t = _dot(cq, wuq_ref[...]).T
    cos_q, sin_q = cosq_ref[...], sinq_ref[...]
    gain_q = along_lanes(mqg_ref[...])
    half = MLA_ROPE_DIM // 2
    heads_t = []
    for h in range(MLA_HEADS):
        rows = slice(h * MLA_HEAD_PAD, (h + 1) * MLA_HEAD_PAD)
        c = q_t[rows]
        ss = jnp.sum(c * c, axis=0, keepdims=True) * (1.0 / MLA_QK_DIM)
        c = c * lax.rsqrt(ss + EPS) * gain_q[rows]
        swapped = jnp.concatenate([c[:MLA_NOPE_DIM], c[MLA_NOPE_DIM + half:MLA_QK_DIM],
                                   c[MLA_NOPE_DIM:MLA_NOPE_DIM + half], c[MLA_QK_DIM:]], axis=0)
        heads_t.append((c * cos_q + swapped * sin_q).astype(BF16))
    mqt_ref[0] = jnp.concatenate(heads_t, axis=0)

    ckv = (_rms(p[:, C_CKV:C_CKV + MLA_KV_RANK], MLA_KV_RANK) * ckvg_ref[...]).astype(BF16)
    kcat = jnp.concatenate([ckv, p[:, C_KR:C_KR + MLA_ROPE_DIM].astype(BF16)], axis=1)
    mk_ref[0] = head_norm_rotary(_dot(kcat, wk_ref[...]), mkg_ref[...]).astype(BF16)
    store_tiles(mvt_ref, _dot(ckv, wv_ref[...]).T)


def _in_proj(x, g1, w_all, layer, gsum, dqg, dkg, cqg, wuq, mqg, ckvg, wk, mkg, wv, cos_t, sin_t, cos_q, sin_q, ts):
    B, S, _ = x.shape
    full = lambda a: pl.BlockSpec(a.shape, lambda b, i: (0,) * a.ndim)
    w_spec = pl.BlockSpec((1,) + w_all.shape[1:], lambda b, i: (layer, 0, 0))
    tok = lambda width: pl.BlockSpec((1, ts, width), lambda b, i: (b, i, 0))
    tok_t = lambda width: pl.BlockSpec((1, width, ts), lambda b, i: (b, 0, i))
    nkt = ts // KEY_TILE
    tiles_t = lambda width: pl.BlockSpec((1, nkt, width, KEY_TILE), lambda b, i: (b, i, 0, 0))
    rot = pl.BlockSpec((ts, LANES), lambda b, i: (i, 0))
    rot_t = pl.BlockSpec((LANES, ts), lambda b, i: (0, i))
    shp = lambda shape, dt: jax.ShapeDtypeStruct(shape, dt)
    return pl.pallas_call(
        _in_proj_kernel,
        grid=(B, S // ts),
        in_specs=[tok(D_MODEL), full(g1), w_spec, full(gsum), full(dqg), full(dkg), full(cqg), full(wuq),
                  full(mqg), full(ckvg), full(wk), full(mkg), full(wv), rot, rot, rot_t, rot_t],
        out_specs=[tok(2 * LRU_WIDTH), tok_t(DIFF_QK_WIDTH), tok(DIFF_QK_WIDTH), tiles_t(DIFF_WIDTH),
                   tok_t(MLA_QK_PAD), tok(MLA_QK_PAD), tiles_t(MLA_WIDTH)],
        out_shape=[shp((B, S, 2 * LRU_WIDTH), F32), shp((B, DIFF_QK_WIDTH, S), BF16), shp((B, S, DIFF_QK_WIDTH), BF16),
                   shp((B, S // KEY_TILE, DIFF_WIDTH, KEY_TILE), BF16), shp((B, MLA_QK_PAD, S), BF16),
                   shp((B, S, MLA_QK_PAD), BF16), shp((B, S // KEY_TILE, MLA_WIDTH, KEY_TILE), BF16)],
        scratch_shapes=[pltpu.VMEM((D_MODEL, D_IN_PAD), BF16)],
        compiler_params=_cparams(("arbitrary", "arbitrary")),
        name="in_proj",
    )(x, g1, w_all, gsum, dqg, dkg, cqg, wuq, mqg, ckvg, wk, mkg, wv, cos_t, sin_t, cos_q, sin_q)


LRU_HALO = 8


def _lru_kernel(blk_ref, cw_ref, cb_ref, wa_ref, ba_ref, wx_ref, bx_ref, lam_ref, og_ref, y_ref,
                ubuf, a_scr, b_scr, h_scr):
    B, tc, _ = blk_ref.shape
    W = LRU_WIDTH

    @pl.when(pl.program_id(0) == 0)
    def _():
        ubuf[:, 0:LRU_HALO, :] = jnp.zeros((B, LRU_HALO, W), F32)
        h_scr[...] = jnp.zeros(h_scr.shape, F32)

    ubuf[:, LRU_HALO:, :] = blk_ref[:, :, 0:W]
    xc = cb_ref[...][None]
    for j in range(CONV_WIDTH):
        off = LRU_HALO - (CONV_WIDTH - 1) + j
        xc = xc + cw_ref[j:j + 1, :][None] * ubuf[:, off:off + tc, :]
    ubuf[:, 0:LRU_HALO, :] = ubuf[:, tc:tc + LRU_HALO, :]

    x2 = xc.reshape(B * tc, W)
    xb = x2.astype(BF16)
    r = jax.nn.sigmoid(_dot(xb, wa_ref[...]) + ba_ref[...])
    gi = jax.nn.sigmoid(_dot(xb, wx_ref[...]) + bx_ref[...])
    nl = -lam_ref[...]
    softplus = jnp.maximum(nl, 0.0) + jnp.log(1.0 + jnp.exp(-jnp.abs(nl)))
    a = jnp.exp(-RG_C * r * softplus)
    bb = jnp.sqrt(1.0 - a * a) * gi * x2
    a_scr[...] = a.reshape(B, tc, W)
    b_scr[...] = bb.reshape(B, tc, W)

    def step(t, h):
        h = a_scr[:, pl.ds(t, 1), :] * h + b_scr[:, pl.ds(t, 1), :]
        b_scr[:, pl.ds(t, 1), :] = h
        return h

    h_scr[...] = lax.fori_loop(0, tc, step, h_scr[...], unroll=8)

    gate = blk_ref[:, :, W:2 * W]
    gelu = 0.5 * gate * (1.0 + jnp.tanh(math.sqrt(2.0 / math.pi) * (gate + 0.044715 * gate * gate * gate)))
    y = b_scr[...] * gelu
    y_ref[...] = (_rms(y, W) * og_ref[...][None]).astype(y_ref.dtype)


def _lru(lru_in, cw, cb, wa, ba, wx, bx, lam, og, tc):
    B, S, _ = lru_in.shape
    W = LRU_WIDTH
    full = lambda a: pl.BlockSpec(a.shape, lambda i: (0,) * a.ndim)
    return pl.pallas_call(
        _lru_kernel,
        grid=(S // tc,),
        in_specs=[pl.BlockSpec((B, tc, 2 * W), lambda i: (0, i, 0)), full(cw), full(cb), full(wa), full(ba),
                  full(wx), full(bx), full(lam), full(og)],
        out_specs=pl.BlockSpec((B, tc, W), lambda i: (0, i, 0)),
        out_shape=jax.ShapeDtypeStruct((B, S, W), BF16),
        scratch_shapes=[pltpu.VMEM((B, tc + LRU_HALO, W), F32), pltpu.VMEM((B, tc, W), F32),
                        pltpu.VMEM((B, tc, W), F32), pltpu.VMEM((B, 1, W), F32)],
        compiler_params=_cparams(("arbitrary",)),
        name="lru",
    )(lru_in, cw, cb, wa, ba, wx, bx, lam, og)


def _flash_maps(qt_list, k_ref, vt_ref, k_lanes, bias_of, tile_shift, m_scr, l_scr, acc_scr, tiles_per_step):
    qi = pl.program_id(2)
    n_maps = len(qt_list)
    tq = qt_list[0].shape[1]
    maps = range(n_maps)

    own = tq // KEY_TILE
    first_own = own * qi
    last = first_own + own - 1

    def biased_scores(j):
        jc = jnp.minimum(j, last)
        start = pl.multiple_of(jc * KEY_TILE, KEY_TILE)
        kind = jnp.clip(jc - first_own + 1, 0, own)
        return tuple(_dot(k_ref[0, pl.ds(start, KEY_TILE), k_lanes[i]], qt_list[i]) + bias_of(i, kind)
                     for i in maps)

    for i in maps:
        m_scr[i] = jnp.full((1, tq), NEG_BIG, F32)
        l_scr[i] = jnp.zeros((1, tq), F32)
        acc_scr[i] = jnp.zeros(acc_scr.shape[1:], F32)

    def update(j, s_tile):
        vt = vt_ref[0, jnp.minimum(j, last)]
        for i in maps:
            s = s_tile[i]
            c = tile_shift(i, first_own - j)
            m_old = m_scr[i]
            m_new = jnp.maximum(m_old, jnp.max(s, axis=0, keepdims=True) + c)
            alpha = jnp.exp2(m_old - m_new)
            p = jnp.exp2(s - (m_new - c))
            m_scr[i] = m_new
            l_scr[i] = alpha * l_scr[i] + jnp.sum(p, axis=0, keepdims=True)
            acc_scr[i] = alpha * acc_scr[i] + _dot(vt, p.astype(BF16))

    def one_tile(j, s_tile):
        s_ahead = biased_scores(j + 1)
        update(j, s_tile)
        return s_ahead

    def group_of(size, first):
        def body(t, s_tile):
            for u in range(size):
                s_tile = one_tile(first + size * t + u, s_tile)
            return s_tile
        return body

    rest_size = math.gcd(tiles_per_step, own)
    n_groups = (last + 1) // tiles_per_step
    s_rest = lax.fori_loop(0, n_groups, group_of(tiles_per_step, 0), biased_scores(0))
    done = n_groups * tiles_per_step
    lax.fori_loop(0, (last + 1 - done) // rest_size, group_of(rest_size, done), s_rest)


def _own_tile_geometry(tq, u):
    krow = lax.broadcasted_iota(jnp.int32, (KEY_TILE, tq), 0) + u * KEY_TILE
    qcol = lax.broadcasted_iota(jnp.int32, (KEY_TILE, tq), 1)
    return krow.astype(F32), qcol.astype(F32), (krow // CHUNK) <= (qcol // CHUNK)


def _diff_attn_kernel(qt_ref, k_ref, vt_ref, slope_ref, lq1_ref, lk1_ref, lq2_ref, lk2_ref, subg_ref, o_ref,
                      m_scr, l_scr, acc_scr, bias_scr, *, lambda_init, tq):
    qt = qt_ref[0]
    feat = lax.broadcasted_iota(jnp.int32, qt.shape, 0)
    zero = jnp.zeros_like(qt)
    qt_list = [jnp.where(feat // DIFF_HEAD_DIM == i, qt, zero) for i in range(4)]
    slope = [slope_ref[0, 0:1, hh:hh + 1] for hh in range(2)]

    @pl.when(pl.program_id(2) == 0)
    def _():
        for hh in range(2):
            bias_scr[hh, 0] = slope[hh] * _own_tile_geometry(tq, 0)[0]
            for u in range(tq // KEY_TILE):
                kf, qf, allowed = _own_tile_geometry(tq, u)
                own_bias = slope[hh] * (qf - jnp.abs(qf - kf) - float(u * KEY_TILE))
                bias_scr[hh, u + 1] = jnp.where(allowed, own_bias, NEG_BIG)

    def tile_shift(i, n_tiles):
        return -slope[i // 2] * (n_tiles * KEY_TILE).astype(F32)

    _flash_maps(qt_list, k_ref, vt_ref, [slice(None)] * 4, lambda i, kind: bias_scr[i // 2, kind],
                tile_shift, m_scr, l_scr, acc_scr, TILES_PER_GROUP)

    lam = (jnp.exp(jnp.sum(lq1_ref[...] * lk1_ref[...], axis=-1, keepdims=True))
           - jnp.exp(jnp.sum(lq2_ref[...] * lk2_ref[...], axis=-1, keepdims=True)) + lambda_init)
    orow = lax.broadcasted_iota(jnp.int32, (LANES, tq), 0)
    out_t = jnp.zeros((LANES, tq), F32)
    for hh in range(2):
        o = (acc_scr[2 * hh] * (1.0 / l_scr[2 * hh])
             - acc_scr[2 * hh + 1] * (lam / l_scr[2 * hh + 1]))
        mine = (orow // DIFF_V_DIM) == hh
        ms = jnp.sum(jnp.where(mine, o * o, 0.0), axis=0, keepdims=True) * (1.0 / DIFF_V_DIM)
        out_t = jnp.where(mine, o * lax.rsqrt(ms + EPS), out_t)
    o_ref[0] = (out_t.T * subg_ref[...] * (1.0 - lambda_init)).astype(o_ref.dtype)


def _diff_attn(dqt, dk, dvt, slopes, lq1, lk1, lq2, lk2, subg, lambda_init):
    B, S, _ = dk.shape
    tq = min(QUERY_TILE, S)
    n_pairs = DIFF_HEADS // 2
    full = lambda a: pl.BlockSpec(a.shape, lambda b, p, i: (0,) * a.ndim)
    kern = functools.partial(_diff_attn_kernel, lambda_init=lambda_init, tq=tq)
    return pl.pallas_call(
        kern,
        grid=(B, n_pairs, S // tq),
        in_specs=[pl.BlockSpec((1, LANES, tq), lambda b, p, i: (b, p, i)),
                  pl.BlockSpec((1, S, LANES), lambda b, p, i: (b, 0, p)),
                  pl.BlockSpec((1, S // KEY_TILE, LANES, KEY_TILE), lambda b, p, i: (b, 0, p, 0)),
                  pl.BlockSpec((1, 1, LANES), lambda b, p, i: (p, 0, 0)),
                  full(lq1), full(lk1), full(lq2), full(lk2), full(subg)],
        out_specs=pl.BlockSpec((1, tq, LANES), lambda b, p, i: (b, i, p)),
        out_shape=jax.ShapeDtypeStruct((B, S, DIFF_WIDTH), BF16),
        scratch_shapes=[pltpu.VMEM((4, 1, tq), F32), pltpu.VMEM((4, 1, tq), F32), pltpu.VMEM((4, LANES, tq), F32),
                        pltpu.VMEM((2, 1 + tq // KEY_TILE, KEY_TILE, tq), F32)],
        compiler_params=_cparams(("parallel", "parallel", "arbitrary")),
        name="diff_attn",
    )(dqt, dk, dvt, slopes, lq1, lk1, lq2, lk2, subg)


def _mla_attn_kernel(qt_ref, k_ref, vt_ref, o_ref, m_scr, l_scr, acc_scr, bias_scr, *, tq):
    qt = qt_ref[0]
    qt_list = [qt[hh * MLA_HEAD_PAD:(hh + 1) * MLA_HEAD_PAD, :] for hh in range(2)]
    k_lanes = [slice(hh * MLA_HEAD_PAD, (hh + 1) * MLA_HEAD_PAD) for hh in range(2)]

    @pl.when(pl.program_id(2) == 0)
    def _():
        bias_scr[0] = jnp.zeros((KEY_TILE, tq), F32)
        for u in range(tq // KEY_TILE):
            bias_scr[u + 1] = jnp.where(_own_tile_geometry(tq, u)[2], 0.0, NEG_BIG)

    _flash_maps(qt_list, k_ref, vt_ref, k_lanes, lambda i, kind: bias_scr[kind], lambda i, n: 0.0,
                m_scr, l_scr, acc_scr, TILES_PER_GROUP)
    orow = lax.broadcasted_iota(jnp.int32, (LANES, tq), 0)
    out_t = jnp.where(orow < MLA_V_DIM, acc_scr[0] * (1.0 / l_scr[0]), acc_scr[1] * (1.0 / l_scr[1]))
    o_ref[0] = out_t.T.astype(o_ref.dtype)


def _mla_attn(mqt, mk, mvt):
    B, S, _ = mk.shape
    tq = min(QUERY_TILE, S)
    n_pairs = MLA_HEADS // 2
    kern = functools.partial(_mla_attn_kernel, tq=tq)
    return pl.pallas_call(
        kern,
        grid=(B, n_pairs, S // tq),
        in_specs=[pl.BlockSpec((1, 2 * MLA_HEAD_PAD, tq), lambda b, p, i: (b, p, i)),
                  pl.BlockSpec((1, S, 2 * MLA_HEAD_PAD), lambda b, p, i: (b, 0, p)),
                  pl.BlockSpec((1, S // KEY_TILE, LANES, KEY_TILE), lambda b, p, i: (b, 0, p, 0))],
        out_specs=pl.BlockSpec((1, tq, LANES), lambda b, p, i: (b, i, p)),
        out_shape=jax.ShapeDtypeStruct((B, S, MLA_WIDTH), F32),
        scratch_shapes=[pltpu.VMEM((2, 1, tq), F32), pltpu.VMEM((2, 1, tq), F32), pltpu.VMEM((2, LANES, tq), F32),
                        pltpu.VMEM((1 + tq // KEY_TILE, KEY_TILE, tq), F32)],
        compiler_params=_cparams(("parallel", "parallel", "arbitrary")),
        name="mla_attn",
    )(mqt, mk, mvt)


ROUTER_PAD = LANES


def _out_proj_kernel(x_ref, ylru_ref, ydiff_ref, ymla_ref, mlag_ref, wo_ref, g2_ref, wr_ref, rb_ref,
                     x1_ref, h2_ref, eid_ref, gate_ref, cnt_ref):
    @pl.when(pl.program_id(0) == 0)
    def _():
        cnt_ref[...] = jnp.zeros(cnt_ref.shape, F32)

    ymla = (_rms(ymla_ref[...], MLA_WIDTH) * mlag_ref[...]).astype(BF16)
    x1 = (x_ref[...]
          + _dot(ylru_ref[...], wo_ref[0:LRU_WIDTH, :])
          + _dot(ydiff_ref[...], wo_ref[LRU_WIDTH:LRU_WIDTH + DIFF_WIDTH, :])
          + _dot(ymla, wo_ref[LRU_WIDTH + DIFF_WIDTH:, :]))
    x1_ref[...] = x1

    h2 = _rms(x1, D_MODEL) * g2_ref[...]
    hi, lo = _split_bf16(h2)
    h2_ref[...] = hi
    both = _dot(hi, wr_ref[...])
    logits = (both[:, :ROUTER_PAD] + both[:, ROUTER_PAD:] + _dot(lo, wr_ref[:, 0:ROUTER_PAD]) + rb_ref[...])

    lane = lax.broadcasted_iota(jnp.int32, logits.shape, 1)
    big = jnp.int32(1 << 20)
    is_group = lane < N_GROUPS
    gl = jnp.where(is_group, logits, NEG_BIG)
    gmax = jnp.max(gl, axis=-1, keepdims=True)
    g_idx = jnp.min(jnp.where(gl == gmax, lane, big), axis=-1, keepdims=True)
    pg_top = 1.0 / jnp.sum(jnp.where(is_group, jnp.exp(gl - gmax), 0.0), axis=-1, keepdims=True)

    e_lane = lane - N_GROUPS
    in_group = (e_lane >= 0) & (e_lane < N_EXPERTS) & ((e_lane // EXPERTS_PER_GROUP) == g_idx)
    el = jnp.where(in_group, logits, NEG_BIG)
    m1 = jnp.max(el, axis=-1, keepdims=True)
    i1 = jnp.min(jnp.where(el == m1, lane, big), axis=-1, keepdims=True)
    el2 = jnp.where(lane == i1, NEG_BIG, el)
    m2 = jnp.max(el2, axis=-1, keepdims=True)
    i2 = jnp.min(jnp.where(el2 == m2, lane, big), axis=-1, keepdims=True)
    e2 = jnp.exp(m2 - m1)
    g1 = pg_top / (1.0 + e2)
    g2 = pg_top * e2 / (1.0 + e2)

    two = lax.broadcasted_iota(jnp.int32, eid_ref.shape, 1)
    eid_ref[...] = jnp.where(two == 0, i1 - N_GROUPS, i2 - N_GROUPS)
    gate_ref[...] = jnp.where(two == 0, g1, g2)
    onehot = ((e_lane == i1 - N_GROUPS) | (e_lane == i2 - N_GROUPS)).astype(F32)
    cnt_ref[...] += jnp.sum(onehot, axis=0, keepdims=True)


def _out_proj(x, ylru, ydiff, ymla, mlag, wo, g2, wr, rb, tm):
    N = x.shape[0]
    full = lambda a: pl.BlockSpec(a.shape, lambda i: (0,) * a.ndim)
    tok = lambda width: pl.BlockSpec((tm, width), lambda i: (i, 0))
    return pl.pallas_call(
        _out_proj_kernel,
        grid=(N // tm,),
        in_specs=[tok(D_MODEL), tok(LRU_WIDTH), tok(DIFF_WIDTH), tok(MLA_WIDTH), full(mlag), full(wo), full(g2),
                  full(wr), full(rb)],
        out_specs=[tok(D_MODEL), tok(D_MODEL), tok(2), tok(2), pl.BlockSpec((1, ROUTER_PAD), lambda i: (0, 0))],
        out_shape=[jax.ShapeDtypeStruct((N, D_MODEL), F32), jax.ShapeDtypeStruct((N, D_MODEL), BF16),
                   jax.ShapeDtypeStruct((N, 2), jnp.int32),
                   jax.ShapeDtypeStruct((N, 2), F32), jax.ShapeDtypeStruct((1, ROUTER_PAD), F32)],
        compiler_params=_cparams(("arbitrary",)),
        name="out_proj",
    )(x, ylru, ydiff, ymla, mlag, wo, g2, wr, rb)


TOK_TILE = 256
RUN_CHUNK = 8
TAB_ROWS = 8


def _rank_kernel(eid_ref, start_ref, loc_ref, tab_ref, carry):
    tm = eid_ref.shape[0]

    @pl.when(pl.program_id(0) == 0)
    def _():
        carry[...] = start_ref[...]

    lane = lax.broadcasted_iota(jnp.int32, (tm, LANES), 1)
    r = lax.broadcasted_iota(jnp.int32, (tm, tm), 0)
    c = lax.broadcasted_iota(jnp.int32, (tm, tm), 1)
    lower = (c < r).astype(BF16)
    eid = eid_ref[...]
    onehot = [lane == eid[:, k:k + 1] for k in range(2)]
    cnt = [jnp.sum(oh.astype(F32), axis=0, keepdims=True) for oh in onehot]
    cnt_tile = cnt[0] + cnt[1]

    chunks = jnp.floor((cnt_tile + (RUN_CHUNK - 1)) * (1.0 / RUN_CHUNK))
    lane8 = lax.broadcasted_iota(jnp.int32, (TAB_ROWS, LANES), 1)
    incl = jnp.broadcast_to(chunks, (TAB_ROWS, LANES))
    shift = 1
    while shift < LANES:
        incl = incl + jnp.where(lane8 >= shift, pltpu.roll(incl, shift, 1), 0.0)
        shift *= 2
    offs = (incl[0:1] - chunks) * RUN_CHUNK

    locs = []
    for k in range(2):
        before = _dot(lower, onehot[k].astype(BF16))
        first = offs if k == 0 else offs + cnt[0]
        locs.append(jnp.sum(jnp.where(onehot[k], before + first, 0.0), axis=-1, keepdims=True))
    two = lax.broadcasted_iota(jnp.int32, loc_ref.shape, 1)
    loc_ref[...] = jnp.where(two == 0, locs[0], locs[1]).astype(jnp.int32)

    row8 = lax.broadcasted_iota(jnp.int32, (TAB_ROWS, LANES), 0)
    base = carry[...]
    tab = jnp.where(row8 == 0, cnt_tile, jnp.where(row8 == 1, offs, jnp.where(row8 == 2, base, 0.0)))
    tab_ref[...] = tab.astype(jnp.int32)
    carry[...] = base + cnt_tile


def _rank(eid, starts):
    N = eid.shape[0]
    n_tiles = N // TOK_TILE
    return pl.pallas_call(
        _rank_kernel,
        grid=(n_tiles,),
        in_specs=[pl.BlockSpec((TOK_TILE, 2), lambda i: (i, 0)), pl.BlockSpec((1, LANES), lambda i: (0, 0))],
        out_specs=[pl.BlockSpec((TOK_TILE, 2), lambda i: (i, 0)), pl.BlockSpec((TAB_ROWS, LANES), lambda i: (i, 0))],
        out_shape=[jax.ShapeDtypeStruct((N, 2), jnp.int32),
                   jax.ShapeDtypeStruct((n_tiles * TAB_ROWS, LANES), jnp.int32)],
        scratch_shapes=[pltpu.VMEM((1, LANES), F32)],
        compiler_params=_cparams(("arbitrary",)),
        name="rank",
    )(eid, starts)


SORT_ROWS = 2 * TOK_TILE + N_EXPERTS * RUN_CHUNK


def _run_chunks(tab_ref, tile, e):
    cnt = tab_ref[(tile * 3 + 0) * N_EXPERTS + e]
    offs = tab_ref[(tile * 3 + 1) * N_EXPERTS + e]
    base = tab_ref[(tile * 3 + 2) * N_EXPERTS + e]
    return lax.shift_right_logical(cnt + (RUN_CHUNK - 1), RUN_CHUNK.bit_length() - 1), offs, base


def _rows(ref, row, n):
    return ref.at[pl.ds(pl.multiple_of(row * ROW_TILES, ROW_TILES), n * ROW_TILES)]


def _for_each_chunk(tab_ref, tile, fn):
    for e in range(N_EXPERTS):
        chunks, offs, base = _run_chunks(tab_ref, tile, e)

        def one(k, carry):
            fn(offs + k * RUN_CHUNK, base + k * RUN_CHUNK)
            return carry

        lax.fori_loop(0, chunks, one, 0)


def _drain_chunks(tab_ref, tile, wait_one):
    total = 0
    for e in range(N_EXPERTS):
        total = total + _run_chunks(tab_ref, tile, e)[0]

    def one(k, carry):
        wait_one()
        return carry

    lax.fori_loop(0, total, one, 0)


def _dispatch_kernel(tab_ref, ztab_ref, h_ref, loc_ref, rows_hbm, sbuf, zbuf, sems, zsem):
    i = pl.program_id(0)
    n_tiles = pl.num_programs(0)
    slot = lax.rem(i, 2)
    T = TOK_TILE

    def run_copy(s, offs_row, base_row):
        return pltpu.make_async_copy(_rows(sbuf, s * SORT_ROWS + offs_row, RUN_CHUNK),
                                     _rows(rows_hbm, base_row, RUN_CHUNK), sems.at[s])

    @pl.when(i == 0)
    def _():
        zbuf[...] = jnp.zeros(zbuf.shape, F32)

        def zero_chunk(first_row):
            return pltpu.make_async_copy(zbuf, _rows(rows_hbm, first_row, RUN_CHUNK), zsem)

        def done(k, carry):
            zero_chunk(0).wait()
            return carry

        for e in range(N_EXPERTS):
            zero_chunk(ztab_ref[e]).start()
        lax.fori_loop(0, N_EXPERTS, done, 0)
        total = 0
        for e in range(N_EXPERTS):
            start, chunks = ztab_ref[N_EXPERTS + e], ztab_ref[2 * N_EXPERTS + e]
            total = total + chunks

            def fill(k, carry, start=start):
                zero_chunk(start + k * RUN_CHUNK).start()
                return carry

            lax.fori_loop(0, chunks, fill, 0)
        lax.fori_loop(0, total, done, 0)

    locf = loc_ref[...].astype(F32)
    loc_lanes = [jnp.broadcast_to(locf[:, k:k + 1], (T, LANES)).T[0:1, :] for k in range(2)]
    pos = lax.broadcasted_iota(jnp.int32, (SORT_ROWS, T), 0).astype(F32)
    perm = jnp.where((pos == loc_lanes[0]) | (pos == loc_lanes[1]), 1.0, 0.0).astype(BF16)
    sorted_rows = _dot(perm, h_ref[...])
    for c in range(ROW_TILES):
        sbuf[pl.ds(slot * (SORT_ROWS * ROW_TILES) + c, SORT_ROWS, stride=ROW_TILES), :] = (
            sorted_rows[:, c * LANES:(c + 1) * LANES])

    @pl.when(i > 0)
    def _():
        _drain_chunks(tab_ref, i - 1, lambda: run_copy(1 - slot, 0, 0).wait())

    _for_each_chunk(tab_ref, i, lambda offs_row, base_row: run_copy(slot, offs_row, base_row).start())

    @pl.when(i == n_tiles - 1)
    def _():
        _drain_chunks(tab_ref, i, lambda: run_copy(slot, 0, 0).wait())


def _dispatch(tabs, ztab, h2, loc, n_rows):
    N = h2.shape[0]
    return pl.pallas_call(
        _dispatch_kernel,
        grid_spec=pltpu.PrefetchScalarGridSpec(
            num_scalar_prefetch=2,
            grid=(N // TOK_TILE,),
            in_specs=[pl.BlockSpec((TOK_TILE, D_MODEL), lambda i, t, z: (i, 0)),
                      pl.BlockSpec((TOK_TILE, 2), lambda i, t, z: (i, 0))],
            out_specs=pl.BlockSpec(memory_space=pl.ANY),
            scratch_shapes=[pltpu.VMEM((2 * SORT_ROWS * ROW_TILES, LANES), F32),
                            pltpu.VMEM((RUN_CHUNK * ROW_TILES, LANES), F32),
                            pltpu.SemaphoreType.DMA((2,)), pltpu.SemaphoreType.DMA(())],
        ),
        out_shape=jax.ShapeDtypeStruct((n_rows * ROW_TILES, LANES), F32),
        compiler_params=_cparams(("arbitrary",)),
        name="dispatch",
    )(tabs, ztab, h2, loc)


def _expert_kernel(be_ref, nused_ref, rows_ref, w1_ref, w3_ref, w2_ref, y_ref):
    del be_ref
    used = pl.program_id(0) < nused_ref[0]

    @pl.when(used)
    def _():
        h = jnp.concatenate([rows_ref[pl.ds(c, ROW_BLOCK, stride=ROW_TILES), :] for c in range(ROW_TILES)],
                            axis=1).astype(BF16)
        a = _dot(h, w1_ref[0].astype(BF16))
        b = _dot(h, w3_ref[0].astype(BF16))
        z = (a * jax.nn.sigmoid(a) * b).astype(BF16)
        y = _dot(z, w2_ref[0].astype(BF16))
        for c in range(ROW_TILES):
            y_ref[pl.ds(c, ROW_BLOCK, stride=ROW_TILES), :] = y[:, c * LANES:(c + 1) * LANES]

    @pl.when(jnp.logical_not(used))
    def _():
        y_ref[...] = jnp.zeros(y_ref.shape, y_ref.dtype)


def _experts(blk_expert, n_used, rows, w1, w3, w2):
    n_blk = blk_expert.shape[0]
    row_blk = lambda i, be, nu: (jnp.minimum(i, nu[0] - 1), 0)
    wspec = lambda shape: pl.BlockSpec((1,) + shape, lambda i, be, nu: (be[i], 0, 0))
    return pl.pallas_call(
        _expert_kernel,
        grid_spec=pltpu.PrefetchScalarGridSpec(
            num_scalar_prefetch=2,
            grid=(n_blk,),
            in_specs=[pl.BlockSpec((ROW_BLOCK * ROW_TILES, LANES), row_blk),
                      wspec((D_MODEL, D_EXPERT)), wspec((D_MODEL, D_EXPERT)), wspec((D_EXPERT, D_MODEL))],
            out_specs=pl.BlockSpec((ROW_BLOCK * ROW_TILES, LANES), lambda i, be, nu: (i, 0)),
        ),
        out_shape=jax.ShapeDtypeStruct((n_blk * ROW_BLOCK * ROW_TILES, LANES), F32),
        compiler_params=_cparams(("arbitrary",)),
        name="experts",
    )(blk_expert, n_used, rows, w1, w3, w2)


def _combine_kernel(tab_ref, x1_ref, gate_ref, loc_ref, y_hbm, o_ref, ybuf, sems):
    i = pl.program_id(0)
    n_tiles = pl.num_programs(0)
    slot = lax.rem(i, 2)
    T = TOK_TILE

    def run_copy(s, offs_row, base_row):
        return pltpu.make_async_copy(_rows(y_hbm, base_row, RUN_CHUNK),
                                     _rows(ybuf, s * SORT_ROWS + offs_row, RUN_CHUNK), sems.at[s])

    def fetch(tile, s):
        _for_each_chunk(tab_ref, tile, lambda offs_row, base_row: run_copy(s, offs_row, base_row).start())

    @pl.when(i == 0)
    def _():
        ybuf[...] = jnp.zeros(ybuf.shape, F32)
        fetch(0, 0)

    @pl.when(i + 1 < n_tiles)
    def _():
        fetch(i + 1, 1 - slot)

    _drain_chunks(tab_ref, i, lambda: run_copy(slot, 0, 0).wait())

    y_sorted = jnp.concatenate([ybuf[pl.ds(slot * (SORT_ROWS * ROW_TILES) + c, SORT_ROWS, stride=ROW_TILES), :]
                                for c in range(ROW_TILES)], axis=1).astype(BF16)
    gate = gate_ref[...]
    loc = loc_ref[...]
    pos = lax.broadcasted_iota(jnp.int32, (T, SORT_ROWS), 1)
    g = (jnp.where(pos == loc[:, 0:1], gate[:, 0:1], 0.0) + jnp.where(pos == loc[:, 1:2], gate[:, 1:2], 0.0))
    g_hi, g_lo = _split_bf16(g)
    o_ref[...] = x1_ref[...] + _dot(g_hi, y_sorted) + _dot(g_lo, y_sorted)


def _combine(tabs, x1, gate, loc, y_rows):
    N = gate.shape[0]
    tok = lambda width: pl.BlockSpec((TOK_TILE, width), lambda i, t: (i, 0))
    return pl.pallas_call(
        _combine_kernel,
        grid_spec=pltpu.PrefetchScalarGridSpec(
            num_scalar_prefetch=1,
            grid=(N // TOK_TILE,),
            in_specs=[tok(D_MODEL), tok(2), tok(2), pl.BlockSpec(memory_space=pl.ANY)],
            out_specs=tok(D_MODEL),
            scratch_shapes=[pltpu.VMEM((2 * SORT_ROWS * ROW_TILES, LANES), F32), pltpu.SemaphoreType.DMA((2,))],
        ),
        out_shape=jax.ShapeDtypeStruct((N, D_MODEL), F32),
        compiler_params=_cparams(("arbitrary",)),
        name="combine",
    )(tabs, x1, gate, loc, y_rows)


def _block_diag(w):
    n, a, b = w.shape
    eye = np.eye(n, dtype=np.float32)
    return (w[:, :, None, :] * eye[:, None, :, None]).reshape(n * a, n * b)


def _pad_heads(v, used):
    lead = v.shape[:-1]
    v = v.reshape(lead + (MLA_HEADS, used))
    v = jnp.pad(v, [(0, 0)] * len(lead) + [(0, 0), (0, MLA_HEAD_PAD - used)])
    return v.reshape(lead + (MLA_QK_PAD,))


def _rotary_lane_tables(seq):
    half = MLA_ROPE_DIM // 2
    inv_freq = ROPE_THETA ** (-jnp.arange(half, dtype=F32) / half)
    ang = jnp.arange(seq, dtype=F32)[:, None] * inv_freq[None, :]
    cos, sin = jnp.cos(ang), jnp.sin(ang)
    ones = jnp.ones((seq, MLA_NOPE_DIM), F32)
    tail = MLA_HEAD_PAD - MLA_QK_DIM
    cos_t = jnp.concatenate([ones, cos, cos, jnp.ones((seq, tail), F32)], axis=1)
    sin_t = jnp.concatenate([0.0 * ones, -sin, sin, jnp.zeros((seq, tail), F32)], axis=1)
    return cos_t, sin_t


def _pick_tile(n, pref):
    t = min(n, pref)
    while n % t:
        t //= 2
    return t


def kernel(x, norm1_g, w_in, lru_conv_w, lru_conv_b, lru_wa, lru_ba, lru_wx, lru_bx, lru_lambda, lru_out_g,
           diff_q_g, diff_k_g, diff_lq1, diff_lk1, diff_lq2, diff_lk2, diff_sub_g,
           mla_cq_g, mla_ckv_g, mla_w_uq, mla_w_ukv, mla_q_g, mla_k_g, mla_out_g, w_out, norm2_g,
           router_g_w, router_g_b, router_e_w, router_e_b, exp_w1, exp_w3, exp_w2):
    B, S, D = x.shape
    N = B * S
    depth = w_in.shape[0]
    assert S % KEY_TILE == 0 and S % min(QUERY_TILE, S) == 0 and D == D_MODEL
    ts = _pick_tile(S, 512)
    tc = _pick_tile(S, 512)
    assert N % TOK_TILE == 0
    tm = _pick_tile(N, 512)

    cos_t, sin_t = _rotary_lane_tables(S)
    cos_q, sin_q = cos_t.T, sin_t.T
    per_row = lambda v: jnp.broadcast_to(v.astype(F32)[:, None], (v.shape[0], LANES))
    n_dgroups = DIFF_QK_WIDTH // DIFF_HEAD_DIM
    gsum = jnp.asarray(np.kron(np.eye(n_dgroups), np.ones((DIFF_HEAD_DIM, DIFF_HEAD_DIM))), BF16)
    slopes = np.asarray([2.0 ** (-ALIBI_MAX_BIAS * (h + 1) / DIFF_HEADS) * LOG2E for h in range(DIFF_HEADS)], np.float32)
    slopes = jnp.asarray(np.pad(slopes.reshape(DIFF_HEADS // 2, 1, 2), ((0, 0), (0, 0), (0, LANES - 2))))
    place = np.zeros((MLA_ROPE_DIM, MLA_HEADS, MLA_HEAD_PAD), np.float32)
    place[np.arange(MLA_ROPE_DIM), :, MLA_NOPE_DIM + np.arange(MLA_ROPE_DIM)] = 1.0
    place = jnp.asarray(place.reshape(MLA_ROPE_DIM, MLA_QK_PAD))
    row = lambda v: v.reshape(1, -1).astype(F32)

    n_blk = -(-(2 * N + N_EXPERTS * (RUN_CHUNK + ROW_BLOCK)) // ROW_BLOCK) + 1
    n_rows = n_blk * ROW_BLOCK

    for l in range(depth):
        dqg = per_row(jnp.tile(diff_q_g[l], n_dgroups) * (DIFF_HEAD_DIM ** -0.5 * LOG2E))
        dkg = row(jnp.tile(diff_k_g[l], n_dgroups))
        wuq = _pad_heads(mla_w_uq[l], MLA_QK_DIM).astype(BF16)
        mqg = per_row(_pad_heads(jnp.tile(mla_q_g[l], MLA_HEADS) * (MLA_QK_DIM ** -0.5 * LOG2E), MLA_QK_DIM))
        mkg = row(_pad_heads(jnp.tile(mla_k_g[l], MLA_HEADS), MLA_QK_DIM))
        wukv = mla_w_ukv[l].reshape(MLA_KV_RANK, MLA_HEADS, MLA_NOPE_DIM + MLA_V_DIM)
        wk_nope = _pad_heads(wukv[:, :, :MLA_NOPE_DIM].reshape(MLA_KV_RANK, -1), MLA_NOPE_DIM)
        wk = jnp.concatenate([wk_nope, place], axis=0).astype(BF16)
        wv = wukv[:, :, MLA_NOPE_DIM:].reshape(MLA_KV_RANK, MLA_WIDTH).astype(BF16)

        lru_in, dq, dk, dv, mq, mk, mv = _in_proj(
            x, row(norm1_g[l]), w_in, l, gsum, dqg, dkg, row(mla_cq_g[l]), wuq, mqg, row(mla_ckv_g[l]), wk, mkg, wv,
            cos_t, sin_t, cos_q, sin_q, ts)

        y_lru = _lru(lru_in, lru_conv_w[l], row(lru_conv_b[l]), _block_diag(lru_wa[l]).astype(BF16),
                     row(lru_ba[l]), _block_diag(lru_wx[l]).astype(BF16), row(lru_bx[l]), row(lru_lambda[l]),
                     row(lru_out_g[l]), tc)

        lambda_init = 0.8 - 0.6 * math.exp(-0.3 * l)
        y_diff = _diff_attn(dq, dk, dv, slopes, row(diff_lq1[l]), row(diff_lk1[l]), row(diff_lq2[l]),
                            row(diff_lk2[l]), row(jnp.tile(diff_sub_g[l], 2)), lambda_init)
        y_mla = _mla_attn(mq, mk, mv)

        wr = jnp.concatenate([router_g_w[l], router_e_w[l],
                              jnp.zeros((D, ROUTER_PAD - N_GROUPS - N_EXPERTS), F32)], axis=1)
        wrh = wr.astype(BF16)
        wr_split = jnp.concatenate([wrh, (wr - wrh.astype(F32)).astype(BF16)], axis=1)
        rb = jnp.pad(jnp.concatenate([router_g_b[l], router_e_b[l]]), (0, ROUTER_PAD - N_GROUPS - N_EXPERTS))
        x1, h2, eid, gate, cnt = _out_proj(
            x.reshape(N, D), y_lru.reshape(N, LRU_WIDTH), y_diff.reshape(N, DIFF_WIDTH), y_mla.reshape(N, MLA_WIDTH),
            row(mla_out_g[l]), w_out[l].astype(BF16), row(norm2_g[l]), wr_split, row(rb), tm)

        counts = cnt[0, N_GROUPS:N_GROUPS + N_EXPERTS].astype(jnp.int32)
        padded = (counts + RUN_CHUNK + ROW_BLOCK - 1) // ROW_BLOCK * ROW_BLOCK
        pad_ends = jnp.cumsum(padded)
        pad_starts = pad_ends - padded
        blk_row0 = jnp.arange(n_blk, dtype=jnp.int32) * ROW_BLOCK
        blk_expert = jnp.minimum(jnp.sum((pad_ends[None, :] <= blk_row0[:, None]).astype(jnp.int32), axis=1),
                                 N_EXPERTS - 1)
        n_used = (pad_ends[-1:] // ROW_BLOCK).astype(jnp.int32)
        starts = jnp.pad(pad_starts.astype(F32), (0, LANES - N_EXPERTS)).reshape(1, LANES)
        tail_start = pad_starts + counts
        last = jnp.arange(N_EXPERTS) == N_EXPERTS - 1
        tail_ends = jnp.where(last, n_rows, pad_ends)
        aligned_start = (tail_start + RUN_CHUNK - 1) // RUN_CHUNK * RUN_CHUNK
        ztab = jnp.concatenate([tail_start, aligned_start, (tail_ends - aligned_start) // RUN_CHUNK]).astype(jnp.int32)

        loc, tab = _rank(eid, starts)
        tabs = tab.reshape(N // TOK_TILE, TAB_ROWS, LANES)[:, :3, :N_EXPERTS].reshape(-1)
        rows = _dispatch(tabs, ztab, h2, loc, n_rows)
        y_rows = _experts(blk_expert + l * N_EXPERTS, n_used, rows, exp_w1.reshape(-1, D, D_EXPERT),
                          exp_w3.reshape(-1, D, D_EXPERT), exp_w2.reshape(-1, D_EXPERT, D))
        x = _combine(tabs, x1, gate, loc, y_rows).reshape(B, S, D)
    return x
```

```python
import functools
import math

import jax
import jax.numpy as jnp
import numpy as np
from jax import lax
from jax.experimental import pallas as pl
from jax.experimental.pallas import tpu as pltpu

F32 = jnp.float32
BF16 = jnp.bfloat16

D_MODEL = 1024
CHUNK = 64
LRU_WIDTH = 256
CONV_WIDTH = 4
RG_C = 8.0
DIFF_HEADS = 6
DIFF_HEAD_DIM = 32
DIFF_V_DIM = 64
DIFF_QK_WIDTH = DIFF_HEADS * 2 * DIFF_HEAD_DIM
DIFF_WIDTH = DIFF_HEADS * DIFF_V_DIM
ALIBI_MAX_BIAS = 8.0
MLA_HEADS = 6
MLA_Q_RANK = 192
MLA_KV_RANK = 128
MLA_NOPE_DIM = 64
MLA_ROPE_DIM = 32
MLA_V_DIM = 64
MLA_QK_DIM = MLA_NOPE_DIM + MLA_ROPE_DIM
MLA_WIDTH = MLA_HEADS * MLA_V_DIM
ROPE_THETA = 10000.0
N_GROUPS = 4
EXPERTS_PER_GROUP = 8
N_EXPERTS = N_GROUPS * EXPERTS_PER_GROUP
D_EXPERT = 256
ROW_BLOCK = 256
EPS = 1e-6

LANES = 128
ROW_TILES = D_MODEL // LANES
MLA_HEAD_PAD = LANES
MLA_QK_PAD = MLA_HEADS * MLA_HEAD_PAD
VMEM_LIMIT = 56 * 1024 * 1024
NEG_BIG = -1e30
KEY_TILE = 256
QUERY_TILE = 512
TILES_PER_GROUP = 4
LOG2E = math.log2(math.e)

C_LRU = 0
C_DQ = 512
C_DK = 896
C_DV = 1280
C_CQ = 1664
C_CKV = 1856
C_KR = 1984
D_IN = 2016
D_IN_PAD = 2048


def _cparams(sem):
    return pltpu.CompilerParams(dimension_semantics=sem, vmem_limit_bytes=VMEM_LIMIT)


def _rms(v, width):
    return v * lax.rsqrt(jnp.sum(v * v, axis=-1, keepdims=True) * (1.0 / width) + EPS)


def _dot(a, b):
    return jnp.dot(a, b, preferred_element_type=F32)


def _split_bf16(v):
    hi = v.astype(BF16)
    lo = (v - hi.astype(F32)).astype(BF16)
    return hi, lo


def _in_proj_kernel(x_ref, g1_ref, w_ref, gsum_ref, dqg_ref, dkg_ref, cqg_ref, wuq_ref, mqg_ref, ckvg_ref,
                    wk_ref, mkg_ref, wv_ref, cos_ref, sin_ref, cosq_ref, sinq_ref,
                    lru_ref, dqt_ref, dk_ref, dvt_ref, mqt_ref, mk_ref, mvt_ref, w_bf):
    @pl.when((pl.program_id(0) == 0) & (pl.program_id(1) == 0))
    def _():
        w_bf[:, 0:D_IN] = w_ref[0].astype(BF16)
        w_bf[:, D_IN:] = jnp.zeros((D_MODEL, D_IN_PAD - D_IN), BF16)

    x = x_ref[0]
    hn = (_rms(x, D_MODEL) * g1_ref[...]).astype(BF16)
    p = _dot(hn, w_bf[...])
    lru_ref[0] = p[:, C_LRU:C_LRU + 2 * LRU_WIDTH]

    gsum = gsum_ref[...]

    def group_norm32(v, gain):
        hi, lo = _split_bf16(v * v)
        ss = _dot(hi, gsum) + _dot(lo, gsum)
        return v * lax.rsqrt(ss * (1.0 / DIFF_HEAD_DIM) + EPS) * gain

    def store_tiles(ref, vt):
        for c in range(vt.shape[1] // KEY_TILE):
            ref[0, c] = vt[:, c * KEY_TILE:(c + 1) * KEY_TILE].astype(BF16)

    t = x.shape[0]

    def along_lanes(per_feature):
        return jnp.concatenate([per_feature] * (t // LANES), axis=1)

    dq_t = p[:, C_DQ:C_DQ + DIFF_QK_WIDTH].T.reshape(DIFF_QK_WIDTH // DIFF_HEAD_DIM, DIFF_HEAD_DIM, t)
    dq_ss = jnp.sum(dq_t * dq_t, axis=1, keepdims=True) * (1.0 / DIFF_HEAD_DIM)
    dq_n = (dq_t * lax.rsqrt(dq_ss + EPS)).reshape(DIFF_QK_WIDTH, t) * along_lanes(dqg_ref[...])
    dqt_ref[0] = dq_n.astype(BF16)
    dk_ref[0] = group_norm32(p[:, C_DK:C_DK + DIFF_QK_WIDTH], dkg_ref[...]).astype(BF16)
    store_tiles(dvt_ref, p[:, C_DV:C_DV + DIFF_WIDTH].T)

    cos = cos_ref[...]
    sin = sin_ref[...]
    lane = lax.broadcasted_iota(jnp.int32, cos.shape, 1)
    first_half = lane < MLA_NOPE_DIM + MLA_ROPE_DIM // 2

    def head_norm_rotary(v, gain):
        outs = []
        for h in range(MLA_HEADS):
            c = v[:, h * MLA_HEAD_PAD:(h + 1) * MLA_HEAD_PAD]
            c = _rms(c, MLA_QK_DIM) * gain[:, h * MLA_HEAD_PAD:(h + 1) * MLA_HEAD_PAD]
            swapped = jnp.where(first_half, pltpu.roll(c, MLA_HEAD_PAD - MLA_ROPE_DIM // 2, 1),
                                pltpu.roll(c, MLA_ROPE_DIM // 2, 1))
            outs.append(c * cos + swapped * sin)
        return jnp.concatenate(outs, axis=1)

    cq = (_rms(p[:, C_CQ:C_CQ + MLA_Q_RANK], MLA_Q_RANK) * cqg_ref[...]).astype(BF16)
    q_t = _dot(cq, wuq_ref[...]).T
    cos_q, sin_q = cosq_ref[...], sinq_ref[...]
    gain_q = along_lanes(mqg_ref[...])
    half = MLA_ROPE_DIM // 2
    heads_t = []
    for h in range(MLA_HEADS):
        rows = slice(h * MLA_HEAD_PAD, (h + 1) * MLA_HEAD_PAD)
        c = q_t[rows]
        ss = jnp.sum(c * c, axis=0, keepdims=True) * (1.0 / MLA_QK_DIM)
        c = c * lax.rsqrt(ss + EPS) * gain_q[rows]
        swapped = jnp.concatenate([c[:MLA_NOPE_DIM], c[MLA_NOPE_DIM + half:MLA_QK_DIM],
                                   c[MLA_NOPE_DIM:MLA_NOPE_DIM + half], c[MLA_QK_DIM:]], axis=0)
        heads_t.append((c * cos_q + swapped * sin_q).astype(BF16))
    mqt_ref[0] = jnp.concatenate(heads_t, axis=0)

    ckv = (_rms(p[:, C_CKV:C_CKV + MLA_KV_RANK], MLA_KV_RANK) * ckvg_ref[...]).astype(BF16)
    kcat = jnp.concatenate([ckv, p[:, C_KR:C_KR + MLA_ROPE_DIM].astype(BF16)], axis=1)
    mk_ref[0] = head_norm_rotary(_dot(kcat, wk_ref[...]), mkg_ref[...]).astype(BF16)
    store_tiles(mvt_ref, _dot(ckv, wv_ref[...]).T)


def _in_proj(x, g1, w_all, layer, gsum, dqg, dkg, cqg, wuq, mqg, ckvg, wk, mkg, wv, cos_t, sin_t, cos_q, sin_q, ts):
    B, S, _ = x.shape
    full = lambda a: pl.BlockSpec(a.shape, lambda b, i: (0,) * a.ndim)
    w_spec = pl.BlockSpec((1,) + w_all.shape[1:], lambda b, i: (layer, 0, 0))
    tok = lambda width: pl.BlockSpec((1, ts, width), lambda b, i: (b, i, 0))
    tok_t = lambda width: pl.BlockSpec((1, width, ts), lambda b, i: (b, 0, i))
    nkt = ts // KEY_TILE
    tiles_t = lambda width: pl.BlockSpec((1, nkt, width, KEY_TILE), lambda b, i: (b, i, 0, 0))
    rot = pl.BlockSpec((ts, LANES), lambda b, i: (i, 0))
    rot_t = pl.BlockSpec((LANES, ts), lambda b, i: (0, i))
    shp = lambda shape, dt: jax.ShapeDtypeStruct(shape, dt)
    return pl.pallas_call(
        _in_proj_kernel,
        grid=(B, S // ts),
        in_specs=[tok(D_MODEL), full(g1), w_spec, full(gsum), full(dqg), full(dkg), full(cqg), full(wuq),
                  full(mqg), full(ckvg), full(wk), full(mkg), full(wv), rot, rot, rot_t, rot_t],
        out_specs=[tok(2 * LRU_WIDTH), tok_t(DIFF_QK_WIDTH), tok(DIFF_QK_WIDTH), tiles_t(DIFF_WIDTH),
                   tok_t(MLA_QK_PAD), tok(MLA_QK_PAD), tiles_t(MLA_WIDTH)],
        out_shape=[shp((B, S, 2 * LRU_WIDTH), F32), shp((B, DIFF_QK_WIDTH, S), BF16), shp((B, S, DIFF_QK_WIDTH), BF16),
                   shp((B, S // KEY_TILE, DIFF_WIDTH, KEY_TILE), BF16), shp((B, MLA_QK_PAD, S), BF16),
                   shp((B, S, MLA_QK_PAD), BF16), shp((B, S // KEY_TILE, MLA_WIDTH, KEY_TILE), BF16)],
        scratch_shapes=[pltpu.VMEM((D_MODEL, D_IN_PAD), BF16)],
        compiler_params=_cparams(("arbitrary", "arbitrary")),
        name="in_proj",
    )(x, g1, w_all, gsum, dqg, dkg, cqg, wuq, mqg, ckvg, wk, mkg, wv, cos_t, sin_t, cos_q, sin_q)


LRU_HALO = 8


def _lru_kernel(blk_ref, cw_ref, cb_ref, wa_ref, ba_ref, wx_ref, bx_ref, lam_ref, og_ref, y_ref,
                ubuf, a_scr, b_scr, h_scr):
    B, tc, _ = blk_ref.shape
    W = LRU_WIDTH

    @pl.when(pl.program_id(0) == 0)
    def _():
        ubuf[:, 0:LRU_HALO, :] = jnp.zeros((B, LRU_HALO, W), F32)
        h_scr[...] = jnp.zeros(h_scr.shape, F32)

    ubuf[:, LRU_HALO:, :] = blk_ref[:, :, 0:W]
    xc = cb_ref[...][None]
    for j in range(CONV_WIDTH):
        off = LRU_HALO - (CONV_WIDTH - 1) + j
        xc = xc + cw_ref[j:j + 1, :][None] * ubuf[:, off:off + tc, :]
    ubuf[:, 0:LRU_HALO, :] = ubuf[:, tc:tc + LRU_HALO, :]

    x2 = xc.reshape(B * tc, W)
    xb = x2.astype(BF16)
    r = jax.nn.sigmoid(_dot(xb, wa_ref[...]) + ba_ref[...])
    gi = jax.nn.sigmoid(_dot(xb, wx_ref[...]) + bx_ref[...])
    nl = -lam_ref[...]
    softplus = jnp.maximum(nl, 0.0) + jnp.log(1.0 + jnp.exp(-jnp.abs(nl)))
    a = jnp.exp(-RG_C * r * softplus)
    bb = jnp.sqrt(1.0 - a * a) * gi * x2
    a_scr[...] = a.reshape(B, tc, W)
    b_scr[...] = bb.reshape(B, tc, W)

    def step(t, h):
        h = a_scr[:, pl.ds(t, 1), :] * h + b_scr[:, pl.ds(t, 1), :]
        b_scr[:, pl.ds(t, 1), :] = h
        return h

    h_scr[...] = lax.fori_loop(0, tc, step, h_scr[...], unroll=8)

    gate = blk_ref[:, :, W:2 * W]
    gelu = 0.5 * gate * (1.0 + jnp.tanh(math.sqrt(2.0 / math.pi) * (gate + 0.044715 * gate * gate * gate)))
    y = b_scr[...] * gelu
    y_ref[...] = (_rms(y, W) * og_ref[...][None]).astype(y_ref.dtype)


def _lru(lru_in, cw, cb, wa, ba, wx, bx, lam, og, tc):
    B, S, _ = lru_in.shape
    W = LRU_WIDTH
    full = lambda a: pl.BlockSpec(a.shape, lambda i: (0,) * a.ndim)
    return pl.pallas_call(
        _lru_kernel,
        grid=(S // tc,),
        in_specs=[pl.BlockSpec((B, tc, 2 * W), lambda i: (0, i, 0)), full(cw), full(cb), full(wa), full(ba),
                  full(wx), full(bx), full(lam), full(og)],
        out_specs=pl.BlockSpec((B, tc, W), lambda i: (0, i, 0)),
        out_shape=jax.ShapeDtypeStruct((B, S, W), BF16),
        scratch_shapes=[pltpu.VMEM((B, tc + LRU_HALO, W), F32), pltpu.VMEM((B, tc, W), F32),
                        pltpu.VMEM((B, tc, W), F32), pltpu.VMEM((B, 1, W), F32)],
        compiler_params=_cparams(("arbitrary",)),
        name="lru",
    )(lru_in, cw, cb, wa, ba, wx, bx, lam, og)


def _flash_maps(qt_list, k_ref, vt_ref, k_lanes, bias_of, tile_shift, m_scr, l_scr, acc_scr, tiles_per_step):
    qi = pl.program_id(2)
    n_maps = len(qt_list)
    tq = qt_list[0].shape[1]
    maps = range(n_maps)

    own = tq // KEY_TILE
    first_own = own * qi
    last = first_own + own - 1

    def biased_scores(j):
        jc = jnp.minimum(j, last)
        start = pl.multiple_of(jc * KEY_TILE, KEY_TILE)
        kind = jnp.clip(jc - first_own + 1, 0, own)
        return tuple(_dot(k_ref[0, pl.ds(start, KEY_TILE), k_lanes[i]], qt_list[i]) + bias_of(i, kind)
                     for i in maps)

    for i in maps:
        m_scr[i] = jnp.full((1, tq), NEG_BIG, F32)
        l_scr[i] = jnp.zeros((1, tq), F32)
        acc_scr[i] = jnp.zeros(acc_scr.shape[1:], F32)

    def update(j, s_tile):
        vt = vt_ref[0, jnp.minimum(j, last)]
        for i in maps:
            s = s_tile[i]
            c = tile_shift(i, first_own - j)
            m_old = m_scr[i]
            m_new = jnp.maximum(m_old, jnp.max(s, axis=0, keepdims=True) + c)
            alpha = jnp.exp2(m_old - m_new)
            p = jnp.exp2(s - (m_new - c))
            m_scr[i] = m_new
            l_scr[i] = alpha * l_scr[i] + jnp.sum(p, axis=0, keepdims=True)
            acc_scr[i] = alpha * acc_scr[i] + _dot(vt, p.astype(BF16))

    def one_tile(j, s_tile):
        s_ahead = biased_scores(j + 1)
        update(j, s_tile)
        return s_ahead

    def group_of(size, first):
        def body(t, s_tile):
            for u in range(size):
                s_tile = one_tile(first + size * t + u, s_tile)
            return s_tile
        return body

    rest_size = math.gcd(tiles_per_step, own)
    n_groups = (last + 1) // tiles_per_step
    s_rest = lax.fori_loop(0, n_groups, group_of(tiles_per_step, 0), biased_scores(0))
    done = n_groups * tiles_per_step
    lax.fori_loop(0, (last + 1 - done) // rest_size, group_of(rest_size, done), s_rest)


def _own_tile_geometry(tq, u):
    krow = lax.broadcasted_iota(jnp.int32, (KEY_TILE, tq), 0) + u * KEY_TILE
    qcol = lax.broadcasted_iota(jnp.int32, (KEY_TILE, tq), 1)
    return krow.astype(F32), qcol.astype(F32), (krow // CHUNK) <= (qcol // CHUNK)


def _diff_attn_kernel(qt_ref, k_ref, vt_ref, slope_ref, lq1_ref, lk1_ref, lq2_ref, lk2_ref, subg_ref, o_ref,
                      m_scr, l_scr, acc_scr, bias_scr, *, lambda_init, tq):
    qt = qt_ref[0]
    feat = lax.broadcasted_iota(jnp.int32, qt.shape, 0)
    zero = jnp.zeros_like(qt)
    qt_list = [jnp.where(feat // DIFF_HEAD_DIM == i, qt, zero) for i in range(4)]
    slope = [slope_ref[0, 0:1, hh:hh + 1] for hh in range(2)]

    @pl.when(pl.program_id(2) == 0)
    def _():
        for hh in range(2):
            bias_scr[hh, 0] = slope[hh] * _own_tile_geometry(tq, 0)[0]
            for u in range(tq // KEY_TILE):
                kf, qf, allowed = _own_tile_geometry(tq, u)
                own_bias = slope[hh] * (qf - jnp.abs(qf - kf) - float(u * KEY_TILE))
                bias_scr[hh, u + 1] = jnp.where(allowed, own_bias, NEG_BIG)

    def tile_shift(i, n_tiles):
        return -slope[i // 2] * (n_tiles * KEY_TILE).astype(F32)

    _flash_maps(qt_list, k_ref, vt_ref, [slice(None)] * 4, lambda i, kind: bias_scr[i // 2, kind],
                tile_shift, m_scr, l_scr, acc_scr, TILES_PER_GROUP)

    lam = (jnp.exp(jnp.sum(lq1_ref[...] * lk1_ref[...], axis=-1, keepdims=True))
           - jnp.exp(jnp.sum(lq2_ref[...] * lk2_ref[...], axis=-1, keepdims=True)) + lambda_init)
    orow = lax.broadcasted_iota(jnp.int32, (LANES, tq), 0)
    out_t = jnp.zeros((LANES, tq), F32)
    for hh in range(2):
        o = (acc_scr[2 * hh] * (1.0 / l_scr[2 * hh])
             - acc_scr[2 * hh + 1] * (lam / l_scr[2 * hh + 1]))
        mine = (orow // DIFF_V_DIM) == hh
        ms = jnp.sum(jnp.where(mine, o * o, 0.0), axis=0, keepdims=True) * (1.0 / DIFF_V_DIM)
        out_t = jnp.where(mine, o * lax.rsqrt(ms + EPS), out_t)
    o_ref[0] = (out_t.T * subg_ref[...] * (1.0 - lambda_init)).astype(o_ref.dtype)


def _diff_attn(dqt, dk, dvt, slopes, lq1, lk1, lq2, lk2, subg, lambda_init):
    B, S, _ = dk.shape
    tq = min(QUERY_TILE, S)
    n_pairs = DIFF_HEADS // 2
    full = lambda a: pl.BlockSpec(a.shape, lambda b, p, i: (0,) * a.ndim)
    kern = functools.partial(_diff_attn_kernel, lambda_init=lambda_init, tq=tq)
    return pl.pallas_call(
        kern,
        grid=(B, n_pairs, S // tq),
        in_specs=[pl.BlockSpec((1, LANES, tq), lambda b, p, i: (b, p, i)),
                  pl.BlockSpec((1, S, LANES), lambda b, p, i: (b, 0, p)),
                  pl.BlockSpec((1, S // KEY_TILE, LANES, KEY_TILE), lambda b, p, i: (b, 0, p, 0)),
                  pl.BlockSpec((1, 1, LANES), lambda b, p, i: (p, 0, 0)),
                  full(lq1), full(lk1), full(lq2), full(lk2), full(subg)],
        out_specs=pl.BlockSpec((1, tq, LANES), lambda b, p, i: (b, i, p)),
        out_shape=jax.ShapeDtypeStruct((B, S, DIFF_WIDTH), BF16),
        scratch_shapes=[pltpu.VMEM((4, 1, tq), F32), pltpu.VMEM((4, 1, tq), F32), pltpu.VMEM((4, LANES, tq), F32),
                        pltpu.VMEM((2, 1 + tq // KEY_TILE, KEY_TILE, tq), F32)],
        compiler_params=_cparams(("parallel", "parallel", "arbitrary")),
        name="diff_attn",
    )(dqt, dk, dvt, slopes, lq1, lk1, lq2, lk2, subg)


def _mla_attn_kernel(qt_ref, k_ref, vt_ref, o_ref, m_scr, l_scr, acc_scr, bias_scr, *, tq):
    qt = qt_ref[0]
    qt_list = [qt[hh * MLA_HEAD_PAD:(hh + 1) * MLA_HEAD_PAD, :] for hh in range(2)]
    k_lanes = [slice(hh * MLA_HEAD_PAD, (hh + 1) * MLA_HEAD_PAD) for hh in range(2)]

    @pl.when(pl.program_id(2) == 0)
    def _():
        bias_scr[0] = jnp.zeros((KEY_TILE, tq), F32)
        for u in range(tq // KEY_TILE):
            bias_scr[u + 1] = jnp.where(_own_tile_geometry(tq, u)[2], 0.0, NEG_BIG)

    _flash_maps(qt_list, k_ref, vt_ref, k_lanes, lambda i, kind: bias_scr[kind], lambda i, n: 0.0,
                m_scr, l_scr, acc_scr, TILES_PER_GROUP)
    orow = lax.broadcasted_iota(jnp.int32, (LANES, tq), 0)
    out_t = jnp.where(orow < MLA_V_DIM, acc_scr[0] * (1.0 / l_scr[0]), acc_scr[1] * (1.0 / l_scr[1]))
    o_ref[0] = out_t.T.astype(o_ref.dtype)


def _mla_attn(mqt, mk, mvt):
    B, S, _ = mk.shape
    tq = min(QUERY_TILE, S)
    n_pairs = MLA_HEADS // 2
    kern = functools.partial(_mla_attn_kernel, tq=tq)
    return pl.pallas_call(
        kern,
        grid=(B, n_pairs, S // tq),
        in_specs=[pl.BlockSpec((1, 2 * MLA_HEAD_PAD, tq), lambda b, p, i: (b, p, i)),
                  pl.BlockSpec((1, S, 2 * MLA_HEAD_PAD), lambda b, p, i: (b, 0, p)),
                  pl.BlockSpec((1, S // KEY_TILE, LANES, KEY_TILE), lambda b, p, i: (b, 0, p, 0))],
        out_specs=pl.BlockSpec((1, tq, LANES), lambda b, p, i: (b, i, p)),
        out_shape=jax.ShapeDtypeStruct((B, S, MLA_WIDTH), F32),
        scratch_shapes=[pltpu.VMEM((2, 1, tq), F32), pltpu.VMEM((2, 1, tq), F32), pltpu.VMEM((2, LANES, tq), F32),
                        pltpu.VMEM((1 + tq // KEY_TILE, KEY_TILE, tq), F32)],
        compiler_params=_cparams(("parallel", "parallel", "arbitrary")),
        name="mla_attn",
    )(mqt, mk, mvt)


ROUTER_PAD = LANES


def _out_proj_kernel(x_ref, ylru_ref, ydiff_ref, ymla_ref, mlag_ref, wo_ref, g2_ref, wr_ref, rb_ref,
                     x1_ref, h2_ref, eid_ref, gate_ref, cnt_ref):
    @pl.when(pl.program_id(0) == 0)
    def _():
        cnt_ref[...] = jnp.zeros(cnt_ref.shape, F32)

    ymla = (_rms(ymla_ref[...], MLA_WIDTH) * mlag_ref[...]).astype(BF16)
    x1 = (x_ref[...]
          + _dot(ylru_ref[...], wo_ref[0:LRU_WIDTH, :])
          + _dot(ydiff_ref[...], wo_ref[LRU_WIDTH:LRU_WIDTH + DIFF_WIDTH, :])
          + _dot(ymla, wo_ref[LRU_WIDTH + DIFF_WIDTH:, :]))
    x1_ref[...] = x1

    h2 = _rms(x1, D_MODEL) * g2_ref[...]
    hi, lo = _split_bf16(h2)
    h2_ref[...] = hi
    both = _dot(hi, wr_ref[...])
    logits = (both[:, :ROUTER_PAD] + both[:, ROUTER_PAD:] + _dot(lo, wr_ref[:, 0:ROUTER_PAD]) + rb_ref[...])

    lane = lax.broadcasted_iota(jnp.int32, logits.shape, 1)
    big = jnp.int32(1 << 20)
    is_group = lane < N_GROUPS
    gl = jnp.where(is_group, logits, NEG_BIG)
    gmax = jnp.max(gl, axis=-1, keepdims=True)
    g_idx = jnp.min(jnp.where(gl == gmax, lane, big), axis=-1, keepdims=True)
    pg_top = 1.0 / jnp.sum(jnp.where(is_group, jnp.exp(gl - gmax), 0.0), axis=-1, keepdims=True)

    e_lane = lane - N_GROUPS
    in_group = (e_lane >= 0) & (e_lane < N_EXPERTS) & ((e_lane // EXPERTS_PER_GROUP) == g_idx)
    el = jnp.where(in_group, logits, NEG_BIG)
    m1 = jnp.max(el, axis=-1, keepdims=True)
    i1 = jnp.min(jnp.where(el == m1, lane, big), axis=-1, keepdims=True)
    el2 = jnp.where(lane == i1, NEG_BIG, el)
    m2 = jnp.max(el2, axis=-1, keepdims=True)
    i2 = jnp.min(jnp.where(el2 == m2, lane, big), axis=-1, keepdims=True)
    e2 = jnp.exp(m2 - m1)
    g1 = pg_top / (1.0 + e2)
    g2 = pg_top * e2 / (1.0 + e2)

    two = lax.broadcasted_iota(jnp.int32, eid_ref.shape, 1)
    eid_ref[...] = jnp.where(two == 0, i1 - N_GROUPS, i2 - N_GROUPS)
    gate_ref[...] = jnp.where(two == 0, g1, g2)
    onehot = ((e_lane == i1 - N_GROUPS) | (e_lane == i2 - N_GROUPS)).astype(F32)
    cnt_ref[...] += jnp.sum(onehot, axis=0, keepdims=True)


def _out_proj(x, ylru, ydiff, ymla, mlag, wo, g2, wr, rb, tm):
    N = x.shape[0]
    full = lambda a: pl.BlockSpec(a.shape, lambda i: (0,) * a.ndim)
    tok = lambda width: pl.BlockSpec((tm, width), lambda i: (i, 0))
    return pl.pallas_call(
        _out_proj_kernel,
        grid=(N // tm,),
        in_specs=[tok(D_MODEL), tok(LRU_WIDTH), tok(DIFF_WIDTH), tok(MLA_WIDTH), full(mlag), full(wo), full(g2),
                  full(wr), full(rb)],
        out_specs=[tok(D_MODEL), tok(D_MODEL), tok(2), tok(2), pl.BlockSpec((1, ROUTER_PAD), lambda i: (0, 0))],
        out_shape=[jax.ShapeDtypeStruct((N, D_MODEL), F32), jax.ShapeDtypeStruct((N, D_MODEL), BF16),
                   jax.ShapeDtypeStruct((N, 2), jnp.int32),
                   jax.ShapeDtypeStruct((N, 2), F32), jax.ShapeDtypeStruct((1, ROUTER_PAD), F32)],
        compiler_params=_cparams(("arbitrary",)),
        name="out_proj",
    )(x, ylru, ydiff, ymla, mlag, wo, g2, wr, rb)


TOK_TILE = 256
RUN_CHUNK = 8
TAB_ROWS = 8


def _rank_kernel(eid_ref, start_ref, loc_ref, tab_ref, carry):
    tm = eid_ref.shape[0]

    @pl.when(pl.program_id(0) == 0)
    def _():
        carry[...] = start_ref[...]

    lane = lax.broadcasted_iota(jnp.int32, (tm, LANES), 1)
    r = lax.broadcasted_iota(jnp.int32, (tm, tm), 0)
    c = lax.broadcasted_iota(jnp.int32, (tm, tm), 1)
    lower = (c < r).astype(BF16)
    eid = eid_ref[...]
    onehot = [lane == eid[:, k:k + 1] for k in range(2)]
    cnt = [jnp.sum(oh.astype(F32), axis=0, keepdims=True) for oh in onehot]
    cnt_tile = cnt[0] + cnt[1]

    chunks = jnp.floor((cnt_tile + (RUN_CHUNK - 1)) * (1.0 / RUN_CHUNK))
    lane8 = lax.broadcasted_iota(jnp.int32, (TAB_ROWS, LANES), 1)
    incl = jnp.broadcast_to(chunks, (TAB_ROWS, LANES))
    shift = 1
    while shift < LANES:
        incl = incl + jnp.where(lane8 >= shift, pltpu.roll(incl, shift, 1), 0.0)
        shift *= 2
    offs = (incl[0:1] - chunks) * RUN_CHUNK

    locs = []
    for k in range(2):
        before = _dot(lower, onehot[k].astype(BF16))
        first = offs if k == 0 else offs + cnt[0]
        locs.append(jnp.sum(jnp.where(onehot[k], before + first, 0.0), axis=-1, keepdims=True))
    two = lax.broadcasted_iota(jnp.int32, loc_ref.shape, 1)
    loc_ref[...] = jnp.where(two == 0, locs[0], locs[1]).astype(jnp.int32)

    row8 = lax.broadcasted_iota(jnp.int32, (TAB_ROWS, LANES), 0)
    base = carry[...]
    tab = jnp.where(row8 == 0, cnt_tile, jnp.where(row8 == 1, offs, jnp.where(row8 == 2, base, 0.0)))
    tab_ref[...] = tab.astype(jnp.int32)
    carry[...] = base + cnt_tile


def _rank(eid, starts):
    N = eid.shape[0]
    n_tiles = N // TOK_TILE
    return pl.pallas_call(
        _rank_kernel,
        grid=(n_tiles,),
        in_specs=[pl.BlockSpec((TOK_TILE, 2), lambda i: (i, 0)), pl.BlockSpec((1, LANES), lambda i: (0, 0))],
        out_specs=[pl.BlockSpec((TOK_TILE, 2), lambda i: (i, 0)), pl.BlockSpec((TAB_ROWS, LANES), lambda i: (i, 0))],
        out_shape=[jax.ShapeDtypeStruct((N, 2), jnp.int32),
                   jax.ShapeDtypeStruct((n_tiles * TAB_ROWS, LANES), jnp.int32)],
        scratch_shapes=[pltpu.VMEM((1, LANES), F32)],
        compiler_params=_cparams(("arbitrary",)),
        name="rank",
    )(eid, starts)


SORT_ROWS = 2 * TOK_TILE + N_EXPERTS * RUN_CHUNK


def _run_chunks(tab_ref, tile, e):
    cnt = tab_ref[(tile * 3 + 0) * N_EXPERTS + e]
    offs = tab_ref[(tile * 3 + 1) * N_EXPERTS + e]
    base = tab_ref[(tile * 3 + 2) * N_EXPERTS + e]
    return lax.shift_right_logical(cnt + (RUN_CHUNK - 1), RUN_CHUNK.bit_length() - 1), offs, base


def _rows(ref, row, n):
    return ref.at[pl.ds(pl.multiple_of(row * ROW_TILES, ROW_TILES), n * ROW_TILES)]


def _for_each_chunk(tab_ref, tile, fn):
    for e in range(N_EXPERTS):
        chunks, offs, base = _run_chunks(tab_ref, tile, e)

        def one(k, carry, offs=offs, base=base, queue=e % 2):
            fn(offs + k * RUN_CHUNK, base + k * RUN_CHUNK, queue)
            return carry

        lax.fori_loop(0, chunks, one, 0)


def _drain_chunks(tab_ref, tile, wait_one):
    total = 0
    for e in range(N_EXPERTS):
        total = total + _run_chunks(tab_ref, tile, e)[0]

    def one(k, carry):
        wait_one()
        return carry

    lax.fori_loop(0, total, one, 0)


def _dispatch_kernel(tab_ref, ztab_ref, h_ref, loc_ref, rows_hbm, sbuf, zbuf, sems, zsem):
    i = pl.program_id(0)
    n_tiles = pl.num_programs(0)
    slot = lax.rem(i, 2)
    T = TOK_TILE

    def run_copy(s, offs_row, base_row):
        return pltpu.make_async_copy(_rows(sbuf, s * SORT_ROWS + offs_row, RUN_CHUNK),
                                     _rows(rows_hbm, base_row, RUN_CHUNK), sems.at[s])

    @pl.when(i == 0)
    def _():
        zbuf[...] = jnp.zeros(zbuf.shape, F32)

        def zero_chunk(first_row):
            return pltpu.make_async_copy(zbuf, _rows(rows_hbm, first_row, RUN_CHUNK), zsem)

        def done(k, carry):
            zero_chunk(0).wait()
            return carry

        for e in range(N_EXPERTS):
            zero_chunk(ztab_ref[e]).start()
        lax.fori_loop(0, N_EXPERTS, done, 0)
        total = 0
        for e in range(N_EXPERTS):
            start, chunks = ztab_ref[N_EXPERTS + e], ztab_ref[2 * N_EXPERTS + e]
            total = total + chunks

            def fill(k, carry, start=start):
                zero_chunk(start + k * RUN_CHUNK).start()
                return carry

            lax.fori_loop(0, chunks, fill, 0)
        lax.fori_loop(0, total, done, 0)

    locf = loc_ref[...].astype(F32)
    loc_lanes = [jnp.broadcast_to(locf[:, k:k + 1], (T, LANES)).T[0:1, :] for k in range(2)]
    pos = lax.broadcasted_iota(jnp.int32, (SORT_ROWS, T), 0).astype(F32)
    perm = jnp.where((pos == loc_lanes[0]) | (pos == loc_lanes[1]), 1.0, 0.0).astype(BF16)
    sorted_rows = _dot(perm, h_ref[...])
    for c in range(ROW_TILES):
        sbuf[pl.ds(slot * (SORT_ROWS * ROW_TILES) + c, SORT_ROWS, stride=ROW_TILES), :] = (
            sorted_rows[:, c * LANES:(c + 1) * LANES])

    @pl.when(i > 0)
    def _():
        _drain_chunks(tab_ref, i - 1, lambda: run_copy(1 - slot, 0, 0).wait())

    _for_each_chunk(tab_ref, i, lambda offs_row, base_row, queue:
                    run_copy(slot, offs_row, base_row).start(priority=queue))

    @pl.when(i == n_tiles - 1)
    def _():
        _drain_chunks(tab_ref, i, lambda: run_copy(slot, 0, 0).wait())


def _dispatch(tabs, ztab, h2, loc, n_rows):
    N = h2.shape[0]
    return pl.pallas_call(
        _dispatch_kernel,
        grid_spec=pltpu.PrefetchScalarGridSpec(
            num_scalar_prefetch=2,
            grid=(N // TOK_TILE,),
            in_specs=[pl.BlockSpec((TOK_TILE, D_MODEL), lambda i, t, z: (i, 0)),
                      pl.BlockSpec((TOK_TILE, 2), lambda i, t, z: (i, 0))],
            out_specs=pl.BlockSpec(memory_space=pl.ANY),
            scratch_shapes=[pltpu.VMEM((2 * SORT_ROWS * ROW_TILES, LANES), F32),
                            pltpu.VMEM((RUN_CHUNK * ROW_TILES, LANES), F32),
                            pltpu.SemaphoreType.DMA((2,)), pltpu.SemaphoreType.DMA(())],
        ),
        out_shape=jax.ShapeDtypeStruct((n_rows * ROW_TILES, LANES), F32),
        compiler_params=_cparams(("arbitrary",)),
        name="dispatch",
    )(tabs, ztab, h2, loc)


def _expert_kernel(be_ref, nused_ref, rows_ref, w1_ref, w3_ref, w2_ref, y_ref):
    del be_ref
    used = pl.program_id(0) < nused_ref[0]

    @pl.when(used)
    def _():
        h = jnp.concatenate([rows_ref[pl.ds(c, ROW_BLOCK, stride=ROW_TILES), :] for c in range(ROW_TILES)],
                            axis=1).astype(BF16)
        a = _dot(h, w1_ref[0].astype(BF16))
        b = _dot(h, w3_ref[0].astype(BF16))
        z = (a * jax.nn.sigmoid(a) * b).astype(BF16)
        y = _dot(z, w2_ref[0].astype(BF16))
        for c in range(ROW_TILES):
            y_ref[pl.ds(c, ROW_BLOCK, stride=ROW_TILES), :] = y[:, c * LANES:(c + 1) * LANES]

    @pl.when(jnp.logical_not(used))
    def _():
        y_ref[...] = jnp.zeros(y_ref.shape, y_ref.dtype)


def _experts(blk_expert, n_used, rows, w1, w3, w2):
    n_blk = blk_expert.shape[0]
    row_blk = lambda i, be, nu: (jnp.minimum(i, nu[0] - 1), 0)
    wspec = lambda shape: pl.BlockSpec((1,) + shape, lambda i, be, nu: (be[i], 0, 0))
    return pl.pallas_call(
        _expert_kernel,
        grid_spec=pltpu.PrefetchScalarGridSpec(
            num_scalar_prefetch=2,
            grid=(n_blk,),
            in_specs=[pl.BlockSpec((ROW_BLOCK * ROW_TILES, LANES), row_blk),
                      wspec((D_MODEL, D_EXPERT)), wspec((D_MODEL, D_EXPERT)), wspec((D_EXPERT, D_MODEL))],
            out_specs=pl.BlockSpec((ROW_BLOCK * ROW_TILES, LANES), lambda i, be, nu: (i, 0)),
        ),
        out_shape=jax.ShapeDtypeStruct((n_blk * ROW_BLOCK * ROW_TILES, LANES), F32),
        compiler_params=_cparams(("arbitrary",)),
        name="experts",
    )(blk_expert, n_used, rows, w1, w3, w2)


def _combine_kernel(tab_ref, x1_ref, gate_ref, loc_ref, y_hbm, o_ref, ybuf, sems):
    i = pl.program_id(0)
    n_tiles = pl.num_programs(0)
    slot = lax.rem(i, 2)
    T = TOK_TILE

    def run_copy(s, offs_row, base_row):
        return pltpu.make_async_copy(_rows(y_hbm, base_row, RUN_CHUNK),
                                     _rows(ybuf, s * SORT_ROWS + offs_row, RUN_CHUNK), sems.at[s])

    def fetch(tile, s):
        _for_each_chunk(tab_ref, tile, lambda offs_row, base_row, queue:
                        run_copy(s, offs_row, base_row).start(priority=queue))

    @pl.when(i == 0)
    def _():
        ybuf[...] = jnp.zeros(ybuf.shape, F32)
        fetch(0, 0)

    @pl.when(i + 1 < n_tiles)
    def _():
        fetch(i + 1, 1 - slot)

    _drain_chunks(tab_ref, i, lambda: run_copy(slot, 0, 0).wait())

    y_sorted = jnp.concatenate([ybuf[pl.ds(slot * (SORT_ROWS * ROW_TILES) + c, SORT_ROWS, stride=ROW_TILES), :]
                                for c in range(ROW_TILES)], axis=1).astype(BF16)
    gate = gate_ref[...]
    loc = loc_ref[...]
    pos = lax.broadcasted_iota(jnp.int32, (T, SORT_ROWS), 1)
    g = (jnp.where(pos == loc[:, 0:1], gate[:, 0:1], 0.0) + jnp.where(pos == loc[:, 1:2], gate[:, 1:2], 0.0))
    g_hi, g_lo = _split_bf16(g)
    o_ref[...] = x1_ref[...] + _dot(g_hi, y_sorted) + _dot(g_lo, y_sorted)


def _combine(tabs, x1, gate, loc, y_rows):
    N = gate.shape[0]
    tok = lambda width: pl.BlockSpec((TOK_TILE, width), lambda i, t: (i, 0))
    return pl.pallas_call(
        _combine_kernel,
        grid_spec=pltpu.PrefetchScalarGridSpec(
            num_scalar_prefetch=1,
            grid=(N // TOK_TILE,),
            in_specs=[tok(D_MODEL), tok(2), tok(2), pl.BlockSpec(memory_space=pl.ANY)],
            out_specs=tok(D_MODEL),
            scratch_shapes=[pltpu.VMEM((2 * SORT_ROWS * ROW_TILES, LANES), F32), pltpu.SemaphoreType.DMA((2,))],
        ),
        out_shape=jax.ShapeDtypeStruct((N, D_MODEL), F32),
        compiler_params=_cparams(("arbitrary",)),
        name="combine",
    )(tabs, x1, gate, loc, y_rows)


def _block_diag(w):
    n, a, b = w.shape
    eye = np.eye(n, dtype=np.float32)
    return (w[:, :, None, :] * eye[:, None, :, None]).reshape(n * a, n * b)


def _pad_heads(v, used):
    lead = v.shape[:-1]
    v = v.reshape(lead + (MLA_HEADS, used))
    v = jnp.pad(v, [(0, 0)] * len(lead) + [(0, 0), (0, MLA_HEAD_PAD - used)])
    return v.reshape(lead + (MLA_QK_PAD,))


def _rotary_lane_tables(seq):
    half = MLA_ROPE_DIM // 2
    inv_freq = ROPE_THETA ** (-jnp.arange(half, dtype=F32) / half)
    ang = jnp.arange(seq, dtype=F32)[:, None] * inv_freq[None, :]
    cos, sin = jnp.cos(ang), jnp.sin(ang)
    ones = jnp.ones((seq, MLA_NOPE_DIM), F32)
    tail = MLA_HEAD_PAD - MLA_QK_DIM
    cos_t = jnp.concatenate([ones, cos, cos, jnp.ones((seq, tail), F32)], axis=1)
    sin_t = jnp.concatenate([0.0 * ones, -sin, sin, jnp.zeros((seq, tail), F32)], axis=1)
    return cos_t, sin_t


def _pick_tile(n, pref):
    t = min(n, pref)
    while n % t:
        t //= 2
    return t


def kernel(x, norm1_g, w_in, lru_conv_w, lru_conv_b, lru_wa, lru_ba, lru_wx, lru_bx, lru_lambda, lru_out_g,
           diff_q_g, diff_k_g, diff_lq1, diff_lk1, diff_lq2, diff_lk2, diff_sub_g,
           mla_cq_g, mla_ckv_g, mla_w_uq, mla_w_ukv, mla_q_g, mla_k_g, mla_out_g, w_out, norm2_g,
           router_g_w, router_g_b, router_e_w, router_e_b, exp_w1, exp_w3, exp_w2):
    B, S, D = x.shape
    N = B * S
    depth = w_in.shape[0]
    assert S % KEY_TILE == 0 and S % min(QUERY_TILE, S) == 0 and D == D_MODEL
    ts = _pick_tile(S, 512)
    tc = _pick_tile(S, 512)
    assert N % TOK_TILE == 0
    tm = _pick_tile(N, 512)

    cos_t, sin_t = _rotary_lane_tables(S)
    cos_q, sin_q = cos_t.T, sin_t.T
    per_row = lambda v: jnp.broadcast_to(v.astype(F32)[:, None], (v.shape[0], LANES))
    n_dgroups = DIFF_QK_WIDTH // DIFF_HEAD_DIM
    gsum = jnp.asarray(np.kron(np.eye(n_dgroups), np.ones((DIFF_HEAD_DIM, DIFF_HEAD_DIM))), BF16)
    slopes = np.asarray([2.0 ** (-ALIBI_MAX_BIAS * (h + 1) / DIFF_HEADS) * LOG2E for h in range(DIFF_HEADS)], np.float32)
    slopes = jnp.asarray(np.pad(slopes.reshape(DIFF_HEADS // 2, 1, 2), ((0, 0), (0, 0), (0, LANES - 2))))
    place = np.zeros((MLA_ROPE_DIM, MLA_HEADS, MLA_HEAD_PAD), np.float32)
    place[np.arange(MLA_ROPE_DIM), :, MLA_NOPE_DIM + np.arange(MLA_ROPE_DIM)] = 1.0
    place = jnp.asarray(place.reshape(MLA_ROPE_DIM, MLA_QK_PAD))
    row = lambda v: v.reshape(1, -1).astype(F32)

    n_blk = -(-(2 * N + N_EXPERTS * (RUN_CHUNK + ROW_BLOCK)) // ROW_BLOCK) + 1
    n_rows = n_blk * ROW_BLOCK

    for l in range(depth):
        dqg = per_row(jnp.tile(diff_q_g[l], n_dgroups) * (DIFF_HEAD_DIM ** -0.5 * LOG2E))
        dkg = row(jnp.tile(diff_k_g[l], n_dgroups))
        wuq = _pad_heads(mla_w_uq[l], MLA_QK_DIM).astype(BF16)
        mqg = per_row(_pad_heads(jnp.tile(mla_q_g[l], MLA_HEADS) * (MLA_QK_DIM ** -0.5 * LOG2E), MLA_QK_DIM))
        mkg = row(_pad_heads(jnp.tile(mla_k_g[l], MLA_HEADS), MLA_QK_DIM))
        wukv = mla_w_ukv[l].reshape(MLA_KV_RANK, MLA_HEADS, MLA_NOPE_DIM + MLA_V_DIM)
        wk_nope = _pad_heads(wukv[:, :, :MLA_NOPE_DIM].reshape(MLA_KV_RANK, -1), MLA_NOPE_DIM)
        wk = jnp.concatenate([wk_nope, place], axis=0).astype(BF16)
        wv = wukv[:, :, MLA_NOPE_DIM:].reshape(MLA_KV_RANK, MLA_WIDTH).astype(BF16)

        lru_in, dq, dk, dv, mq, mk, mv = _in_proj(
            x, row(norm1_g[l]), w_in, l, gsum, dqg, dkg, row(mla_cq_g[l]), wuq, mqg, row(mla_ckv_g[l]), wk, mkg, wv,
            cos_t, sin_t, cos_q, sin_q, ts)

        y_lru = _lru(lru_in, lru_conv_w[l], row(lru_conv_b[l]), _block_diag(lru_wa[l]).astype(BF16),
                     row(lru_ba[l]), _block_diag(lru_wx[l]).astype(BF16), row(lru_bx[l]), row(lru_lambda[l]),
                     row(lru_out_g[l]), tc)

        lambda_init = 0.8 - 0.6 * math.exp(-0.3 * l)
        y_diff = _diff_attn(dq, dk, dv, slopes, row(diff_lq1[l]), row(diff_lk1[l]), row(diff_lq2[l]),
                            row(diff_lk2[l]), row(jnp.tile(diff_sub_g[l], 2)), lambda_init)
        y_mla = _mla_attn(mq, mk, mv)

        wr = jnp.concatenate([router_g_w[l], router_e_w[l],
                              jnp.zeros((D, ROUTER_PAD - N_GROUPS - N_EXPERTS), F32)], axis=1)
        wrh = wr.astype(BF16)
        wr_split = jnp.concatenate([wrh, (wr - wrh.astype(F32)).astype(BF16)], axis=1)
        rb = jnp.pad(jnp.concatenate([router_g_b[l], router_e_b[l]]), (0, ROUTER_PAD - N_GROUPS - N_EXPERTS))
        x1, h2, eid, gate, cnt = _out_proj(
            x.reshape(N, D), y_lru.reshape(N, LRU_WIDTH), y_diff.reshape(N, DIFF_WIDTH), y_mla.reshape(N, MLA_WIDTH),
            row(mla_out_g[l]), w_out[l].astype(BF16), row(norm2_g[l]), wr_split, row(rb), tm)

        counts = cnt[0, N_GROUPS:N_GROUPS + N_EXPERTS].astype(jnp.int32)
        padded = (counts + RUN_CHUNK + ROW_BLOCK - 1) // ROW_BLOCK * ROW_BLOCK
        pad_ends = jnp.cumsum(padded)
        pad_starts = pad_ends - padded
        blk_row0 = jnp.arange(n_blk, dtype=jnp.int32) * ROW_BLOCK
        blk_expert = jnp.minimum(jnp.sum((pad_ends[None, :] <= blk_row0[:, None]).astype(jnp.int32), axis=1),
                                 N_EXPERTS - 1)
        n_used = (pad_ends[-1:] // ROW_BLOCK).astype(jnp.int32)
        starts = jnp.pad(pad_starts.astype(F32), (0, LANES - N_EXPERTS)).reshape(1, LANES)
        tail_start = pad_starts + counts
        last = jnp.arange(N_EXPERTS) == N_EXPERTS - 1
        tail_ends = jnp.where(last, n_rows, pad_ends)
        aligned_start = (tail_start + RUN_CHUNK - 1) // RUN_CHUNK * RUN_CHUNK
        ztab = jnp.concatenate([tail_start, aligned_start, (tail_ends - aligned_start) // RUN_CHUNK]).astype(jnp.int32)

        loc, tab = _rank(eid, starts)
        tabs = tab.reshape(N // TOK_TILE, TAB_ROWS, LANES)[:, :3, :N_EXPERTS].reshape(-1)
        rows = _dispatch(tabs, ztab, h2, loc, n_rows)
        y_rows = _experts(blk_expert + l * N_EXPERTS, n_used, rows, exp_w1.reshape(-1, D, D_EXPERT),
                          exp_w3.reshape(-1, D, D_EXPERT), exp_w2.reshape(-1, D_EXPERT, D))
        x = _combine(tabs, x1, gate, loc, y_rows).reshape(B, S, D)
    return x
```
